```python
import jax, jax.numpy as jnp
from jax import lax
import numpy as np

D_MODEL = 1024
BATCH = 4
SEQ = 4096
DEPTH = 2
DEC_BATCH = 32
DEC_SEQ = 16
PAST_LEN = 1024

CHUNK = 64
Q_BLOCK = 128
N_MIXERS = 2
SB_HEADS = 16
SB_HEAD_DIM = D_MODEL // SB_HEADS
RW_HEAD_DIM = 64
RW_HEADS = D_MODEL // RW_HEAD_DIM
DECAY_LORA = 64
AAA_LORA = 64
GATE_LORA = 160
N_EXPERTS = 16
N_GROUPS = 4
EXPERTS_PER_GROUP = N_EXPERTS // N_GROUPS
TOP_K = 2
D_EXPERT = 512
NORM_EPS = 1e-6
LNX_EPS = 64e-5

kernel_name = 'hybrid_stickbreak_rwkv7_grouped_moe_stream_step'


def _rmsnorm(x, g):
    x32 = x.astype(jnp.float32)
    y = x32 * lax.rsqrt(jnp.mean(x32 * x32, axis=-1, keepdims=True) + NORM_EPS)
    return (y * g.astype(jnp.float32)).astype(x.dtype)


def _stick_breaking_attend(q, k, v, q_pos, k_pos):
    z = jnp.einsum('bqhd,bkhd->bhqk', q, k).astype(jnp.float32) * (SB_HEAD_DIM ** -0.5)
    visible = (k_pos[None, :] < q_pos[:, None])[None, None]
    log_beta = jax.nn.log_sigmoid(z)
    log_keep = jnp.where(visible, jax.nn.log_sigmoid(-z), 0.0)
    suffix = lax.cumsum(log_keep, axis=3, reverse=True)
    suffix = jnp.concatenate([suffix[..., 1:], jnp.zeros_like(suffix[..., :1])], axis=-1)
    weights = jnp.where(visible, jnp.exp(log_beta + suffix), 0.0)
    return jnp.einsum('bhqk,bkhd->bqhd', weights.astype(v.dtype), v)


def _stick_breaking_mixer(h, w_qkv, w_o, k_past, v_past):
    bsz, seq, _ = h.shape
    q, k, v = jnp.split(h @ w_qkv, 3, axis=-1)
    q = q.reshape(bsz, seq, SB_HEADS, SB_HEAD_DIM)
    k = k.reshape(bsz, seq, SB_HEADS, SB_HEAD_DIM)
    v = v.reshape(bsz, seq, SB_HEADS, SB_HEAD_DIM)
    if k_past is None:
        outs = []
        for b0 in range(0, seq, Q_BLOCK):
            b1 = min(b0 + Q_BLOCK, seq)
            outs.append(_stick_breaking_attend(q[:, b0:b1], k[:, :b1], v[:, :b1],
                                               jnp.arange(b0, b1), jnp.arange(b1)))
        o = jnp.concatenate(outs, axis=1)
    else:
        past = k_past.shape[1]
        keys = jnp.concatenate([k_past.astype(k.dtype), k], axis=1)
        vals = jnp.concatenate([v_past.astype(v.dtype), v], axis=1)
        o = _stick_breaking_attend(q, keys, vals, past + jnp.arange(seq), jnp.arange(past + seq))
    return o.reshape(bsz, seq, D_MODEL) @ w_o, k, v


def _rwkv7_scan(r, decay, k, v, a, b, s0):
    def step(state, inp):
        r_t, w_t, k_t, v_t, a_t, b_t = inp
        sa = jnp.einsum('bhij,bhj->bhi', state, a_t)
        state = (state * w_t[:, :, None, :] + sa[..., None] * b_t[:, :, None, :]
                 + v_t[..., None] * k_t[:, :, None, :])
        return state, jnp.einsum('bhij,bhj->bhi', state, r_t)
    xs = tuple(jnp.swapaxes(t, 0, 1) for t in (r, decay, k, v, a, b))
    state, ys = lax.scan(step, s0, xs)
    return jnp.swapaxes(ys, 0, 1), state


def _rwkv7_mixer(h, shift_prev, s0, p):
    bsz, seq, _ = h.shape
    f32 = jnp.float32
    heads = lambda t: t.astype(f32).reshape(bsz, seq, RW_HEADS, RW_HEAD_DIM)
    xx = jnp.concatenate([shift_prev.astype(h.dtype), h[:, :-1]], axis=1) - h
    xr, xw, xk, xv, xa, xg = [h + xx * p['rw_mu'][j] for j in range(6)]
    r = xr @ p['rw_w_r']
    k = xk @ p['rw_w_k']
    v = xv @ p['rw_w_v']
    w = -jax.nn.softplus(-(p['rw_w0'] + jnp.tanh(xw @ p['rw_w1']) @ p['rw_w2']).astype(f32)) - 0.5
    decay = jnp.exp(-jnp.exp(w))
    a = jax.nn.sigmoid((p['rw_a0'] + (xa @ p['rw_a1']) @ p['rw_a2']).astype(f32))
    g = jax.nn.sigmoid(xg @ p['rw_g1']) @ p['rw_g2']
    kk = heads(k * p['rw_k_k'])
    kk = kk / jnp.maximum(jnp.sqrt(jnp.sum(kk * kk, axis=-1, keepdims=True)), 1e-12)
    k = k.astype(f32) * (1.0 + (a - 1.0) * p['rw_k_a'].astype(f32))
    r_h, k_h, v_h, a_h = heads(r), heads(k), heads(v), heads(a)
    y, state = _rwkv7_scan(r_h, heads(decay), k_h, v_h, -kk, kk * a_h, s0.astype(f32))
    mu = jnp.mean(y, axis=-1, keepdims=True)
    var = jnp.mean(jnp.square(y - mu), axis=-1, keepdims=True)
    y = (y - mu) * lax.rsqrt(var + LNX_EPS)
    y = (y * p['rw_lnx_g'].astype(f32).reshape(RW_HEADS, RW_HEAD_DIM)
         + p['rw_lnx_b'].astype(f32).reshape(RW_HEADS, RW_HEAD_DIM))
    bonus = jnp.sum(r_h * k_h * p['rw_r_k'].astype(f32), axis=-1, keepdims=True) * v_h
    out = ((y + bonus).reshape(bsz, seq, D_MODEL).astype(h.dtype) * g) @ p['rw_w_o']
    return out, h[:, -1:], state


def _grouped_moe(h, router_w, router_bias, w_gate, w_up, w_down):
    bsz, seq, _ = h.shape
    t = h.reshape(-1, D_MODEL)
    s = jax.nn.sigmoid((t @ router_w).astype(jnp.float32))
    sel = (s + router_bias.astype(jnp.float32)).reshape(-1, N_GROUPS, EXPERTS_PER_GROUP)
    group_score = jnp.sum(lax.top_k(sel, TOP_K)[0], axis=-1)
    gidx = jnp.argmax(group_score, axis=-1)
    within = jnp.take_along_axis(sel, gidx[:, None, None], axis=1)[:, 0]
    _, loc = lax.top_k(within, TOP_K)
    eidx = gidx[:, None] * EXPERTS_PER_GROUP + loc
    wsel = jnp.take_along_axis(s, eidx, axis=-1)
    wsel = wsel / jnp.sum(wsel, axis=-1, keepdims=True)
    gates = jnp.sum(jax.nn.one_hot(eidx, N_EXPERTS, dtype=jnp.float32) * wsel[..., None], axis=1)
    gates = gates.astype(t.dtype)
    out = jnp.zeros_like(t)
    for e in range(N_EXPERTS):
        he = jax.nn.silu(t @ w_gate[e]) * (t @ w_up[e])
        out = out + gates[:, e:e + 1] * (he @ w_down[e])
    return out.reshape(bsz, seq, D_MODEL)


def _run_trunk(x, c, p, sb_k_past, sb_v_past, rw_shift0, rw_s0):
    sb_k = sb_v = rw_state = rw_shift = None
    for i in range(DEPTH):
        mod = (jax.nn.silu(c) @ p['ada_w'][i] + p['ada_b'][i])[:, None, :]
        sh1, sc1, g1, sh2, sc2, g2 = jnp.split(mod, 6, axis=-1)
        h = _rmsnorm(x, p['norm_mix_g'][i]) * (1 + sc1) + sh1
        if i % N_MIXERS == 0:
            mix, sb_k, sb_v = _stick_breaking_mixer(h, p['sb_w_qkv'], p['sb_w_o'], sb_k_past, sb_v_past)
        else:
            mix, rw_shift, rw_state = _rwkv7_mixer(h, rw_shift0, rw_s0, p)
        x = x + (g1 * mix).astype(x.dtype)
        h = _rmsnorm(x, p['norm_ffn_g'][i]) * (1 + sc2) + sh2
        ffn = _grouped_moe(h, p['router_w'], p['router_bias'],
                           p['exp_w_gate'][i], p['exp_w_up'][i], p['exp_w_down'][i])
        x = x + (g2 * ffn).astype(x.dtype)
    return _rmsnorm(x, p['final_norm_g']), sb_k, sb_v, rw_state, rw_shift


def setup_inputs(seed: int = 0) -> dict:
    key = jax.random.key(seed)
    ks = iter(jax.random.split(key, 48))
    f32 = jnp.float32
    nrm = lambda shape, scale: jax.random.normal(next(ks), shape, f32) * scale
    D = D_MODEL
    return {
        'x_prompt': nrm((BATCH, SEQ, D), 1.0),
        'x_sample': nrm((DEC_BATCH, DEC_SEQ, D), 1.0),
        'cache_sb_k': nrm((DEC_BATCH, PAST_LEN, SB_HEADS, SB_HEAD_DIM), 1.0),
        'cache_sb_v': nrm((DEC_BATCH, PAST_LEN, SB_HEADS, SB_HEAD_DIM), 1.0),
        'state_rw_wkv': nrm((DEC_BATCH, RW_HEADS, RW_HEAD_DIM, RW_HEAD_DIM), 0.5),
        'state_rw_shift': nrm((DEC_BATCH, 1, D), 1.0),
        'c_prompt': nrm((BATCH, D), 1.0),
        'c_sample': nrm((DEC_BATCH, D), 1.0),
        'ada_w': nrm((DEPTH, D, 6 * D), 0.5 * D ** -0.5),
        'ada_b': nrm((DEPTH, 6 * D), 0.02),
        'norm_mix_g': 1.0 + nrm((DEPTH, D), 0.02),
        'norm_ffn_g': 1.0 + nrm((DEPTH, D), 0.02),
        'final_norm_g': 1.0 + nrm((D,), 0.02),
        'sb_w_qkv': nrm((D, 3 * D), D ** -0.5),
        'sb_w_o': nrm((D, D), D ** -0.5),
        'rw_mu': jax.random.uniform(next(ks), (6, D), f32, 0.0, 1.0),
        'rw_w_r': nrm((D, D), D ** -0.5),
        'rw_w_k': nrm((D, D), D ** -0.5),
        'rw_w_v': nrm((D, D), D ** -0.5),
        'rw_w0': jax.random.uniform(next(ks), (D,), f32, -5.0, 0.0),
        'rw_w1': nrm((D, DECAY_LORA), D ** -0.5),
        'rw_w2': nrm((DECAY_LORA, D), 0.5 * DECAY_LORA ** -0.5),
        'rw_a0': nrm((D,), 0.3),
        'rw_a1': nrm((D, AAA_LORA), D ** -0.5),
        'rw_a2': nrm((AAA_LORA, D), 0.5 * AAA_LORA ** -0.5),
        'rw_g1': nrm((D, GATE_LORA), D ** -0.5),
        'rw_g2': nrm((GATE_LORA, D), GATE_LORA ** -0.5),
        'rw_k_k': 0.85 + nrm((D,), 0.02),
        'rw_k_a': 1.0 + nrm((D,), 0.02),
        'rw_r_k': nrm((RW_HEADS, RW_HEAD_DIM), 0.1),
        'rw_lnx_g': 1.0 + nrm((D,), 0.02),
        'rw_lnx_b': nrm((D,), 0.02),
        'rw_w_o': nrm((D, D), D ** -0.5),
        'router_w': nrm((D, N_EXPERTS), D ** -0.5),
        'router_bias': nrm((N_EXPERTS,), 0.01),
        'exp_w_gate': nrm((DEPTH, N_EXPERTS, D, D_EXPERT), D ** -0.5),
        'exp_w_up': nrm((DEPTH, N_EXPERTS, D, D_EXPERT), D ** -0.5),
        'exp_w_down': nrm((DEPTH, N_EXPERTS, D_EXPERT, D), D_EXPERT ** -0.5),
    }


def reference(x_prompt, x_sample, cache_sb_k, cache_sb_v, state_rw_wkv, state_rw_shift,
              c_prompt, c_sample, ada_w, ada_b, norm_mix_g, norm_ffn_g, final_norm_g,
              sb_w_qkv, sb_w_o, rw_mu, rw_w_r, rw_w_k, rw_w_v, rw_w0, rw_w1, rw_w2,
              rw_a0, rw_a1, rw_a2, rw_g1, rw_g2, rw_k_k, rw_k_a, rw_r_k, rw_lnx_g, rw_lnx_b,
              rw_w_o, router_w, router_bias, exp_w_gate, exp_w_up, exp_w_down):
    assert x_sample.shape[1] <= CHUNK
    p = {
        'ada_w': ada_w, 'ada_b': ada_b, 'norm_mix_g': norm_mix_g, 'norm_ffn_g': norm_ffn_g,
        'final_norm_g': final_norm_g, 'sb_w_qkv': sb_w_qkv, 'sb_w_o': sb_w_o,
        'rw_mu': rw_mu, 'rw_w_r': rw_w_r, 'rw_w_k': rw_w_k, 'rw_w_v': rw_w_v,
        'rw_w0': rw_w0, 'rw_w1': rw_w1, 'rw_w2': rw_w2, 'rw_a0': rw_a0, 'rw_a1': rw_a1,
        'rw_a2': rw_a2, 'rw_g1': rw_g1, 'rw_g2': rw_g2, 'rw_k_k': rw_k_k, 'rw_k_a': rw_k_a,
        'rw_r_k': rw_r_k, 'rw_lnx_g': rw_lnx_g, 'rw_lnx_b': rw_lnx_b, 'rw_w_o': rw_w_o,
        'router_w': router_w, 'router_bias': router_bias,
        'exp_w_gate': exp_w_gate, 'exp_w_up': exp_w_up, 'exp_w_down': exp_w_down,
    }
    bsz = x_prompt.shape[0]
    shift0 = jnp.zeros((bsz, 1, D_MODEL), x_prompt.dtype)
    s0 = jnp.zeros((bsz, RW_HEADS, RW_HEAD_DIM, RW_HEAD_DIM), jnp.float32)
    y_prompt, sb_k_prompt, sb_v_prompt, rw_wkv_prompt, rw_shift_prompt = _run_trunk(
        x_prompt, c_prompt, p, None, None, shift0, s0)
    y_sample, sb_k_sample, sb_v_sample, rw_wkv_sample, rw_shift_sample = _run_trunk(
        x_sample, c_sample, p, cache_sb_k, cache_sb_v, state_rw_shift, state_rw_wkv)
    return (y_prompt, y_sample, sb_k_prompt, sb_v_prompt, sb_k_sample, sb_v_sample,
            rw_wkv_prompt, rw_shift_prompt, rw_wkv_sample, rw_shift_sample)
```

```python
import functools

import jax
import jax.numpy as jnp
from jax import lax
from jax.experimental import pallas as pl
from jax.experimental.pallas import tpu as pltpu

F32 = jnp.float32
BF16 = jnp.bfloat16

D_MODEL = 1024
HEADS = 16
HEAD_DIM = 64
LANES = 128
SUBLANES = 8
PAIRS = D_MODEL // LANES
N_EXPERTS = 16
N_GROUPS = 4
EXPERTS_PER_GROUP = 4
D_EXPERT = 512
NORM_EPS = 1e-6
LNX_EPS = 64e-5
TOKEN_TILE = 512
RWKV_PRE_TILE = 256
VMEM_LIMIT = 56 * 1024 * 1024

_NT = (((1,), (1,)), ((), ()))


def _cparams(*sem):
    return pltpu.CompilerParams(dimension_semantics=sem, vmem_limit_bytes=VMEM_LIMIT)


def _sigmoid(x):
    return 1.0 / (1.0 + jnp.exp(-x))


def _split2(x):
    hi = x.astype(BF16)
    lo = (x - hi.astype(F32)).astype(BF16)
    return hi, lo


def _dot(a, b):
    return jnp.dot(a, b, preferred_element_type=F32)


def _dot2(x, w2):
    hi, lo = _split2(x)
    return _dot(jnp.concatenate([hi, lo], axis=1), w2)


def _norm_mod(x, g, sc, sh):
    ms = jnp.mean(x * x, axis=-1, keepdims=True)
    y = x * lax.rsqrt(ms + NORM_EPS) * g
    return y * (1.0 + sc) + sh


def _seq_blocks(seqs, length, tile=TOKEN_TILE):
    if length >= tile:
        assert length % tile == 0
        return 1, tile
    assert tile % length == 0 and seqs % (tile // length) == 0 and length % 16 == 0
    return tile // length, length


def _ada_kernel(c_ref, w_ref, b_ref, o_ref):
    c = c_ref[...]
    s = c * _sigmoid(c)
    o_ref[0] = jnp.dot(s, w_ref[0], precision=lax.Precision.HIGHEST,
                       preferred_element_type=F32) + b_ref[0]


def _ada(c, ada_w, ada_b):
    rows = c.shape[0]
    depth = ada_w.shape[0]
    return pl.pallas_call(
        _ada_kernel,
        out_shape=jax.ShapeDtypeStruct((depth, rows, 6 * D_MODEL), F32),
        grid=(depth, 6),
        in_specs=[pl.BlockSpec((rows, D_MODEL), lambda l, j: (0, 0)),
                  pl.BlockSpec((1, D_MODEL, D_MODEL), lambda l, j: (l, 0, j)),
                  pl.BlockSpec((1, 1, D_MODEL), lambda l, j: (l, 0, j))],
        out_specs=pl.BlockSpec((1, rows, D_MODEL), lambda l, j: (l, 0, j)),
        compiler_params=_cparams("arbitrary", "arbitrary"),
        name="ada_mod",
    )(c, ada_w, ada_b.reshape(depth, 1, 6 * D_MODEL))


def _qkv_kernel(x_ref, g_ref, sc_ref, sh_ref, w_ref, q_ref, k_ref, v_ref, kb_ref, vb_ref):
    bs, tl, _ = x_ref.shape
    h = _norm_mod(x_ref[...], g_ref[...], sc_ref[...], sh_ref[...])
    h2 = h.reshape(bs * tl, D_MODEL).astype(BF16)
    qkv = _dot(h2, w_ref[...])
    q = qkv[:, :D_MODEL] * (HEAD_DIM ** -0.5)
    k = qkv[:, D_MODEL:2 * D_MODEL]
    v = qkv[:, 2 * D_MODEL:]
    q_ref[...] = q.astype(BF16).reshape(bs, tl, D_MODEL)
    k_ref[...] = k.reshape(bs, tl, D_MODEL)
    v_ref[...] = v.reshape(bs, tl, D_MODEL)
    kb_ref[...] = k.astype(BF16).reshape(bs, tl, D_MODEL)
    vb_ref[...] = v.astype(BF16).reshape(bs, tl, D_MODEL)


def _qkv(x, g, sc, sh, w_bf):
    seqs, length, _ = x.shape
    bs, tl = _seq_blocks(seqs, length)
    tok = pl.BlockSpec((bs, tl, D_MODEL), lambda s, i: (s, i, 0))
    mod = pl.BlockSpec((bs, 1, D_MODEL), lambda s, i: (s, 0, 0))
    f32o = jax.ShapeDtypeStruct((seqs, length, D_MODEL), F32)
    bfo = jax.ShapeDtypeStruct((seqs, length, D_MODEL), BF16)
    return pl.pallas_call(
        _qkv_kernel,
        out_shape=(bfo, f32o, f32o, bfo, bfo),
        grid=(seqs // bs, length // tl),
        in_specs=[tok, pl.BlockSpec((1, 1, D_MODEL), lambda s, i: (0, 0, 0)), mod, mod,
                  pl.BlockSpec((D_MODEL, 3 * D_MODEL), lambda s, i: (0, 0))],
        out_specs=(tok, tok, tok, tok, tok),
        compiler_params=_cparams("arbitrary", "arbitrary"),
        name="sb_qkv",
    )(x, g.reshape(1, 1, D_MODEL), sc, sh, w_bf)


def _pair_masks(rows):
    lane = lax.broadcasted_iota(jnp.int32, (rows, LANES), 1)
    return lane < HEAD_DIM


def _suffix_matrix():
    r = lax.broadcasted_iota(jnp.int32, (LANES, LANES), 0)
    c = lax.broadcasted_iota(jnp.int32, (LANES, LANES), 1)
    u = jnp.where(r > c, 1.0, 0.0).astype(BF16)
    return jnp.concatenate([u, u], axis=0)


def _sb_tile(q2, k, v, carry, acc, u2, visible):
    tq = q2.shape[0] // 2
    z = lax.dot_general(q2, k, _NT, preferred_element_type=F32)
    sp = jnp.maximum(z, 0.0) + jnp.log(1.0 + jnp.exp(-jnp.abs(z)))
    if visible is not None:
        sp = jnp.where(visible, sp, 0.0)
    later = _dot2(sp, u2)
    logw = z - sp - later - carry
    w = jnp.exp(logw)
    if visible is not None:
        w = jnp.where(visible, w, 0.0)
    wb = w.astype(BF16)
    mask_a = _pair_masks(k.shape[0])
    v2 = jnp.concatenate([jnp.where(mask_a, v, 0), jnp.where(mask_a, 0, v)], axis=0)
    w2 = jnp.concatenate([wb[:tq], wb[tq:]], axis=1)
    acc = acc + _dot(w2, v2)
    carry = carry + jnp.sum(sp, axis=-1, keepdims=True)
    return carry, acc


def _stack_heads(q):
    mask_a = _pair_masks(q.shape[0])
    return jnp.concatenate([jnp.where(mask_a, q, 0), jnp.where(mask_a, 0, q)], axis=0)


def _attn_prompt_kernel(q_ref, k_ref, v_ref, o_ref):
    tq = q_ref.shape[1]
    qi = pl.program_id(2)
    q2 = _stack_heads(q_ref[0])
    u2 = _suffix_matrix()
    row = lax.broadcasted_iota(jnp.int32, (2 * tq, LANES), 0)
    col = lax.broadcasted_iota(jnp.int32, (2 * tq, LANES), 1)
    qpos = jnp.where(row >= tq, row - tq, row)
    visible = col < qpos
    carry = jnp.zeros((2 * tq, LANES), F32)
    acc = jnp.zeros((tq, LANES), F32)
    base = pl.multiple_of(qi * tq, tq)
    carry, acc = _sb_tile(q2, k_ref[0, pl.ds(base, tq), :], v_ref[0, pl.ds(base, tq), :],
                          carry, acc, u2, visible)

    def body(jj, state):
        start = pl.multiple_of((qi - 1 - jj) * tq, tq)
        return _sb_tile(q2, k_ref[0, pl.ds(start, tq), :], v_ref[0, pl.ds(start, tq), :],
                        state[0], state[1], u2, None)

    carry, acc = lax.fori_loop(0, qi, body, (carry, acc))
    o_ref[0] = acc.astype(BF16)


def _attn_prompt(q_bf, k_bf, v_bf):
    batch, length, _ = q_bf.shape
    tq = LANES
    qspec = pl.BlockSpec((1, tq, LANES), lambda b, p, i: (b, i, p))
    kspec = pl.BlockSpec((1, length, LANES), lambda b, p, i: (b, 0, p))
    return pl.pallas_call(
        _attn_prompt_kernel,
        out_shape=jax.ShapeDtypeStruct((batch, length, D_MODEL), BF16),
        grid=(batch, PAIRS, length // tq),
        in_specs=[qspec, kspec, kspec],
        out_specs=qspec,
        compiler_params=_cparams("arbitrary", "arbitrary", "arbitrary"),
        name="sb_attn_prompt",
    )(q_bf, k_bf, v_bf)


def _attn_sample_kernel(q_ref, k_ref, v_ref, kc_ref, vc_ref, o_ref):
    tq = q_ref.shape[1]
    past = kc_ref.shape[1]
    q2 = _stack_heads(q_ref[0])
    u2 = _suffix_matrix()
    row = lax.broadcasted_iota(jnp.int32, (2 * tq, LANES), 0)
    col = lax.broadcasted_iota(jnp.int32, (2 * tq, LANES), 1)
    qpos = jnp.where(row >= tq, row - tq, row)
    visible = col < qpos
    pad = jnp.zeros((LANES - tq, LANES), BF16)
    k_new = jnp.concatenate([k_ref[0], pad], axis=0)
    v_new = jnp.concatenate([v_ref[0], pad], axis=0)
    carry = jnp.zeros((2 * tq, LANES), F32)
    acc = jnp.zeros((tq, LANES), F32)
    carry, acc = _sb_tile(q2, k_new, v_new, carry, acc, u2, visible)
    for j in reversed(range(past // LANES)):
        kc = kc_ref[0, j * LANES:(j + 1) * LANES, :].astype(BF16)
        vc = vc_ref[0, j * LANES:(j + 1) * LANES, :].astype(BF16)
        carry, acc = _sb_tile(q2, kc, vc, carry, acc, u2, None)
    o_ref[0] = acc.astype(BF16)


def _attn_sample(q_bf, k_bf, v_bf, k_cache, v_cache):
    batch, length, _ = q_bf.shape
    past = k_cache.shape[1]
    assert length <= LANES and past % LANES == 0
    qspec = pl.BlockSpec((1, length, LANES), lambda b, p: (b, 0, p))
    cspec = pl.BlockSpec((1, past, LANES), lambda b, p: (b, 0, p))
    return pl.pallas_call(
        _attn_sample_kernel,
        out_shape=jax.ShapeDtypeStruct((batch, length, D_MODEL), BF16),
        grid=(batch, PAIRS),
        in_specs=[qspec, qspec, qspec, cspec, cspec],
        out_specs=qspec,
        compiler_params=_cparams("arbitrary", "arbitrary"),
        name="sb_attn_sample",
    )(q_bf, k_bf, v_bf, k_cache, v_cache)


def _row_max(rows):
    m = rows[0]
    for r in rows[1:]:
        m = jnp.maximum(m, r)
    return m


def _first_hit(rows, m):
    hits = []
    taken = None
    for r in rows:
        eq = r == m
        hit = eq if taken is None else jnp.logical_and(eq, jnp.logical_not(taken))
        taken = eq if taken is None else jnp.logical_or(taken, eq)
        hits.append(hit)
    return hits


def _route(s_rows, sel_rows):
    neg = jnp.float32(-jnp.inf)
    top1, top2, score = [], [], []
    for g in range(N_GROUPS):
        rows = sel_rows[g * EXPERTS_PER_GROUP:(g + 1) * EXPERTS_PER_GROUP]
        m1 = _row_max(rows)
        h1 = _first_hit(rows, m1)
        rest = [jnp.where(h, neg, r) for h, r in zip(h1, rows)]
        m2 = _row_max(rest)
        h2 = _first_hit(rest, m2)
        top1.append(h1)
        top2.append(h2)
        score.append(m1 + m2)
    best = _row_max(score)
    ghit = _first_hit(score, best)
    picked = []
    for g in range(N_GROUPS):
        for e in range(EXPERTS_PER_GROUP):
            picked.append(jnp.logical_and(ghit[g], jnp.logical_or(top1[g][e], top2[g][e])))
    total = None
    for p, s in zip(picked, s_rows):
        term = jnp.where(p, s, 0.0)
        total = term if total is None else total + term
    return [jnp.where(p, s / total, 0.0) for p, s in zip(picked, s_rows)]


def _oproj_kernel(m_ref, x_ref, gate_ref, w_ref, g_ref, sc_ref, sh_ref, rw_ref, rb_ref,
                  x1_ref, h_ref, gates_ref):
    bs, tl, _ = x_ref.shape
    m = m_ref[...].reshape(bs * tl, D_MODEL)
    mix = _dot(m, w_ref[...]).reshape(bs, tl, D_MODEL)
    x1 = x_ref[...] + gate_ref[...] * mix
    x1_ref[...] = x1
    h = _norm_mod(x1, g_ref[...], sc_ref[...], sh_ref[...]).reshape(bs * tl, D_MODEL)
    h_ref[...] = h.astype(BF16).reshape(bs, tl, D_MODEL)
    logits = lax.dot_general(rw_ref[...], h, _NT, precision=lax.Precision.HIGHEST,
                             preferred_element_type=F32)
    s = _sigmoid(logits)
    sel = s + rb_ref[...]
    s_rows = [s[e:e + 1, :] for e in range(N_EXPERTS)]
    sel_rows = [sel[e:e + 1, :] for e in range(N_EXPERTS)]
    gates_ref[0] = jnp.concatenate(_route(s_rows, sel_rows), axis=0)


def _oproj_router(m_bf, x, gate, w_bf, g, sc, sh, router_wt, router_b):
    seqs, length, _ = x.shape
    bs, tl = _seq_blocks(seqs, length)
    n_tiles = (seqs // bs) * (length // tl)
    per_seq = length // tl
    tok = pl.BlockSpec((bs, tl, D_MODEL), lambda s, i: (s, i, 0))
    mod = pl.BlockSpec((bs, 1, D_MODEL), lambda s, i: (s, 0, 0))
    one = pl.BlockSpec((1, 1, D_MODEL), lambda s, i: (0, 0, 0))
    x1, h, gates = pl.pallas_call(
        _oproj_kernel,
        out_shape=(jax.ShapeDtypeStruct((seqs, length, D_MODEL), F32),
                   jax.ShapeDtypeStruct((seqs, length, D_MODEL), BF16),
                   jax.ShapeDtypeStruct((n_tiles, N_EXPERTS, bs * tl), F32)),
        grid=(seqs // bs, per_seq),
        in_specs=[tok, tok, mod, pl.BlockSpec((D_MODEL, D_MODEL), lambda s, i: (0, 0)),
                  one, mod, mod,
                  pl.BlockSpec((N_EXPERTS, D_MODEL), lambda s, i: (0, 0)),
                  pl.BlockSpec((N_EXPERTS, 1), lambda s, i: (0, 0))],
        out_specs=(tok, tok,
                   pl.BlockSpec((1, N_EXPERTS, bs * tl), lambda s, i: (s * per_seq + i, 0, 0))),
        compiler_params=_cparams("arbitrary", "arbitrary"),
        name="oproj_router",
    )(m_bf, x, gate, w_bf, g.reshape(1, 1, D_MODEL), sc, sh, router_wt,
      router_b.reshape(N_EXPERTS, 1))
    gates = jnp.swapaxes(gates, 1, 2).reshape(seqs, length, N_EXPERTS)
    return x1, h, gates


def _moe_kernel(h_ref, gates_ref, x_ref, gate_ref, fg_ref, wg_ref, wu_ref, wd_ref,
                o_ref, acc_ref, *, final_norm):
    bs, tl, _ = x_ref.shape
    e = pl.program_id(2)
    h = h_ref[...].reshape(bs * tl, D_MODEL)
    a = _dot(h, wg_ref[0])
    u = _dot(h, wu_ref[0])
    he = (a * _sigmoid(a) * u).astype(BF16)
    out = _dot(he, wd_ref[0])
    gates = gates_ref[...].reshape(bs * tl, N_EXPERTS)
    lane = lax.broadcasted_iota(jnp.int32, gates.shape, 1)
    ge = jnp.sum(jnp.where(lane == e, gates, 0.0), axis=-1, keepdims=True)
    contrib = ge * out

    @pl.when(e == 0)
    def _():
        acc_ref[...] = contrib

    @pl.when(e > 0)
    def _():
        acc_ref[...] += contrib

    @pl.when(e == N_EXPERTS - 1)
    def _():
        x2 = x_ref[...] + gate_ref[...] * acc_ref[...].reshape(bs, tl, D_MODEL)
        if final_norm:
            ms = jnp.mean(x2 * x2, axis=-1, keepdims=True)
            x2 = x2 * lax.rsqrt(ms + NORM_EPS) * fg_ref[...]
        o_ref[...] = x2


def _moe(h_bf, gates, x, gate, final_g, wg_bf, wu_bf, wd_bf, final_norm):
    seqs, length, _ = x.shape
    bs, tl = _seq_blocks(seqs, length)
    tok = pl.BlockSpec((bs, tl, D_MODEL), lambda s, i, e: (s, i, 0))
    mod = pl.BlockSpec((bs, 1, D_MODEL), lambda s, i, e: (s, 0, 0))
    return pl.pallas_call(
        functools.partial(_moe_kernel, final_norm=final_norm),
        out_shape=jax.ShapeDtypeStruct((seqs, length, D_MODEL), F32),
        grid=(seqs // bs, length // tl, N_EXPERTS),
        in_specs=[tok, pl.BlockSpec((bs, tl, N_EXPERTS), lambda s, i, e: (s, i, 0)), tok, mod,
                  pl.BlockSpec((1, 1, D_MODEL), lambda s, i, e: (0, 0, 0)),
                  pl.BlockSpec((1, D_MODEL, D_EXPERT), lambda s, i, e: (e, 0, 0)),
                  pl.BlockSpec((1, D_MODEL, D_EXPERT), lambda s, i, e: (e, 0, 0)),
                  pl.BlockSpec((1, D_EXPERT, D_MODEL), lambda s, i, e: (e, 0, 0))],
        out_specs=tok,
        scratch_shapes=[pltpu.VMEM((bs * tl, D_MODEL), F32)],
        compiler_params=_cparams("arbitrary", "arbitrary", "arbitrary"),
        name="moe_experts",
    )(h_bf, gates, x, gate, final_g.reshape(1, 1, D_MODEL), wg_bf, wu_bf, wd_bf)


def _head_selectors():
    r = lax.broadcasted_iota(jnp.int32, (D_MODEL, LANES), 0)
    c = lax.broadcasted_iota(jnp.int32, (D_MODEL, LANES), 1)
    sel = jnp.where(r // HEAD_DIM == c, 1.0, 0.0).astype(BF16)
    rt = lax.broadcasted_iota(jnp.int32, (LANES, D_MODEL), 0)
    ct = lax.broadcasted_iota(jnp.int32, (LANES, D_MODEL), 1)
    selt = jnp.where(ct // HEAD_DIM == rt, 1.0, 0.0).astype(BF16)
    return jnp.concatenate([sel, sel], axis=0), jnp.concatenate([selt, selt], axis=0)


def _head_sum(x, sel2, selt2):
    return _dot2(_dot2(x, sel2), selt2)


def _rwkv_pre_kernel(x_ref, prev_ref, shift_ref, g_ref, sc_ref, sh_ref, mu_ref,
                     wr_ref, wk_ref, wv_ref, w0_ref, w1_ref, w2_ref, a0_ref, a1_ref, a2_ref,
                     g1_ref, g2_ref, kk_ref, ka_ref,
                     r_ref, w_ref, k_ref, v_ref, na_ref, kb_ref, go_ref, so_ref):
    bs, tl, _ = x_ref.shape
    t = bs * tl
    i = pl.program_id(1)
    g, sc, sh = g_ref[...], sc_ref[...], sh_ref[...]
    h = _norm_mod(x_ref[...], g, sc, sh)
    so_ref[...] = h[:, tl - 1:tl, :]
    h_before = _norm_mod(prev_ref[:, 7:8, :], g, sc, sh)
    first = jnp.where(i == 0, shift_ref[...], h_before)
    row = lax.broadcasted_iota(jnp.int32, (bs, tl, D_MODEL), 1)
    prev = jnp.where(row == 0, first, pltpu.roll(h, 1, 1))
    h = h.reshape(t, D_MODEL)
    xx = prev.reshape(t, D_MODEL) - h
    mixed = [(h + xx * mu_ref[j:j + 1, :]).astype(BF16) for j in range(6)]
    xr, xw, xk, xv, xa, xg = mixed
    r = _dot(xr, wr_ref[...])
    k = _dot(xk, wk_ref[...])
    v = _dot(xv, wv_ref[...])
    wl = w0_ref[...] + _dot(jnp.tanh(_dot(xw, w1_ref[...])).astype(BF16), w2_ref[...])
    neg = -wl
    softplus = jnp.maximum(neg, 0.0) + jnp.log(1.0 + jnp.exp(-jnp.abs(neg)))
    w = -softplus - 0.5
    decay = jnp.exp(-jnp.exp(w))
    a = _sigmoid(a0_ref[...] + _dot(_dot(xa, a1_ref[...]).astype(BF16), a2_ref[...]))
    gate = _dot(_sigmoid(_dot(xg, g1_ref[...])).astype(BF16), g2_ref[...])
    sel2, selt2 = _head_selectors()
    kk = k * kk_ref[...]
    norm = jnp.maximum(jnp.sqrt(_head_sum(kk * kk, sel2, selt2)), 1e-12)
    kk = kk / norm
    k = k * (1.0 + (a - 1.0) * ka_ref[...])
    shape = (bs, tl, D_MODEL)
    r_ref[...] = r.reshape(shape)
    w_ref[...] = decay.reshape(shape)
    k_ref[...] = k.reshape(shape)
    v_ref[...] = v.reshape(shape)
    na_ref[...] = (-kk).reshape(shape)
    kb_ref[...] = (kk * a).reshape(shape)
    go_ref[...] = gate.reshape(shape)


def _rwkv_pre(x, shift0, g, sc, sh, p):
    seqs, length, _ = x.shape
    bs, tl = _seq_blocks(seqs, length, RWKV_PRE_TILE)
    tok = pl.BlockSpec((bs, tl, D_MODEL), lambda s, i: (s, i, 0))
    prev = pl.BlockSpec((bs, 8, D_MODEL), lambda s, i: (s, jnp.maximum(i * (tl // 8) - 1, 0), 0))
    mod = pl.BlockSpec((bs, 1, D_MODEL), lambda s, i: (s, 0, 0))
    one = pl.BlockSpec((1, 1, D_MODEL), lambda s, i: (0, 0, 0))

    def full(a):
        return pl.BlockSpec(a.shape, lambda s, i: (0,) * a.ndim)

    vec = lambda a: a.reshape(1, D_MODEL)
    weights = [p['rw_mu'], p['rw_w_r'], p['rw_w_k'], p['rw_w_v'], vec(p['rw_w0']), p['rw_w1'],
               p['rw_w2'], vec(p['rw_a0']), p['rw_a1'], p['rw_a2'], p['rw_g1'], p['rw_g2'],
               vec(p['rw_k_k']), vec(p['rw_k_a'])]
    f32o = jax.ShapeDtypeStruct((seqs, length, D_MODEL), F32)
    outs = pl.pallas_call(
        _rwkv_pre_kernel,
        out_shape=(f32o,) * 7 + (jax.ShapeDtypeStruct((seqs, 1, D_MODEL), F32),),
        grid=(seqs // bs, length // tl),
        in_specs=[tok, prev, mod, one, mod, mod] + [full(a) for a in weights],
        out_specs=(tok,) * 7 + (mod,),
        compiler_params=_cparams("arbitrary", "arbitrary"),
        name="rwkv_pre",
    )(x, x, shift0, g.reshape(1, 1, D_MODEL), sc, sh, *weights)
    return outs


def _scan_kernel(r_ref, w_ref, k_ref, v_ref, a_ref, b_ref, s0_ref, y_ref, st_ref):
    sb, tc, _ = r_ref.shape
    c = pl.program_id(1)

    @pl.when(c == 0)
    def _():
        st_ref[...] = s0_ref[...]

    rr = lax.broadcasted_iota(jnp.int32, (LANES, LANES), 0)
    cc = lax.broadcasted_iota(jnp.int32, (LANES, LANES), 1)
    bd = jnp.where(rr // HEAD_DIM == cc // HEAD_DIM, 1.0, 0.0).astype(BF16)
    bd2 = jnp.concatenate([bd, bd], axis=0)
    er = lax.broadcasted_iota(jnp.int32, (HEAD_DIM, LANES), 0)
    ec = lax.broadcasted_iota(jnp.int32, (HEAD_DIM, LANES), 1)
    eye2 = jnp.where(ec % HEAD_DIM == er, 1.0, 0.0)

    chains = [(s, p) for s in range(sb) for p in range(PAIRS)]

    def steps(t8, _):
        rows8 = pl.ds(pl.multiple_of(t8 * SUBLANES, SUBLANES), SUBLANES)
        blocks = {}
        for s, p in chains:
            lanes = pl.ds(p * LANES, LANES)
            blocks[s, p] = [ref[s, rows8, lanes] for ref in (r_ref, w_ref, k_ref, v_ref, a_ref, b_ref)]
        ys = {c: [] for c in chains}
        for j in range(SUBLANES):
            for s, p in chains:
                r_t, w_t, k_t, v_t, a_t, b_t = [blk[j:j + 1, :] for blk in blocks[s, p]]
                state = st_ref[s, p]
                sa = _dot2(state * a_t, bd2)
                vcol = _dot((v_t * eye2).astype(BF16), bd)
                state = state * w_t + sa * b_t + vcol * k_t
                st_ref[s, p] = state
                yb = _dot((state * r_t).astype(BF16), bd)
                ys[s, p].append(jnp.sum(yb * eye2, axis=0, keepdims=True))
        for s, p in chains:
            y_ref[s, rows8, pl.ds(p * LANES, LANES)] = jnp.concatenate(ys[s, p], axis=0)
        return 0

    lax.fori_loop(0, tc // SUBLANES, steps, 0)


def _scan(r, w, k, v, na, kb, s0):
    seqs, length, _ = r.shape
    sb = 2
    tc = min(length, 256)
    assert seqs % sb == 0 and length % tc == 0
    s0p = s0.reshape(seqs, PAIRS, 2, HEAD_DIM, HEAD_DIM).transpose(0, 1, 3, 2, 4)
    s0p = s0p.reshape(seqs, PAIRS, HEAD_DIM, LANES)
    tok = pl.BlockSpec((sb, tc, D_MODEL), lambda s, c: (s, c, 0))
    st = pl.BlockSpec((sb, PAIRS, HEAD_DIM, LANES), lambda s, c: (s, 0, 0, 0))
    y, stp = pl.pallas_call(
        _scan_kernel,
        out_shape=(jax.ShapeDtypeStruct((seqs, length, D_MODEL), F32),
                   jax.ShapeDtypeStruct((seqs, PAIRS, HEAD_DIM, LANES), F32)),
        grid=(seqs // sb, length // tc),
        in_specs=[tok] * 6 + [st],
        out_specs=(tok, st),
        compiler_params=_cparams("arbitrary", "arbitrary"),
        name="rwkv_scan",
    )(r, w, k, v, na, kb, s0p)
    state = stp.reshape(seqs, PAIRS, HEAD_DIM, 2, HEAD_DIM).transpose(0, 1, 3, 2, 4)
    return y, state.reshape(seqs, HEADS, HEAD_DIM, HEAD_DIM)


def _rwkv_post_kernel(y_ref, r_ref, k_ref, v_ref, g_ref, lg_ref, lb_ref, rk_ref, o_ref):
    bs, tl, _ = y_ref.shape
    t = bs * tl
    sel2, selt2 = _head_selectors()
    y = y_ref[...].reshape(t, D_MODEL)
    mu = _head_sum(y, sel2, selt2) * (1.0 / HEAD_DIM)
    d = y - mu
    var = _head_sum(d * d, sel2, selt2) * (1.0 / HEAD_DIM)
    yn = d * lax.rsqrt(var + LNX_EPS) * lg_ref[...] + lb_ref[...]
    rk = r_ref[...].reshape(t, D_MODEL) * k_ref[...].reshape(t, D_MODEL) * rk_ref[...]
    bonus = _head_sum(rk, sel2, selt2) * v_ref[...].reshape(t, D_MODEL)
    o_ref[...] = ((yn + bonus) * g_ref[...].reshape(t, D_MODEL)).astype(BF16).reshape(bs, tl, D_MODEL)


def _rwkv_post(y, r, k, v, gate, lnx_g, lnx_b, r_k):
    seqs, length, _ = y.shape
    bs, tl = _seq_blocks(seqs, length)
    tok = pl.BlockSpec((bs, tl, D_MODEL), lambda s, i: (s, i, 0))
    vec = pl.BlockSpec((1, D_MODEL), lambda s, i: (0, 0))
    return pl.pallas_call(
        _rwkv_post_kernel,
        out_shape=jax.ShapeDtypeStruct((seqs, length, D_MODEL), BF16),
        grid=(seqs // bs, length // tl),
        in_specs=[tok] * 5 + [vec] * 3,
        out_specs=tok,
        compiler_params=_cparams("arbitrary", "arbitrary"),
        name="rwkv_post",
    )(y, r, k, v, gate, lnx_g.reshape(1, D_MODEL), lnx_b.reshape(1, D_MODEL),
      r_k.reshape(1, D_MODEL))


def _trunk(x, mod, p, wb, k_cache, v_cache, shift0, s0):
    seqs, length, _ = x.shape
    part = lambda layer, j: mod[layer, :, j * D_MODEL:(j + 1) * D_MODEL].reshape(seqs, 1, D_MODEL)
    router_wt = p['router_w'].T
    sh1, sc1, g1, sh2, sc2, g2 = [part(0, j) for j in range(6)]
    q_bf, k, v, k_bf, v_bf = _qkv(x, p['norm_mix_g'][0], sc1, sh1, wb['sb_w_qkv'])
    if k_cache is None:
        attn = _attn_prompt(q_bf, k_bf, v_bf)
    else:
        past = k_cache.shape[1]
        attn = _attn_sample(q_bf, k_bf, v_bf, k_cache.reshape(seqs, past, D_MODEL),
                            v_cache.reshape(seqs, past, D_MODEL))
    x, h, gates = _oproj_router(attn, x, g1, wb['sb_w_o'], p['norm_ffn_g'][0], sc2, sh2,
                                router_wt, p['router_bias'])
    x = _moe(h, gates, x, g2, p['final_norm_g'], wb['exp_w_gate'][0], wb['exp_w_up'][0],
             wb['exp_w_down'][0], final_norm=False)
    sb_k = k.reshape(seqs, length, HEADS, HEAD_DIM)
    sb_v = v.reshape(seqs, length, HEADS, HEAD_DIM)
    sh1, sc1, g1, sh2, sc2, g2 = [part(1, j) for j in range(6)]
    r, w, kq, vv, na, kb, gate, shift = _rwkv_pre(x, shift0, p['norm_mix_g'][1], sc1, sh1, wb)
    y, state = _scan(r, w, kq, vv, na, kb, s0)
    m = _rwkv_post(y, r, kq, vv, gate, p['rw_lnx_g'], p['rw_lnx_b'], p['rw_r_k'])
    x, h, gates = _oproj_router(m, x, g1, wb['rw_w_o'], p['norm_ffn_g'][1], sc2, sh2,
                                router_wt, p['router_bias'])
    y_out = _moe(h, gates, x, g2, p['final_norm_g'], wb['exp_w_gate'][1], wb['exp_w_up'][1],
                 wb['exp_w_down'][1], final_norm=True)
    return y_out, sb_k, sb_v, state, shift


def kernel(x_prompt, x_sample, cache_sb_k, cache_sb_v, state_rw_wkv, state_rw_shift,
           c_prompt, c_sample, ada_w, ada_b, norm_mix_g, norm_ffn_g, final_norm_g,
           sb_w_qkv, sb_w_o, rw_mu, rw_w_r, rw_w_k, rw_w_v, rw_w0, rw_w1, rw_w2,
           rw_a0, rw_a1, rw_a2, rw_g1, rw_g2, rw_k_k, rw_k_a, rw_r_k, rw_lnx_g, rw_lnx_b,
           rw_w_o, router_w, router_bias, exp_w_gate, exp_w_up, exp_w_down):
    p = {
        'norm_mix_g': norm_mix_g, 'norm_ffn_g': norm_ffn_g, 'final_norm_g': final_norm_g,
        'rw_r_k': rw_r_k, 'rw_lnx_g': rw_lnx_g, 'rw_lnx_b': rw_lnx_b,
        'router_w': router_w, 'router_bias': router_bias,
    }
    bf = lambda a: a.astype(BF16)
    wb = {
        'sb_w_qkv': bf(sb_w_qkv), 'sb_w_o': bf(sb_w_o), 'rw_w_o': bf(rw_w_o),
        'rw_mu': rw_mu, 'rw_w_r': bf(rw_w_r), 'rw_w_k': bf(rw_w_k), 'rw_w_v': bf(rw_w_v),
        'rw_w0': rw_w0, 'rw_w1': bf(rw_w1), 'rw_w2': bf(rw_w2), 'rw_a0': rw_a0,
        'rw_a1': bf(rw_a1), 'rw_a2': bf(rw_a2), 'rw_g1': bf(rw_g1), 'rw_g2': bf(rw_g2),
        'rw_k_k': rw_k_k, 'rw_k_a': rw_k_a,
        'exp_w_gate': bf(exp_w_gate), 'exp_w_up': bf(exp_w_up), 'exp_w_down': bf(exp_w_down),
    }
    n_prompt = x_prompt.shape[0]
    mod = _ada(jnp.concatenate([c_prompt, c_sample], axis=0), ada_w, ada_b)
    shift0 = jnp.zeros((n_prompt, 1, D_MODEL), x_prompt.dtype)
    s0 = jnp.zeros((n_prompt, HEADS, HEAD_DIM, HEAD_DIM), F32)
    y_p, k_p, v_p, st_p, sh_p = _trunk(x_prompt, mod[:, :n_prompt], p, wb, None, None, shift0, s0)
    y_s, k_s, v_s, st_s, sh_s = _trunk(x_sample, mod[:, n_prompt:], p, wb, cache_sb_k, cache_sb_v,
                                       state_rw_shift, state_rw_wkv)
    return (y_p, y_s, k_p, v_p, k_s, v_s, st_p, sh_p, st_s, sh_s)
```

```python
import functools

import jax
import jax.numpy as jnp
from jax import lax
from jax.experimental import pallas as pl
from jax.experimental.pallas import tpu as pltpu

F32 = jnp.float32
BF16 = jnp.bfloat16

D_MODEL = 1024
HEADS = 16
HEAD_DIM = 64
LANES = 128
SUBLANES = 8
PAIRS = D_MODEL // LANES
N_EXPERTS = 16
N_GROUPS = 4
EXPERTS_PER_GROUP = 4
D_EXPERT = 512
NORM_EPS = 1e-6
LNX_EPS = 64e-5
ZERO_WEIGHT_CARRY = 104.0
TOKEN_TILE = 512
RWKV_PRE_TILE = 256
QKV_TILE = 256
VMEM_LIMIT = 56 * 1024 * 1024

_NT = (((1,), (1,)), ((), ()))


def _cparams(*sem):
    return pltpu.CompilerParams(dimension_semantics=sem, vmem_limit_bytes=VMEM_LIMIT)


def _sigmoid(x):
    return 1.0 / (1.0 + jnp.exp(-x))


def _split2(x):
    hi = x.astype(BF16)
    lo = (x - hi.astype(F32)).astype(BF16)
    return hi, lo


def _dot(a, b):
    return jnp.dot(a, b, preferred_element_type=F32)


def _dot2(x, w2):
    hi, lo = _split2(x)
    return _dot(jnp.concatenate([hi, lo], axis=1), w2)


def _norm_mod(x, g, sc, sh):
    ms = jnp.mean(x * x, axis=-1, keepdims=True)
    y = x * lax.rsqrt(ms + NORM_EPS) * g
    return y * (1.0 + sc) + sh


def _seq_blocks(seqs, length, tile=TOKEN_TILE):
    if length >= tile:
        assert length % tile == 0
        return 1, tile
    assert tile % length == 0 and seqs % (tile // length) == 0 and length % 16 == 0
    return tile // length, length


def _ada_kernel(c_ref, w_ref, b_ref, o_ref):
    c = c_ref[...]
    s = c * _sigmoid(c)
    o_ref[0] = jnp.dot(s, w_ref[0], precision=lax.Precision.HIGHEST,
                       preferred_element_type=F32) + b_ref[0]


def _ada(c, ada_w, ada_b):
    rows = c.shape[0]
    depth = ada_w.shape[0]
    return pl.pallas_call(
        _ada_kernel,
        out_shape=jax.ShapeDtypeStruct((depth, rows, 6 * D_MODEL), F32),
        grid=(depth, 6),
        in_specs=[pl.BlockSpec((rows, D_MODEL), lambda l, j: (0, 0)),
                  pl.BlockSpec((1, D_MODEL, D_MODEL), lambda l, j: (l, 0, j)),
                  pl.BlockSpec((1, 1, D_MODEL), lambda l, j: (l, 0, j))],
        out_specs=pl.BlockSpec((1, rows, D_MODEL), lambda l, j: (l, 0, j)),
        compiler_params=_cparams("arbitrary", "arbitrary"),
        name="ada_mod",
    )(c, ada_w, ada_b.reshape(depth, 1, 6 * D_MODEL))


def _dot3(x, w_hi, w_lo):
    hi, lo = _split2(x)
    rows = x.shape[0]
    both = _dot(jnp.concatenate([hi, lo], axis=0), w_hi)
    return both[:rows] + both[rows:] + _dot(hi, w_lo)


def _split_weight(w):
    hi = w.astype(BF16)
    return hi, (w - hi.astype(F32)).astype(BF16)


def _qkv_kernel(x_ref, g_ref, sc_ref, sh_ref, wh_ref, wl_ref, k_ref, v_ref, *piece_refs):
    bs, tl, _ = x_ref.shape
    h = _norm_mod(x_ref[...], g_ref[...], sc_ref[...], sh_ref[...]).reshape(bs * tl, D_MODEL)
    qkv = _dot3(h, wh_ref[...], wl_ref[...])
    q = qkv[:, :D_MODEL] * (HEAD_DIM ** -0.5)
    k = qkv[:, D_MODEL:2 * D_MODEL]
    v = qkv[:, 2 * D_MODEL:]
    k_ref[...] = k.reshape(bs, tl, D_MODEL)
    v_ref[...] = v.reshape(bs, tl, D_MODEL)
    pieces = _split2(q) + _split2(k) + _split2(v)
    for ref, piece in zip(piece_refs, pieces):
        ref[...] = piece.reshape(bs, tl, D_MODEL)


def _qkv(x, g, sc, sh, w_hi, w_lo):
    seqs, length, _ = x.shape
    bs, tl = _seq_blocks(seqs, length, QKV_TILE)
    tok = pl.BlockSpec((bs, tl, D_MODEL), lambda s, i: (s, i, 0))
    mod = pl.BlockSpec((bs, 1, D_MODEL), lambda s, i: (s, 0, 0))
    wspec = pl.BlockSpec((D_MODEL, 3 * D_MODEL), lambda s, i: (0, 0))
    f32o = jax.ShapeDtypeStruct((seqs, length, D_MODEL), F32)
    bfo = jax.ShapeDtypeStruct((seqs, length, D_MODEL), BF16)
    return pl.pallas_call(
        _qkv_kernel,
        out_shape=(f32o, f32o) + (bfo,) * 6,
        grid=(seqs // bs, length // tl),
        in_specs=[tok, pl.BlockSpec((1, 1, D_MODEL), lambda s, i: (0, 0, 0)), mod, mod, wspec, wspec],
        out_specs=(tok,) * 8,
        compiler_params=_cparams("arbitrary", "arbitrary"),
        name="sb_qkv",
    )(x, g.reshape(1, 1, D_MODEL), sc, sh, w_hi, w_lo)


def _pair_masks(rows):
    lane = lax.broadcasted_iota(jnp.int32, (rows, LANES), 1)
    return lane < HEAD_DIM


def _suffix_matrix():
    r = lax.broadcasted_iota(jnp.int32, (LANES, LANES), 0)
    c = lax.broadcasted_iota(jnp.int32, (LANES, LANES), 1)
    u = jnp.where(r > c, 1.0, 0.0).astype(BF16)
    return jnp.concatenate([u, u], axis=0)


def _sb_tile(q2, k, v, carry, acc, u2, visible):
    tq = q2[0].shape[0] // 2
    z = (lax.dot_general(jnp.concatenate(q2, axis=1), jnp.concatenate([k[0], k[0]], axis=1), _NT,
                         preferred_element_type=F32)
         + lax.dot_general(q2[0], k[1], _NT, preferred_element_type=F32))
    sp = jnp.maximum(z, 0.0) + jnp.log(1.0 + jnp.exp(-jnp.abs(z)))
    if visible is not None:
        sp = jnp.where(visible, sp, 0.0)
    later = _dot2(sp, u2)
    logw = z - sp - later - carry
    w = jnp.exp(logw)
    if visible is not None:
        w = jnp.where(visible, w, 0.0)
    w_hi, w_lo = _split2(w)
    v_hi, v_lo = _stack_heads(v[0], axis=0), _stack_heads(v[1], axis=0)
    wide = lambda x: jnp.concatenate([x[:tq], x[tq:]], axis=1)
    acc = (acc + _dot(jnp.concatenate([wide(w_hi), wide(w_lo)], axis=1),
                      jnp.concatenate([v_hi, v_hi], axis=0))
           + _dot(wide(w_hi), v_lo))
    carry = carry + jnp.sum(sp, axis=-1, keepdims=True)
    return carry, acc


def _stack_heads(x, axis=0):
    mask_a = _pair_masks(x.shape[0])
    return jnp.concatenate([jnp.where(mask_a, x, 0), jnp.where(mask_a, 0, x)], axis=axis)


def _diag_visible(tq):
    row = lax.broadcasted_iota(jnp.int32, (2 * tq, LANES), 0)
    col = lax.broadcasted_iota(jnp.int32, (2 * tq, LANES), 1)
    qpos = jnp.where(row >= tq, row - tq, row)
    return col < qpos


def _pair_lanes(p):
    return slice(p * LANES, (p + 1) * LANES)


def _sweep_earlier_tiles(n_tiles, sweep, state):
    def cond(c):
        return jnp.logical_and(c[0] < n_tiles, jnp.logical_not(c[1]))

    def body(c):
        st = sweep(n_tiles - 1 - c[0], c[2])
        low = st[0][0]
        for p in range(1, len(st)):
            low = jnp.minimum(low, st[p][0])
        return c[0] + 1, jnp.min(low) > ZERO_WEIGHT_CARRY, st

    return lax.while_loop(cond, body, (jnp.int32(0), False, state))[2]


def _attn_prompt_kernel(qh_ref, ql_ref, kh_ref, kl_ref, vh_ref, vl_ref, o_ref):
    tq = qh_ref.shape[1]
    qi = pl.program_id(1)
    u2 = _suffix_matrix()
    q2 = [(_stack_heads(qh_ref[0, :, _pair_lanes(p)]), _stack_heads(ql_ref[0, :, _pair_lanes(p)]))
          for p in range(PAIRS)]

    def sweep(start, state, visible):
        rows = pl.ds(pl.multiple_of(start, tq), tq)
        tile = lambda ref, p: ref[0, rows, _pair_lanes(p)]
        return tuple(
            _sb_tile(q2[p], (tile(kh_ref, p), tile(kl_ref, p)), (tile(vh_ref, p), tile(vl_ref, p)),
                     state[p][0], state[p][1], u2, visible)
            for p in range(PAIRS))

    zero = (jnp.zeros((2 * tq, LANES), F32), jnp.zeros((tq, LANES), F32))
    state = sweep(qi * tq, (zero,) * PAIRS, _diag_visible(tq))
    state = _sweep_earlier_tiles(qi, lambda j, st: sweep(j * tq, st, None), state)
    for p in range(PAIRS):
        o_ref[0, :, _pair_lanes(p)] = state[p][1]


def _attn_prompt(q, k, v):
    batch, length, _ = q[0].shape
    tq = LANES
    qspec = pl.BlockSpec((1, tq, D_MODEL), lambda b, i: (b, i, 0))
    kspec = pl.BlockSpec((1, length, D_MODEL), lambda b, i: (b, 0, 0), pipeline_mode=pl.Buffered(1))
    return pl.pallas_call(
        _attn_prompt_kernel,
        out_shape=jax.ShapeDtypeStruct((batch, length, D_MODEL), F32),
        grid=(batch, length // tq),
        in_specs=[qspec, qspec, kspec, kspec, kspec, kspec],
        out_specs=qspec,
        compiler_params=_cparams("arbitrary", "arbitrary"),
        name="sb_attn_prompt",
    )(*q, *k, *v)


def _attn_sample_kernel(qh_ref, ql_ref, kh_ref, kl_ref, vh_ref, vl_ref, kc_ref, vc_ref, o_ref):
    tq = qh_ref.shape[1]
    past = kc_ref.shape[1]
    u2 = _suffix_matrix()
    pad = jnp.zeros((LANES - tq, LANES), BF16)
    padded = lambda ref, p: jnp.concatenate([ref[0, :, _pair_lanes(p)], pad], axis=0)
    q2 = [(_stack_heads(qh_ref[0, :, _pair_lanes(p)]), _stack_heads(ql_ref[0, :, _pair_lanes(p)]))
          for p in range(PAIRS)]
    state = tuple(
        _sb_tile(q2[p], (padded(kh_ref, p), padded(kl_ref, p)), (padded(vh_ref, p), padded(vl_ref, p)),
                 jnp.zeros((2 * tq, LANES), F32), jnp.zeros((tq, LANES), F32), u2, _diag_visible(tq))
        for p in range(PAIRS))

    def sweep(j, st):
        rows = pl.ds(pl.multiple_of(j * LANES, LANES), LANES)
        return tuple(
            _sb_tile(q2[p], _split2(kc_ref[0, rows, _pair_lanes(p)]),
                     _split2(vc_ref[0, rows, _pair_lanes(p)]), st[p][0], st[p][1], u2, None)
            for p in range(PAIRS))

    state = _sweep_earlier_tiles(past // LANES, sweep, state)
    for p in range(PAIRS):
        o_ref[0, :, _pair_lanes(p)] = state[p][1]


def _attn_sample(q, k, v, k_cache, v_cache):
    batch, length, _ = q[0].shape
    past = k_cache.shape[1]
    assert length <= LANES and past % LANES == 0
    qspec = pl.BlockSpec((1, length, D_MODEL), lambda b: (b, 0, 0))
    cspec = pl.BlockSpec((1, past, D_MODEL), lambda b: (b, 0, 0))
    return pl.pallas_call(
        _attn_sample_kernel,
        out_shape=jax.ShapeDtypeStruct((batch, length, D_MODEL), F32),
        grid=(batch,),
        in_specs=[qspec] * 6 + [cspec, cspec],
        out_specs=qspec,
        compiler_params=_cparams("arbitrary"),
        name="sb_attn_sample",
    )(*q, *k, *v, k_cache, v_cache)


def _row_max(rows):
    m = rows[0]
    for r in rows[1:]:
        m = jnp.maximum(m, r)
    return m


def _first_hit(rows, m):
    hits = []
    taken = None
    for r in rows:
        eq = r == m
        hit = eq if taken is None else jnp.logical_and(eq, jnp.logical_not(taken))
        taken = eq if taken is None else jnp.logical_or(taken, eq)
        hits.append(hit)
    return hits


def _route(s_rows, sel_rows):
    neg = jnp.float32(-jnp.inf)
    top1, top2, score = [], [], []
    for g in range(N_GROUPS):
        rows = sel_rows[g * EXPERTS_PER_GROUP:(g + 1) * EXPERTS_PER_GROUP]
        m1 = _row_max(rows)
        h1 = _first_hit(rows, m1)
        rest = [jnp.where(h, neg, r) for h, r in zip(h1, rows)]
        m2 = _row_max(rest)
        h2 = _first_hit(rest, m2)
        top1.append(h1)
        top2.append(h2)
        score.append(m1 + m2)
    best = _row_max(score)
    ghit = _first_hit(score, best)
    picked = []
    for g in range(N_GROUPS):
        for e in range(EXPERTS_PER_GROUP):
            picked.append(jnp.logical_and(ghit[g], jnp.logical_or(top1[g][e], top2[g][e])))
    total = None
    for p, s in zip(picked, s_rows):
        term = jnp.where(p, s, 0.0)
        total = term if total is None else total + term
    return [jnp.where(p, s / total, 0.0) for p, s in zip(picked, s_rows)]


def _oproj_kernel(m_ref, x_ref, gate_ref, wh_ref, wl_ref, g_ref, sc_ref, sh_ref, rw_ref, rb_ref,
                  x1_ref, h_ref, gates_ref):
    bs, tl, _ = x_ref.shape
    m = m_ref[...].reshape(bs * tl, D_MODEL)
    mix = _dot3(m, wh_ref[...], wl_ref[...]).reshape(bs, tl, D_MODEL)
    x1 = x_ref[...] + gate_ref[...] * mix
    x1_ref[...] = x1
    h = _norm_mod(x1, g_ref[...], sc_ref[...], sh_ref[...]).reshape(bs * tl, D_MODEL)
    h_ref[...] = h.astype(BF16).reshape(bs, tl, D_MODEL)
    logits = lax.dot_general(rw_ref[...], h, _NT, precision=lax.Precision.HIGHEST,
                             preferred_element_type=F32)
    s = _sigmoid(logits)
    sel = s + rb_ref[...]
    s_rows = [s[e:e + 1, :] for e in range(N_EXPERTS)]
    sel_rows = [sel[e:e + 1, :] for e in range(N_EXPERTS)]
    gates_ref[0] = jnp.concatenate(_route(s_rows, sel_rows), axis=0)


def _oproj_router(m, x, gate, w_pieces, g, sc, sh, router_wt, router_b):
    seqs, length, _ = x.shape
    bs, tl = _seq_blocks(seqs, length)
    n_tiles = (seqs // bs) * (length // tl)
    per_seq = length // tl
    tok = pl.BlockSpec((bs, tl, D_MODEL), lambda s, i: (s, i, 0))
    mod = pl.BlockSpec((bs, 1, D_MODEL), lambda s, i: (s, 0, 0))
    one = pl.BlockSpec((1, 1, D_MODEL), lambda s, i: (0, 0, 0))
    wspec = pl.BlockSpec((D_MODEL, D_MODEL), lambda s, i: (0, 0))
    x1, h, gates = pl.pallas_call(
        _oproj_kernel,
        out_shape=(jax.ShapeDtypeStruct((seqs, length, D_MODEL), F32),
                   jax.ShapeDtypeStruct((seqs, length, D_MODEL), BF16),
                   jax.ShapeDtypeStruct((n_tiles, N_EXPERTS, bs * tl), F32)),
        grid=(seqs // bs, per_seq),
        in_specs=[tok, tok, mod, wspec, wspec, one, mod, mod,
                  pl.BlockSpec((N_EXPERTS, D_MODEL), lambda s, i: (0, 0)),
                  pl.BlockSpec((N_EXPERTS, 1), lambda s, i: (0, 0))],
        out_specs=(tok, tok,
                   pl.BlockSpec((1, N_EXPERTS, bs * tl), lambda s, i: (s * per_seq + i, 0, 0))),
        compiler_params=_cparams("arbitrary", "arbitrary"),
        name="oproj_router",
    )(m, x, gate, *w_pieces, g.reshape(1, 1, D_MODEL), sc, sh, router_wt,
      router_b.reshape(N_EXPERTS, 1))
    gates = jnp.swapaxes(gates, 1, 2).reshape(seqs, length, N_EXPERTS)
    return x1, h, gates


def _moe_kernel(h_ref, gates_ref, x_ref, gate_ref, fg_ref, wg_ref, wu_ref, wd_ref,
                o_ref, acc_ref, *, final_norm):
    bs, tl, _ = x_ref.shape
    e = pl.program_id(2)
    h = h_ref[...].reshape(bs * tl, D_MODEL)
    a = _dot(h, wg_ref[0])
    u = _dot(h, wu_ref[0])
    he = (a * _sigmoid(a) * u).astype(BF16)
    out = _dot(he, wd_ref[0])
    gates = gates_ref[...].reshape(bs * tl, N_EXPERTS)
    lane = lax.broadcasted_iota(jnp.int32, gates.shape, 1)
    ge = jnp.sum(jnp.where(lane == e, gates, 0.0), axis=-1, keepdims=True)
    contrib = ge * out

    @pl.when(e == 0)
    def _():
        acc_ref[...] = contrib

    @pl.when(e > 0)
    def _():
        acc_ref[...] += contrib

    @pl.when(e == N_EXPERTS - 1)
    def _():
        x2 = x_ref[...] + gate_ref[...] * acc_ref[...].reshape(bs, tl, D_MODEL)
        if final_norm:
            ms = jnp.mean(x2 * x2, axis=-1, keepdims=True)
            x2 = x2 * lax.rsqrt(ms + NORM_EPS) * fg_ref[...]
        o_ref[...] = x2


def _moe(h_bf, gates, x, gate, final_g, wg_bf, wu_bf, wd_bf, final_norm):
    seqs, length, _ = x.shape
    bs, tl = _seq_blocks(seqs, length)
    tok = pl.BlockSpec((bs, tl, D_MODEL), lambda s, i, e: (s, i, 0))
    mod = pl.BlockSpec((bs, 1, D_MODEL), lambda s, i, e: (s, 0, 0))
    return pl.pallas_call(
        functools.partial(_moe_kernel, final_norm=final_norm),
        out_shape=jax.ShapeDtypeStruct((seqs, length, D_MODEL), F32),
        grid=(seqs // bs, length // tl, N_EXPERTS),
        in_specs=[tok, pl.BlockSpec((bs, tl, N_EXPERTS), lambda s, i, e: (s, i, 0)), tok, mod,
                  pl.BlockSpec((1, 1, D_MODEL), lambda s, i, e: (0, 0, 0)),
                  pl.BlockSpec((1, D_MODEL, D_EXPERT), lambda s, i, e: (e, 0, 0)),
                  pl.BlockSpec((1, D_MODEL, D_EXPERT), lambda s, i, e: (e, 0, 0)),
                  pl.BlockSpec((1, D_EXPERT, D_MODEL), lambda s, i, e: (e, 0, 0))],
        out_specs=tok,
        scratch_shapes=[pltpu.VMEM((bs * tl, D_MODEL), F32)],
        compiler_params=_cparams("arbitrary", "arbitrary", "arbitrary"),
        name="moe_experts",
    )(h_bf, gates, x, gate, final_g.reshape(1, 1, D_MODEL), wg_bf, wu_bf, wd_bf)


def _head_selectors():
    r = lax.broadcasted_iota(jnp.int32, (D_MODEL, LANES), 0)
    c = lax.broadcasted_iota(jnp.int32, (D_MODEL, LANES), 1)
    sel = jnp.where(r // HEAD_DIM == c, 1.0, 0.0).astype(BF16)
    rt = lax.broadcasted_iota(jnp.int32, (LANES, D_MODEL), 0)
    ct = lax.broadcasted_iota(jnp.int32, (LANES, D_MODEL), 1)
    selt = jnp.where(ct // HEAD_DIM == rt, 1.0, 0.0).astype(BF16)
    return jnp.concatenate([sel, sel], axis=0), jnp.concatenate([selt, selt], axis=0)


def _head_sum(x, sel2, selt2):
    return _dot2(_dot2(x, sel2), selt2)


def _rwkv_pre_kernel(x_ref, prev_ref, shift_ref, g_ref, sc_ref, sh_ref, mu_ref,
                     wr_ref, wk_ref, wv_ref, w0_ref, w1_ref, w2_ref, a0_ref, a1_ref, a2_ref,
                     g1_ref, g2_ref, kk_ref, ka_ref,
                     r_ref, w_ref, k_ref, v_ref, na_ref, kb_ref, go_ref, so_ref):
    bs, tl, _ = x_ref.shape
    t = bs * tl
    i = pl.program_id(1)
    g, sc, sh = g_ref[...], sc_ref[...], sh_ref[...]
    h = _norm_mod(x_ref[...], g, sc, sh)
    so_ref[...] = h[:, tl - 1:tl, :]
    h_before = _norm_mod(prev_ref[:, 7:8, :], g, sc, sh)
    first = jnp.where(i == 0, shift_ref[...], h_before)
    row = lax.broadcasted_iota(jnp.int32, (bs, tl, D_MODEL), 1)
    prev = jnp.where(row == 0, first, pltpu.roll(h, 1, 1))
    h = h.reshape(t, D_MODEL)
    xx = prev.reshape(t, D_MODEL) - h
    mixed = [(h + xx * mu_ref[j:j + 1, :]).astype(BF16) for j in range(6)]
    xr, xw, xk, xv, xa, xg = mixed
    r = _dot(xr, wr_ref[...])
    k = _dot(xk, wk_ref[...])
    v = _dot(xv, wv_ref[...])
    wl = w0_ref[...] + _dot(jnp.tanh(_dot(xw, w1_ref[...])).astype(BF16), w2_ref[...])
    neg = -wl
    softplus = jnp.maximum(neg, 0.0) + jnp.log(1.0 + jnp.exp(-jnp.abs(neg)))
    w = -softplus - 0.5
    decay = jnp.exp(-jnp.exp(w))
    a = _sigmoid(a0_ref[...] + _dot(_dot(xa, a1_ref[...]).astype(BF16), a2_ref[...]))
    gate = _dot(_sigmoid(_dot(xg, g1_ref[...])).astype(BF16), g2_ref[...])
    sel2, selt2 = _head_selectors()
    kk = k * kk_ref[...]
    norm = jnp.maximum(jnp.sqrt(_head_sum(kk * kk, sel2, selt2)), 1e-12)
    kk = kk / norm
    k = k * (1.0 + (a - 1.0) * ka_ref[...])
    shape = (bs, tl, D_MODEL)
    r_ref[...] = r.reshape(shape)
    w_ref[...] = decay.reshape(shape)
    k_ref[...] = k.reshape(shape)
    v_ref[...] = v.reshape(shape)
    na_ref[...] = (-kk).reshape(shape)
    kb_ref[...] = (kk * a).reshape(shape)
    go_ref[...] = gate.reshape(shape)


def _rwkv_pre(x, shift0, g, sc, sh, p):
    seqs, length, _ = x.shape
    bs, tl = _seq_blocks(seqs, length, RWKV_PRE_TILE)
    tok = pl.BlockSpec((bs, tl, D_MODEL), lambda s, i: (s, i, 0))
    prev = pl.BlockSpec((bs, 8, D_MODEL), lambda s, i: (s, jnp.maximum(i * (tl // 8) - 1, 0), 0))
    mod = pl.BlockSpec((bs, 1, D_MODEL), lambda s, i: (s, 0, 0))
    one = pl.BlockSpec((1, 1, D_MODEL), lambda s, i: (0, 0, 0))

    def full(a):
        return pl.BlockSpec(a.shape, lambda s, i: (0,) * a.ndim)

    vec = lambda a: a.reshape(1, D_MODEL)
    weights = [p['rw_mu'], p['rw_w_r'], p['rw_w_k'], p['rw_w_v'], vec(p['rw_w0']), p['rw_w1'],
               p['rw_w2'], vec(p['rw_a0']), p['rw_a1'], p['rw_a2'], p['rw_g1'], p['rw_g2'],
               vec(p['rw_k_k']), vec(p['rw_k_a'])]
    f32o = jax.ShapeDtypeStruct((seqs, length, D_MODEL), F32)
    outs = pl.pallas_call(
        _rwkv_pre_kernel,
        out_shape=(f32o,) * 7 + (jax.ShapeDtypeStruct((seqs, 1, D_MODEL), F32),),
        grid=(seqs // bs, length // tl),
        in_specs=[tok, prev, mod, one, mod, mod] + [full(a) for a in weights],
        out_specs=(tok,) * 7 + (mod,),
        compiler_params=_cparams("arbitrary", "arbitrary"),
        name="rwkv_pre",
    )(x, x, shift0, g.reshape(1, 1, D_MODEL), sc, sh, *weights)
    return outs


def _scan_kernel(r_ref, w_ref, k_ref, v_ref, a_ref, b_ref, s0_ref, y_ref, st_ref):
    sb, tc = r_ref.shape[:2]
    rows = PAIRS * HEAD_DIM
    c = pl.program_id(1)

    @pl.when(c == 0)
    def _():
        st_ref[...] = s0_ref[...]

    rr = lax.broadcasted_iota(jnp.int32, (LANES, LANES), 0)
    cc = lax.broadcasted_iota(jnp.int32, (LANES, LANES), 1)
    bd = jnp.where(rr // HEAD_DIM == cc // HEAD_DIM, 1.0, 0.0).astype(BF16)
    bd2 = jnp.concatenate([bd, bd], axis=0)
    er = lax.broadcasted_iota(jnp.int32, (rows, LANES), 0)
    ec = lax.broadcasted_iota(jnp.int32, (rows, LANES), 1)
    eye = jnp.where(ec % HEAD_DIM == er % HEAD_DIM, 1.0, 0.0)

    def spread(tile):
        return jnp.concatenate(
            [jnp.broadcast_to(tile[p:p + 1, :], (HEAD_DIM, LANES)) for p in range(PAIRS)], axis=0)

    def step(t, _):
        for s in range(sb):
            state = st_ref[s]
            sa = _dot2(state * spread(a_ref[s, t]), bd2)
            vcol = _dot((spread(v_ref[s, t]) * eye).astype(BF16), bd)
            state = (state * spread(w_ref[s, t]) + sa * spread(b_ref[s, t])
                     + vcol * spread(k_ref[s, t]))
            st_ref[s] = state
            yb = _dot((state * spread(r_ref[s, t])).astype(BF16), bd) * eye
            y_ref[s, t] = jnp.sum(yb.reshape(PAIRS, HEAD_DIM, LANES), axis=1)
        return 0

    lax.fori_loop(0, tc, step, 0, unroll=2)


def _scan(r, w, k, v, na, kb, s0):
    seqs, length, _ = r.shape
    sb = 4
    tc = min(length, 128)
    assert seqs % sb == 0 and length % tc == 0
    s0p = s0.reshape(seqs, PAIRS, 2, HEAD_DIM, HEAD_DIM).transpose(0, 1, 3, 2, 4)
    s0p = s0p.reshape(seqs, PAIRS * HEAD_DIM, LANES)
    tiled = lambda a: a.reshape(seqs, length, PAIRS, LANES)
    tok = pl.BlockSpec((sb, tc, PAIRS, LANES), lambda s, c: (s, c, 0, 0))
    st = pl.BlockSpec((sb, PAIRS * HEAD_DIM, LANES), lambda s, c: (s, 0, 0))
    y, stp = pl.pallas_call(
        _scan_kernel,
        out_shape=(jax.ShapeDtypeStruct((seqs, length, PAIRS, LANES), F32),
                   jax.ShapeDtypeStruct((seqs, PAIRS * HEAD_DIM, LANES), F32)),
        grid=(seqs // sb, length // tc),
        in_specs=[tok] * 6 + [st],
        out_specs=(tok, st),
        compiler_params=_cparams("arbitrary", "arbitrary"),
        name="rwkv_scan",
    )(tiled(r), tiled(w), tiled(k), tiled(v), tiled(na), tiled(kb), s0p)
    state = stp.reshape(seqs, PAIRS, HEAD_DIM, 2, HEAD_DIM).transpose(0, 1, 3, 2, 4)
    return y.reshape(seqs, length, D_MODEL), state.reshape(seqs, HEADS, HEAD_DIM, HEAD_DIM)


def _rwkv_post_kernel(y_ref, r_ref, k_ref, v_ref, g_ref, lg_ref, lb_ref, rk_ref, o_ref):
    bs, tl, _ = y_ref.shape
    t = bs * tl
    sel2, selt2 = _head_selectors()
    y = y_ref[...].reshape(t, D_MODEL)
    mu = _head_sum(y, sel2, selt2) * (1.0 / HEAD_DIM)
    d = y - mu
    var = _head_sum(d * d, sel2, selt2) * (1.0 / HEAD_DIM)
    yn = d * lax.rsqrt(var + LNX_EPS) * lg_ref[...] + lb_ref[...]
    rk = r_ref[...].reshape(t, D_MODEL) * k_ref[...].reshape(t, D_MODEL) * rk_ref[...]
    bonus = _head_sum(rk, sel2, selt2) * v_ref[...].reshape(t, D_MODEL)
    o_ref[...] = ((yn + bonus) * g_ref[...].reshape(t, D_MODEL)).reshape(bs, tl, D_MODEL)


def _rwkv_post(y, r, k, v, gate, lnx_g, lnx_b, r_k):
    seqs, length, _ = y.shape
    bs, tl = _seq_blocks(seqs, length)
    tok = pl.BlockSpec((bs, tl, D_MODEL), lambda s, i: (s, i, 0))
    vec = pl.BlockSpec((1, D_MODEL), lambda s, i: (0, 0))
    return pl.pallas_call(
        _rwkv_post_kernel,
        out_shape=jax.ShapeDtypeStruct((seqs, length, D_MODEL), F32),
        grid=(seqs // bs, length // tl),
        in_specs=[tok] * 5 + [vec] * 3,
        out_specs=tok,
        compiler_params=_cparams("arbitrary", "arbitrary"),
        name="rwkv_post",
    )(y, r, k, v, gate, lnx_g.reshape(1, D_MODEL), lnx_b.reshape(1, D_MODEL),
      r_k.reshape(1, D_MODEL))


def _trunk(x, mod, p, wb, k_cache, v_cache, shift0, s0):
    seqs, length, _ = x.shape
    part = lambda layer, j: mod[layer, :, j * D_MODEL:(j + 1) * D_MODEL].reshape(seqs, 1, D_MODEL)
    router_wt = p['router_w'].T
    sh1, sc1, g1, sh2, sc2, g2 = [part(0, j) for j in range(6)]
    k, v, qh, ql, kh, kl, vh, vl = _qkv(x, p['norm_mix_g'][0], sc1, sh1, *wb['sb_w_qkv'])
    if k_cache is None:
        attn = _attn_prompt((qh, ql), (kh, kl), (vh, vl))
    else:
        past = k_cache.shape[1]
        attn = _attn_sample((qh, ql), (kh, kl), (vh, vl), k_cache.reshape(seqs, past, D_MODEL),
                            v_cache.reshape(seqs, past, D_MODEL))
    x, h, gates = _oproj_router(attn, x, g1, wb['sb_w_o'], p['norm_ffn_g'][0], sc2, sh2,
                                router_wt, p['router_bias'])
    x = _moe(h, gates, x, g2, p['final_norm_g'], wb['exp_w_gate'][0], wb['exp_w_up'][0],
             wb['exp_w_down'][0], final_norm=False)
    sb_k = k.reshape(seqs, length, HEADS, HEAD_DIM)
    sb_v = v.reshape(seqs, length, HEADS, HEAD_DIM)
    sh1, sc1, g1, sh2, sc2, g2 = [part(1, j) for j in range(6)]
    r, w, kq, vv, na, kb, gate, shift = _rwkv_pre(x, shift0, p['norm_mix_g'][1], sc1, sh1, wb)
    y, state = _scan(r, w, kq, vv, na, kb, s0)
    m = _rwkv_post(y, r, kq, vv, gate, p['rw_lnx_g'], p['rw_lnx_b'], p['rw_r_k'])
    x, h, gates = _oproj_router(m, x, g1, wb['rw_w_o'], p['norm_ffn_g'][1], sc2, sh2,
                                router_wt, p['router_bias'])
    y_out = _moe(h, gates, x, g2, p['final_norm_g'], wb['exp_w_gate'][1], wb['exp_w_up'][1],
                 wb['exp_w_down'][1], final_norm=True)
    return y_out, sb_k, sb_v, state, shift


def kernel(x_prompt, x_sample, cache_sb_k, cache_sb_v, state_rw_wkv, state_rw_shift,
           c_prompt, c_sample, ada_w, ada_b, norm_mix_g, norm_ffn_g, final_norm_g,
           sb_w_qkv, sb_w_o, rw_mu, rw_w_r, rw_w_k, rw_w_v, rw_w0, rw_w1, rw_w2,
           rw_a0, rw_a1, rw_a2, rw_g1, rw_g2, rw_k_k, rw_k_a, rw_r_k, rw_lnx_g, rw_lnx_b,
           rw_w_o, router_w, router_bias, exp_w_gate, exp_w_up, exp_w_down):
    p = {
        'norm_mix_g': norm_mix_g, 'norm_ffn_g': norm_ffn_g, 'final_norm_g': final_norm_g,
        'rw_r_k': rw_r_k, 'rw_lnx_g': rw_lnx_g, 'rw_lnx_b': rw_lnx_b,
        'router_w': router_w, 'router_bias': router_bias,
    }
    bf = lambda a: a.astype(BF16)
    wb = {
        'sb_w_qkv': _split_weight(sb_w_qkv), 'sb_w_o': _split_weight(sb_w_o),
        'rw_w_o': _split_weight(rw_w_o),
        'rw_mu': rw_mu, 'rw_w_r': bf(rw_w_r), 'rw_w_k': bf(rw_w_k), 'rw_w_v': bf(rw_w_v),
        'rw_w0': rw_w0, 'rw_w1': bf(rw_w1), 'rw_w2': bf(rw_w2), 'rw_a0': rw_a0,
        'rw_a1': bf(rw_a1), 'rw_a2': bf(rw_a2), 'rw_g1': bf(rw_g1), 'rw_g2': bf(rw_g2),
        'rw_k_k': rw_k_k, 'rw_k_a': rw_k_a,
        'exp_w_gate': bf(exp_w_gate), 'exp_w_up': bf(exp_w_up), 'exp_w_down': bf(exp_w_down),
    }
    n_prompt = x_prompt.shape[0]
    mod = _ada(jnp.concatenate([c_prompt, c_sample], axis=0), ada_w, ada_b)
    shift0 = jnp.zeros((n_prompt, 1, D_MODEL), x_prompt.dtype)
    s0 = jnp.zeros((n_prompt, HEADS, HEAD_DIM, HEAD_DIM), F32)
    y_p, k_p, v_p, st_p, sh_p = _trunk(x_prompt, mod[:, :n_prompt], p, wb, None, None, shift0, s0)
    y_s, k_s, v_s, st_s, sh_s = _trunk(x_sample, mod[:, n_prompt:], p, wb, cache_sb_k, cache_sb_v,
                                       state_rw_shift, state_rw_wkv)
    return (y_p, y_s, k_p, v_p, k_s, v_s, st_p, sh_p, st_s, sh_s)
```

```python
import functools

import jax
import jax.numpy as jnp
from jax import lax
from jax.experimental import pallas as pl
from jax.experimental.pallas import tpu as pltpu

F32 = jnp.float32
BF16 = jnp.bfloat16

D_MODEL = 1024
HEADS = 16
HEAD_DIM = 64
LANES = 128
SUBLANES = 8
PAIRS = D_MODEL // LANES
N_EXPERTS = 16
N_GROUPS = 4
EXPERTS_PER_GROUP = 4
D_EXPERT = 512
NORM_EPS = 1e-6
LNX_EPS = 64e-5
ZERO_WEIGHT_CARRY = 104.0
TOKEN_TILE = 512
RWKV_PRE_TILE = 256
QKV_TILE = 256
SCAN_CHUNK = 64
SCAN_BLOCK = 256
SCAN_GROUP = 2
VMEM_LIMIT = 56 * 1024 * 1024

_NT = (((1,), (1,)), ((), ()))


def _cparams(*sem):
    return pltpu.CompilerParams(dimension_semantics=sem, vmem_limit_bytes=VMEM_LIMIT)


def _sigmoid(x):
    return 1.0 / (1.0 + jnp.exp(-x))


def _split2(x):
    hi = x.astype(BF16)
    lo = (x - hi.astype(F32)).astype(BF16)
    return hi, lo


def _dot(a, b):
    return jnp.dot(a, b, preferred_element_type=F32)


def _dot2(x, w2):
    hi, lo = _split2(x)
    return _dot(jnp.concatenate([hi, lo], axis=1), w2)


def _norm_mod(x, g, sc, sh):
    ms = jnp.mean(x * x, axis=-1, keepdims=True)
    y = x * lax.rsqrt(ms + NORM_EPS) * g
    return y * (1.0 + sc) + sh


def _seq_blocks(seqs, length, tile=TOKEN_TILE):
    if length >= tile:
        assert length % tile == 0
        return 1, tile
    assert tile % length == 0 and seqs % (tile // length) == 0 and length % 16 == 0
    return tile // length, length


def _ada_kernel(c_ref, w_ref, b_ref, o_ref):
    c = c_ref[...]
    s = c * _sigmoid(c)
    o_ref[0] = jnp.dot(s, w_ref[0], precision=lax.Precision.HIGHEST,
                       preferred_element_type=F32) + b_ref[0]


def _ada(c, ada_w, ada_b):
    rows = c.shape[0]
    depth = ada_w.shape[0]
    return pl.pallas_call(
        _ada_kernel,
        out_shape=jax.ShapeDtypeStruct((depth, rows, 6 * D_MODEL), F32),
        grid=(depth, 6),
        in_specs=[pl.BlockSpec((rows, D_MODEL), lambda l, j: (0, 0)),
                  pl.BlockSpec((1, D_MODEL, D_MODEL), lambda l, j: (l, 0, j)),
                  pl.BlockSpec((1, 1, D_MODEL), lambda l, j: (l, 0, j))],
        out_specs=pl.BlockSpec((1, rows, D_MODEL), lambda l, j: (l, 0, j)),
        compiler_params=_cparams("arbitrary", "arbitrary"),
        name="ada_mod",
    )(c, ada_w, ada_b.reshape(depth, 1, 6 * D_MODEL))


def _dot3(x, w_hi, w_lo):
    hi, lo = _split2(x)
    rows = x.shape[0]
    both = _dot(jnp.concatenate([hi, lo], axis=0), w_hi)
    return both[:rows] + both[rows:] + _dot(hi, w_lo)


def _split_weight(w):
    hi = w.astype(BF16)
    return hi, (w - hi.astype(F32)).astype(BF16)


def _qkv_kernel(x_ref, g_ref, sc_ref, sh_ref, wh_ref, wl_ref, k_ref, v_ref, *piece_refs):
    bs, tl, _ = x_ref.shape
    h = _norm_mod(x_ref[...], g_ref[...], sc_ref[...], sh_ref[...]).reshape(bs * tl, D_MODEL)
    qkv = _dot3(h, wh_ref[...], wl_ref[...])
    q = qkv[:, :D_MODEL] * (HEAD_DIM ** -0.5)
    k = qkv[:, D_MODEL:2 * D_MODEL]
    v = qkv[:, 2 * D_MODEL:]
    k_ref[...] = k.reshape(bs, tl, D_MODEL)
    v_ref[...] = v.reshape(bs, tl, D_MODEL)
    pieces = _split2(q) + _split2(k) + _split2(v)
    for ref, piece in zip(piece_refs, pieces):
        ref[...] = piece.reshape(bs, tl, D_MODEL)


def _qkv(x, g, sc, sh, w_hi, w_lo):
    seqs, length, _ = x.shape
    bs, tl = _seq_blocks(seqs, length, QKV_TILE)
    tok = pl.BlockSpec((bs, tl, D_MODEL), lambda s, i: (s, i, 0))
    mod = pl.BlockSpec((bs, 1, D_MODEL), lambda s, i: (s, 0, 0))
    wspec = pl.BlockSpec((D_MODEL, 3 * D_MODEL), lambda s, i: (0, 0))
    f32o = jax.ShapeDtypeStruct((seqs, length, D_MODEL), F32)
    bfo = jax.ShapeDtypeStruct((seqs, length, D_MODEL), BF16)
    return pl.pallas_call(
        _qkv_kernel,
        out_shape=(f32o, f32o) + (bfo,) * 6,
        grid=(seqs // bs, length // tl),
        in_specs=[tok, pl.BlockSpec((1, 1, D_MODEL), lambda s, i: (0, 0, 0)), mod, mod, wspec, wspec],
        out_specs=(tok,) * 8,
        compiler_params=_cparams("arbitrary", "arbitrary"),
        name="sb_qkv",
    )(x, g.reshape(1, 1, D_MODEL), sc, sh, w_hi, w_lo)


def _pair_masks(rows):
    lane = lax.broadcasted_iota(jnp.int32, (rows, LANES), 1)
    return lane < HEAD_DIM


def _suffix_matrix():
    r = lax.broadcasted_iota(jnp.int32, (LANES, LANES), 0)
    c = lax.broadcasted_iota(jnp.int32, (LANES, LANES), 1)
    u = jnp.where(r > c, 1.0, 0.0).astype(BF16)
    return jnp.concatenate([u, u], axis=0)


def _query_pieces(q_hi, q_lo):
    hi, lo = _stack_heads(q_hi), _stack_heads(q_lo)
    return jnp.concatenate([hi, lo, hi], axis=1)


def _key_pieces(k_hi, k_lo):
    return jnp.concatenate([k_hi, k_hi, k_lo], axis=1)


def _sb_sweep(q3, keys, values, state, u2, visible):
    rows = q3[0].shape[0]
    tq = rows // 2
    wide = lambda x: jnp.concatenate([x[:tq], x[tq:]], axis=1)
    zs, sps = [], []
    for p in range(len(q3)):
        z = lax.dot_general(q3[p], _key_pieces(*keys[p]), _NT, preferred_element_type=F32)
        sp = jnp.maximum(z, 0.0) + jnp.log(1.0 + jnp.exp(-jnp.abs(z)))
        if visible is not None:
            sp = jnp.where(visible, sp, 0.0)
        zs.append(z)
        sps.append(sp)
    later_all = _dot2(jnp.concatenate(sps, axis=0), u2)
    out = []
    for p in range(len(q3)):
        carry, acc = state[p]
        later = later_all[p * rows:(p + 1) * rows]
        w = jnp.exp(zs[p] - sps[p] - later - carry)
        if visible is not None:
            w = jnp.where(visible, w, 0.0)
        w_hi, w_lo = _split2(w)
        v_hi, v_lo = _stack_heads(values[p][0]), _stack_heads(values[p][1])
        acc = acc + _dot(jnp.concatenate([wide(w_hi), wide(w_lo), wide(w_hi)], axis=1),
                         jnp.concatenate([v_hi, v_hi, v_lo], axis=0))
        out.append((carry + jnp.sum(sps[p], axis=-1, keepdims=True), acc))
    return tuple(out)


def _stack_heads(x, axis=0):
    mask_a = _pair_masks(x.shape[0])
    return jnp.concatenate([jnp.where(mask_a, x, 0), jnp.where(mask_a, 0, x)], axis=axis)


def _diag_visible(tq):
    row = lax.broadcasted_iota(jnp.int32, (2 * tq, LANES), 0)
    col = lax.broadcasted_iota(jnp.int32, (2 * tq, LANES), 1)
    qpos = jnp.where(row >= tq, row - tq, row)
    return col < qpos


def _pair_lanes(p):
    return slice(p * LANES, (p + 1) * LANES)


def _sweep_earlier_tiles(n_tiles, sweep, state):
    def cond(c):
        return jnp.logical_and(c[0] < n_tiles, jnp.logical_not(c[1]))

    def body(c):
        st = sweep(n_tiles - 1 - c[0], c[2])
        low = st[0][0]
        for p in range(1, len(st)):
            low = jnp.minimum(low, st[p][0])
        return c[0] + 1, jnp.min(low) > ZERO_WEIGHT_CARRY, st

    return lax.while_loop(cond, body, (jnp.int32(0), False, state))[2]


def _attn_prompt_kernel(qh_ref, ql_ref, kh_ref, kl_ref, vh_ref, vl_ref, o_ref):
    tq = qh_ref.shape[1]
    qi = pl.program_id(1)
    u2 = _suffix_matrix()
    q3 = [_query_pieces(qh_ref[0, :, _pair_lanes(p)], ql_ref[0, :, _pair_lanes(p)])
          for p in range(PAIRS)]

    def sweep(start, state, visible):
        rows = pl.ds(pl.multiple_of(start, tq), tq)
        tile = lambda ref, p: ref[0, rows, _pair_lanes(p)]
        return _sb_sweep(q3, [(tile(kh_ref, p), tile(kl_ref, p)) for p in range(PAIRS)],
                         [(tile(vh_ref, p), tile(vl_ref, p)) for p in range(PAIRS)],
                         state, u2, visible)

    zero = (jnp.zeros((2 * tq, LANES), F32), jnp.zeros((tq, LANES), F32))
    state = sweep(qi * tq, (zero,) * PAIRS, _diag_visible(tq))
    state = _sweep_earlier_tiles(qi, lambda j, st: sweep(j * tq, st, None), state)
    for p in range(PAIRS):
        o_ref[0, :, _pair_lanes(p)] = state[p][1]


def _attn_prompt(q, k, v):
    batch, length, _ = q[0].shape
    tq = LANES
    qspec = pl.BlockSpec((1, tq, D_MODEL), lambda b, i: (b, i, 0))
    kspec = pl.BlockSpec((1, length, D_MODEL), lambda b, i: (b, 0, 0), pipeline_mode=pl.Buffered(1))
    return pl.pallas_call(
        _attn_prompt_kernel,
        out_shape=jax.ShapeDtypeStruct((batch, length, D_MODEL), F32),
        grid=(batch, length // tq),
        in_specs=[qspec, qspec, kspec, kspec, kspec, kspec],
        out_specs=qspec,
        compiler_params=_cparams("arbitrary", "arbitrary"),
        name="sb_attn_prompt",
    )(*q, *k, *v)


def _attn_sample_kernel(qh_ref, ql_ref, kh_ref, kl_ref, vh_ref, vl_ref, kc_ref, vc_ref, o_ref):
    tq = qh_ref.shape[1]
    past = kc_ref.shape[1]
    u2 = _suffix_matrix()
    pad = jnp.zeros((LANES - tq, LANES), BF16)
    padded = lambda ref, p: jnp.concatenate([ref[0, :, _pair_lanes(p)], pad], axis=0)
    q3 = [_query_pieces(qh_ref[0, :, _pair_lanes(p)], ql_ref[0, :, _pair_lanes(p)])
          for p in range(PAIRS)]
    zero = (jnp.zeros((2 * tq, LANES), F32), jnp.zeros((tq, LANES), F32))
    state = _sb_sweep(q3, [(padded(kh_ref, p), padded(kl_ref, p)) for p in range(PAIRS)],
                      [(padded(vh_ref, p), padded(vl_ref, p)) for p in range(PAIRS)],
                      (zero,) * PAIRS, u2, _diag_visible(tq))

    def sweep(j, st):
        rows = pl.ds(pl.multiple_of(j * LANES, LANES), LANES)
        return _sb_sweep(q3, [_split2(kc_ref[0, rows, _pair_lanes(p)]) for p in range(PAIRS)],
                         [_split2(vc_ref[0, rows, _pair_lanes(p)]) for p in range(PAIRS)],
                         st, u2, None)

    state = _sweep_earlier_tiles(past // LANES, sweep, state)
    for p in range(PAIRS):
        o_ref[0, :, _pair_lanes(p)] = state[p][1]


def _attn_sample(q, k, v, k_cache, v_cache):
    batch, length, _ = q[0].shape
    past = k_cache.shape[1]
    assert length <= LANES and past % LANES == 0
    qspec = pl.BlockSpec((1, length, D_MODEL), lambda b: (b, 0, 0))
    cspec = pl.BlockSpec((1, past, D_MODEL), lambda b: (b, 0, 0))
    return pl.pallas_call(
        _attn_sample_kernel,
        out_shape=jax.ShapeDtypeStruct((batch, length, D_MODEL), F32),
        grid=(batch,),
        in_specs=[qspec] * 6 + [cspec, cspec],
        out_specs=qspec,
        compiler_params=_cparams("arbitrary"),
        name="sb_attn_sample",
    )(*q, *k, *v, k_cache, v_cache)


def _row_max(rows):
    m = rows[0]
    for r in rows[1:]:
        m = jnp.maximum(m, r)
    return m


def _first_hit(rows, m):
    hits = []
    taken = None
    for r in rows:
        eq = r == m
        hit = eq if taken is None else jnp.logical_and(eq, jnp.logical_not(taken))
        taken = eq if taken is None else jnp.logical_or(taken, eq)
        hits.append(hit)
    return hits


def _route(s_rows, sel_rows):
    neg = jnp.float32(-jnp.inf)
    top1, top2, score = [], [], []
    for g in range(N_GROUPS):
        rows = sel_rows[g * EXPERTS_PER_GROUP:(g + 1) * EXPERTS_PER_GROUP]
        m1 = _row_max(rows)
        h1 = _first_hit(rows, m1)
        rest = [jnp.where(h, neg, r) for h, r in zip(h1, rows)]
        m2 = _row_max(rest)
        h2 = _first_hit(rest, m2)
        top1.append(h1)
        top2.append(h2)
        score.append(m1 + m2)
    best = _row_max(score)
    ghit = _first_hit(score, best)
    picked = []
    for g in range(N_GROUPS):
        for e in range(EXPERTS_PER_GROUP):
            picked.append(jnp.logical_and(ghit[g], jnp.logical_or(top1[g][e], top2[g][e])))
    total = None
    for p, s in zip(picked, s_rows):
        term = jnp.where(p, s, 0.0)
        total = term if total is None else total + term
    return [jnp.where(p, s / total, 0.0) for p, s in zip(picked, s_rows)]


def _oproj_kernel(m_ref, x_ref, gate_ref, wh_ref, wl_ref, g_ref, sc_ref, sh_ref, rw_ref, rb_ref,
                  x1_ref, h_ref, gates_ref):
    bs, tl, _ = x_ref.shape
    m = m_ref[...].reshape(bs * tl, D_MODEL)
    mix = _dot3(m, wh_ref[...], wl_ref[...]).reshape(bs, tl, D_MODEL)
    x1 = x_ref[...] + gate_ref[...] * mix
    x1_ref[...] = x1
    h = _norm_mod(x1, g_ref[...], sc_ref[...], sh_ref[...]).reshape(bs * tl, D_MODEL)
    h_ref[...] = h.astype(BF16).reshape(bs, tl, D_MODEL)
    logits = lax.dot_general(rw_ref[...], h, _NT, precision=lax.Precision.HIGHEST,
                             preferred_element_type=F32)
    s = _sigmoid(logits)
    sel = s + rb_ref[...]
    s_rows = [s[e:e + 1, :] for e in range(N_EXPERTS)]
    sel_rows = [sel[e:e + 1, :] for e in range(N_EXPERTS)]
    gates_ref[0] = jnp.concatenate(_route(s_rows, sel_rows), axis=0)


def _oproj_router(m, x, gate, w_pieces, g, sc, sh, router_wt, router_b):
    seqs, length, _ = x.shape
    bs, tl = _seq_blocks(seqs, length)
    n_tiles = (seqs // bs) * (length // tl)
    per_seq = length // tl
    tok = pl.BlockSpec((bs, tl, D_MODEL), lambda s, i: (s, i, 0))
    mod = pl.BlockSpec((bs, 1, D_MODEL), lambda s, i: (s, 0, 0))
    one = pl.BlockSpec((1, 1, D_MODEL), lambda s, i: (0, 0, 0))
    wspec = pl.BlockSpec((D_MODEL, D_MODEL), lambda s, i: (0, 0))
    x1, h, gates = pl.pallas_call(
        _oproj_kernel,
        out_shape=(jax.ShapeDtypeStruct((seqs, length, D_MODEL), F32),
                   jax.ShapeDtypeStruct((seqs, length, D_MODEL), BF16),
                   jax.ShapeDtypeStruct((n_tiles, N_EXPERTS, bs * tl), F32)),
        grid=(seqs // bs, per_seq),
        in_specs=[tok, tok, mod, wspec, wspec, one, mod, mod,
                  pl.BlockSpec((N_EXPERTS, D_MODEL), lambda s, i: (0, 0)),
                  pl.BlockSpec((N_EXPERTS, 1), lambda s, i: (0, 0))],
        out_specs=(tok, tok,
                   pl.BlockSpec((1, N_EXPERTS, bs * tl), lambda s, i: (s * per_seq + i, 0, 0))),
        compiler_params=_cparams("arbitrary", "arbitrary"),
        name="oproj_router",
    )(m, x, gate, *w_pieces, g.reshape(1, 1, D_MODEL), sc, sh, router_wt,
      router_b.reshape(N_EXPERTS, 1))
    gates = jnp.swapaxes(gates, 1, 2).reshape(seqs, length, N_EXPERTS)
    return x1, h, gates


def _moe_kernel(h_ref, gates_ref, x_ref, gate_ref, fg_ref, wg_ref, wu_ref, wd_ref,
                o_ref, acc_ref, *, final_norm):
    bs, tl, _ = x_ref.shape
    e = pl.program_id(2)
    h = h_ref[...].reshape(bs * tl, D_MODEL)
    a = _dot(h, wg_ref[0])
    u = _dot(h, wu_ref[0])
    he = (a * _sigmoid(a) * u).astype(BF16)
    out = _dot(he, wd_ref[0])
    gates = gates_ref[...].reshape(bs * tl, N_EXPERTS)
    lane = lax.broadcasted_iota(jnp.int32, gates.shape, 1)
    ge = jnp.sum(jnp.where(lane == e, gates, 0.0), axis=-1, keepdims=True)
    contrib = ge * out

    @pl.when(e == 0)
    def _():
        acc_ref[...] = contrib

    @pl.when(e > 0)
    def _():
        acc_ref[...] += contrib

    @pl.when(e == N_EXPERTS - 1)
    def _():
        x2 = x_ref[...] + gate_ref[...] * acc_ref[...].reshape(bs, tl, D_MODEL)
        if final_norm:
            ms = jnp.mean(x2 * x2, axis=-1, keepdims=True)
            x2 = x2 * lax.rsqrt(ms + NORM_EPS) * fg_ref[...]
        o_ref[...] = x2


def _moe(h_bf, gates, x, gate, final_g, wg_bf, wu_bf, wd_bf, final_norm):
    seqs, length, _ = x.shape
    bs, tl = _seq_blocks(seqs, length)
    tok = pl.BlockSpec((bs, tl, D_MODEL), lambda s, i, e: (s, i, 0))
    mod = pl.BlockSpec((bs, 1, D_MODEL), lambda s, i, e: (s, 0, 0))
    return pl.pallas_call(
        functools.partial(_moe_kernel, final_norm=final_norm),
        out_shape=jax.ShapeDtypeStruct((seqs, length, D_MODEL), F32),
        grid=(seqs // bs, length // tl, N_EXPERTS),
        in_specs=[tok, pl.BlockSpec((bs, tl, N_EXPERTS), lambda s, i, e: (s, i, 0)), tok, mod,
                  pl.BlockSpec((1, 1, D_MODEL), lambda s, i, e: (0, 0, 0)),
                  pl.BlockSpec((1, D_MODEL, D_EXPERT), lambda s, i, e: (e, 0, 0)),
                  pl.BlockSpec((1, D_MODEL, D_EXPERT), lambda s, i, e: (e, 0, 0)),
                  pl.BlockSpec((1, D_EXPERT, D_MODEL), lambda s, i, e: (e, 0, 0))],
        out_specs=tok,
        scratch_shapes=[pltpu.VMEM((bs * tl, D_MODEL), F32)],
        compiler_params=_cparams("arbitrary", "arbitrary", "arbitrary"),
        name="moe_experts",
    )(h_bf, gates, x, gate, final_g.reshape(1, 1, D_MODEL), wg_bf, wu_bf, wd_bf)


def _head_selectors():
    r = lax.broadcasted_iota(jnp.int32, (D_MODEL, LANES), 0)
    c = lax.broadcasted_iota(jnp.int32, (D_MODEL, LANES), 1)
    sel = jnp.where(r // HEAD_DIM == c, 1.0, 0.0).astype(BF16)
    rt = lax.broadcasted_iota(jnp.int32, (LANES, D_MODEL), 0)
    ct = lax.broadcasted_iota(jnp.int32, (LANES, D_MODEL), 1)
    selt = jnp.where(ct // HEAD_DIM == rt, 1.0, 0.0).astype(BF16)
    return jnp.concatenate([sel, sel], axis=0), jnp.concatenate([selt, selt], axis=0)


def _head_sum(x, sel2, selt2):
    return _dot2(_dot2(x, sel2), selt2)


def _rwkv_pre_kernel(x_ref, prev_ref, shift_ref, g_ref, sc_ref, sh_ref, mu_ref,
                     wr_ref, wk_ref, wv_ref, w0_ref, w1_ref, w2_ref, a0_ref, a1_ref, a2_ref,
                     g1_ref, g2_ref, kk_ref, ka_ref,
                     r_ref, w_ref, k_ref, v_ref, na_ref, kb_ref, go_ref, so_ref):
    bs, tl, _ = x_ref.shape
    t = bs * tl
    i = pl.program_id(1)
    g, sc, sh = g_ref[...], sc_ref[...], sh_ref[...]
    h = _norm_mod(x_ref[...], g, sc, sh)
    so_ref[...] = h[:, tl - 1:tl, :]
    h_before = _norm_mod(prev_ref[:, 7:8, :], g, sc, sh)
    first = jnp.where(i == 0, shift_ref[...], h_before)
    row = lax.broadcasted_iota(jnp.int32, (bs, tl, D_MODEL), 1)
    prev = jnp.where(row == 0, first, pltpu.roll(h, 1, 1))
    h = h.reshape(t, D_MODEL)
    xx = prev.reshape(t, D_MODEL) - h
    mixed = [(h + xx * mu_ref[j:j + 1, :]).astype(BF16) for j in range(6)]
    xr, xw, xk, xv, xa, xg = mixed
    r = _dot(xr, wr_ref[...])
    k = _dot(xk, wk_ref[...])
    v = _dot(xv, wv_ref[...])
    wl = w0_ref[...] + _dot(jnp.tanh(_dot(xw, w1_ref[...])).astype(BF16), w2_ref[...])
    neg = -wl
    softplus = jnp.maximum(neg, 0.0) + jnp.log(1.0 + jnp.exp(-jnp.abs(neg)))
    w = -softplus - 0.5
    log_decay = -jnp.exp(w)
    a = _sigmoid(a0_ref[...] + _dot(_dot(xa, a1_ref[...]).astype(BF16), a2_ref[...]))
    gate = _dot(_sigmoid(_dot(xg, g1_ref[...])).astype(BF16), g2_ref[...])
    sel2, selt2 = _head_selectors()
    kk = k * kk_ref[...]
    norm = jnp.maximum(jnp.sqrt(_head_sum(kk * kk, sel2, selt2)), 1e-12)
    kk = kk / norm
    k = k * (1.0 + (a - 1.0) * ka_ref[...])
    shape = (bs, tl, D_MODEL)
    r_ref[...] = r.reshape(shape)
    w_ref[...] = log_decay.reshape(shape)
    k_ref[...] = k.reshape(shape)
    v_ref[...] = v.reshape(shape)
    na_ref[...] = (-kk).reshape(shape)
    kb_ref[...] = (kk * a).reshape(shape)
    go_ref[...] = gate.reshape(shape)


def _rwkv_pre(x, shift0, g, sc, sh, p):
    seqs, length, _ = x.shape
    bs, tl = _seq_blocks(seqs, length, RWKV_PRE_TILE)
    tok = pl.BlockSpec((bs, tl, D_MODEL), lambda s, i: (s, i, 0))
    prev = pl.BlockSpec((bs, 8, D_MODEL), lambda s, i: (s, jnp.maximum(i * (tl // 8) - 1, 0), 0))
    mod = pl.BlockSpec((bs, 1, D_MODEL), lambda s, i: (s, 0, 0))
    one = pl.BlockSpec((1, 1, D_MODEL), lambda s, i: (0, 0, 0))

    def full(a):
        return pl.BlockSpec(a.shape, lambda s, i: (0,) * a.ndim)

    vec = lambda a: a.reshape(1, D_MODEL)
    weights = [p['rw_mu'], p['rw_w_r'], p['rw_w_k'], p['rw_w_v'], vec(p['rw_w0']), p['rw_w1'],
               p['rw_w2'], vec(p['rw_a0']), p['rw_a1'], p['rw_a2'], p['rw_g1'], p['rw_g2'],
               vec(p['rw_k_k']), vec(p['rw_k_a'])]
    f32o = jax.ShapeDtypeStruct((seqs, length, D_MODEL), F32)
    outs = pl.pallas_call(
        _rwkv_pre_kernel,
        out_shape=(f32o,) * 7 + (jax.ShapeDtypeStruct((seqs, 1, D_MODEL), F32),),
        grid=(seqs // bs, length // tl),
        in_specs=[tok, prev, mod, one, mod, mod] + [full(a) for a in weights],
        out_specs=(tok,) * 7 + (mod,),
        compiler_params=_cparams("arbitrary", "arbitrary"),
        name="rwkv_pre",
    )(x, x, shift0, g.reshape(1, 1, D_MODEL), sc, sh, *weights)
    return outs


_TN = (((0,), (0,)), ((), ()))


def _bdot(a, b, dims=None):
    a, b = a.astype(BF16), b.astype(BF16)
    if dims is None:
        return jnp.dot(a, b, preferred_element_type=F32)
    return lax.dot_general(a, b, dims, preferred_element_type=F32)


def _scan_kernel(r_ref, lw_ref, k_ref, v_ref, a_ref, b_ref, s0_ref, y_ref, st_ref, z_ref, *, chunk):
    tc = r_ref.shape[1]
    t2 = 2 * chunk
    c = pl.program_id(1)
    lane_a = _pair_masks(chunk)
    fold = lambda x: jnp.where(lane_a, x[:chunk], x[chunk:])
    stack = lambda x: jnp.concatenate([x, x], axis=0)
    split_heads = lambda x: _stack_heads(x, axis=0)

    def block_diag(mats):
        n = mats[0].shape[0]
        zero = jnp.zeros((n, n), F32)
        return jnp.concatenate(
            [jnp.concatenate([m if j == i else zero for j in range(len(mats))], axis=1)
             for i, m in enumerate(mats)], axis=0)

    @pl.when(c == 0)
    def _():
        for q in range(PAIRS // SCAN_GROUP):
            z_ref[q] = block_diag([s0_ref[0, 2 * SCAN_GROUP * q + h] for h in range(2 * SCAN_GROUP)])

    row = lax.broadcasted_iota(jnp.int32, (t2, t2), 0)
    col = lax.broadcasted_iota(jnp.int32, (t2, t2), 1)
    same_head = (row // chunk) == (col // chunk)
    strict = jnp.logical_and(same_head, (row % chunk) > (col % chunk))
    incl = jnp.logical_and(same_head, (row % chunk) >= (col % chunk))
    tr = lax.broadcasted_iota(jnp.int32, (chunk, chunk), 0)
    tcol = lax.broadcasted_iota(jnp.int32, (chunk, chunk), 1)
    tri = jnp.where(tr >= tcol, 1.0, 0.0).astype(BF16)
    tri2 = jnp.concatenate([tri, tri], axis=1)
    gw = SCAN_GROUP * LANES
    zr = lax.broadcasted_iota(jnp.int32, (gw, gw), 0)
    zc = lax.broadcasted_iota(jnp.int32, (gw, gw), 1)
    z_diag = (zr // HEAD_DIM) == (zc // HEAD_DIM)
    n_doublings = chunk.bit_length() - 1

    def one_chunk(ci, _):
        rows = pl.ds(pl.multiple_of(ci * chunk, chunk), chunk)
        lw, r, k, v, a, b = [ref[0, rows, :] for ref in (lw_ref, r_ref, k_ref, v_ref, a_ref, b_ref)]
        hi, lo = _split2(lw)
        cum = _dot(tri2, jnp.concatenate([hi, lo], axis=0))
        grow = jnp.exp(-cum)
        decay = jnp.exp(cum)
        total = decay[chunk - 1:chunk, :]
        a_t = a * jnp.exp(cum - lw)
        r_t = r * decay
        b_t = b * grow
        k_t = k * grow
        b_end = b_t * total
        k_end = k_t * total
        for q in range(PAIRS // SCAN_GROUP):
            pairs = [SCAN_GROUP * q + j for j in range(SCAN_GROUP)]
            group = slice(q * gw, (q + 1) * gw)
            l_ab, l_ak, l_rb, l_rk = [], [], [], []
            for p in pairs:
                lanes = _pair_lanes(p)
                g = _bdot(jnp.concatenate([split_heads(a_t[:, lanes]), split_heads(r_t[:, lanes])], axis=0),
                          jnp.concatenate([stack(b_t[:, lanes]), stack(k_t[:, lanes])], axis=0),
                          _NT)
                l_ab.append(jnp.where(strict, g[:t2, :t2], 0.0))
                l_ak.append(jnp.where(strict, g[:t2, t2:], 0.0))
                l_rb.append(jnp.where(incl, g[t2:, :t2], 0.0))
                l_rk.append(jnp.where(incl, g[t2:, t2:], 0.0))
            z = z_ref[q]
            applied = _bdot(jnp.concatenate([a_t[:, group], r_t[:, group]], axis=0), z, _NT)
            per_pair = lambda x: [x[:, j * LANES:(j + 1) * LANES] for j in range(SCAN_GROUP)]
            vv = jnp.concatenate([stack(v[:, _pair_lanes(p)]) for p in pairs], axis=0)
            x = (jnp.concatenate([stack(m) for m in per_pair(applied[:chunk])], axis=0)
                 + _bdot(block_diag(l_ak), vv))
            power = block_diag(l_ab)
            for i in range(n_doublings):
                x = x + _bdot(power, x)
                if i + 1 < n_doublings:
                    power = _bdot(power, power)
            u = [fold(x[j * t2:(j + 1) * t2]) for j in range(SCAN_GROUP)]
            uu = jnp.concatenate([stack(m) for m in u], axis=0)
            ys = _bdot(jnp.concatenate([block_diag(l_rb), block_diag(l_rk)], axis=1),
                       jnp.concatenate([uu, vv], axis=0))
            y = applied[chunk:] + jnp.concatenate(
                [fold(ys[j * t2:(j + 1) * t2]) for j in range(SCAN_GROUP)], axis=1)
            y_ref[0, rows, group] = y
            grown = _bdot(jnp.concatenate([jnp.concatenate(u, axis=1), v[:, group]], axis=0),
                          jnp.concatenate([b_end[:, group], k_end[:, group]], axis=0), _TN)
            z_ref[q] = z * total[:, group] + jnp.where(z_diag, grown, 0.0)
        return 0

    lax.fori_loop(0, tc // chunk, one_chunk, 0)

    @pl.when(c == pl.num_programs(1) - 1)
    def _():
        for q in range(PAIRS // SCAN_GROUP):
            z = z_ref[q]
            for h in range(2 * SCAN_GROUP):
                block = slice(h * HEAD_DIM, (h + 1) * HEAD_DIM)
                st_ref[0, 2 * SCAN_GROUP * q + h] = z[block, block]


def _scan(r, lw, k, v, na, kb, s0):
    seqs, length, _ = r.shape
    chunk = min(length, SCAN_CHUNK)
    tc = min(length, SCAN_BLOCK)
    assert length % tc == 0 and tc % chunk == 0 and chunk & (chunk - 1) == 0
    tok = pl.BlockSpec((1, tc, D_MODEL), lambda s, c: (s, c, 0))
    st = pl.BlockSpec((1, HEADS, HEAD_DIM, HEAD_DIM), lambda s, c: (s, 0, 0, 0))
    return pl.pallas_call(
        functools.partial(_scan_kernel, chunk=chunk),
        out_shape=(jax.ShapeDtypeStruct((seqs, length, D_MODEL), F32),
                   jax.ShapeDtypeStruct((seqs, HEADS, HEAD_DIM, HEAD_DIM), F32)),
        grid=(seqs, length // tc),
        in_specs=[tok] * 6 + [st],
        out_specs=(tok, st),
        scratch_shapes=[pltpu.VMEM((PAIRS // SCAN_GROUP, SCAN_GROUP * LANES, SCAN_GROUP * LANES), F32)],
        compiler_params=_cparams("arbitrary", "arbitrary"),
        name="rwkv_scan",
    )(r, lw, k, v, na, kb, s0)


def _rwkv_post_kernel(y_ref, r_ref, k_ref, v_ref, g_ref, lg_ref, lb_ref, rk_ref, o_ref):
    bs, tl, _ = y_ref.shape
    t = bs * tl
    sel2, selt2 = _head_selectors()
    y = y_ref[...].reshape(t, D_MODEL)
    mu = _head_sum(y, sel2, selt2) * (1.0 / HEAD_DIM)
    d = y - mu
    var = _head_sum(d * d, sel2, selt2) * (1.0 / HEAD_DIM)
    yn = d * lax.rsqrt(var + LNX_EPS) * lg_ref[...] + lb_ref[...]
    rk = r_ref[...].reshape(t, D_MODEL) * k_ref[...].reshape(t, D_MODEL) * rk_ref[...]
    bonus = _head_sum(rk, sel2, selt2) * v_ref[...].reshape(t, D_MODEL)
    o_ref[...] = ((yn + bonus) * g_ref[...].reshape(t, D_MODEL)).reshape(bs, tl, D_MODEL)


def _rwkv_post(y, r, k, v, gate, lnx_g, lnx_b, r_k):
    seqs, length, _ = y.shape
    bs, tl = _seq_blocks(seqs, length)
    tok = pl.BlockSpec((bs, tl, D_MODEL), lambda s, i: (s, i, 0))
    vec = pl.BlockSpec((1, D_MODEL), lambda s, i: (0, 0))
    return pl.pallas_call(
        _rwkv_post_kernel,
        out_shape=jax.ShapeDtypeStruct((seqs, length, D_MODEL), F32),
        grid=(seqs // bs, length // tl),
        in_specs=[tok] * 5 + [vec] * 3,
        out_specs=tok,
        compiler_params=_cparams("arbitrary", "arbitrary"),
        name="rwkv_post",
    )(y, r, k, v, gate, lnx_g.reshape(1, D_MODEL), lnx_b.reshape(1, D_MODEL),
      r_k.reshape(1, D_MODEL))


def _trunk(x, mod, p, wb, k_cache, v_cache, shift0, s0):
    seqs, length, _ = x.shape
    part = lambda layer, j: mod[layer, :, j * D_MODEL:(j + 1) * D_MODEL].reshape(seqs, 1, D_MODEL)
    router_wt = p['router_w'].T
    sh1, sc1, g1, sh2, sc2, g2 = [part(0, j) for j in range(6)]
    k, v, qh, ql, kh, kl, vh, vl = _qkv(x, p['norm_mix_g'][0], sc1, sh1, *wb['sb_w_qkv'])
    if k_cache is None:
        attn = _attn_prompt((qh, ql), (kh, kl), (vh, vl))
    else:
        past = k_cache.shape[1]
        attn = _attn_sample((qh, ql), (kh, kl), (vh, vl), k_cache.reshape(seqs, past, D_MODEL),
                            v_cache.reshape(seqs, past, D_MODEL))
    x, h, gates = _oproj_router(attn, x, g1, wb['sb_w_o'], p['norm_ffn_g'][0], sc2, sh2,
                                router_wt, p['router_bias'])
    x = _moe(h, gates, x, g2, p['final_norm_g'], wb['exp_w_gate'][0], wb['exp_w_up'][0],
             wb['exp_w_down'][0], final_norm=False)
    sb_k = k.reshape(seqs, length, HEADS, HEAD_DIM)
    sb_v = v.reshape(seqs, length, HEADS, HEAD_DIM)
    sh1, sc1, g1, sh2, sc2, g2 = [part(1, j) for j in range(6)]
    r, w, kq, vv, na, kb, gate, shift = _rwkv_pre(x, shift0, p['norm_mix_g'][1], sc1, sh1, wb)
    y, state = _scan(r, w, kq, vv, na, kb, s0)
    m = _rwkv_post(y, r, kq, vv, gate, p['rw_lnx_g'], p['rw_lnx_b'], p['rw_r_k'])
    x, h, gates = _oproj_router(m, x, g1, wb['rw_w_o'], p['norm_ffn_g'][1], sc2, sh2,
                                router_wt, p['router_bias'])
    y_out = _moe(h, gates, x, g2, p['final_norm_g'], wb['exp_w_gate'][1], wb['exp_w_up'][1],
                 wb['exp_w_down'][1], final_norm=True)
    return y_out, sb_k, sb_v, state, shift


def kernel(x_prompt, x_sample, cache_sb_k, cache_sb_v, state_rw_wkv, state_rw_shift,
           c_prompt, c_sample, ada_w, ada_b, norm_mix_g, norm_ffn_g, final_norm_g,
           sb_w_qkv, sb_w_o, rw_mu, rw_w_r, rw_w_k, rw_w_v, rw_w0, rw_w1, rw_w2,
           rw_a0, rw_a1, rw_a2, rw_g1, rw_g2, rw_k_k, rw_k_a, rw_r_k, rw_lnx_g, rw_lnx_b,
           rw_w_o, router_w, router_bias, exp_w_gate, exp_w_up, exp_w_down):
    p = {
        'norm_mix_g': norm_mix_g, 'norm_ffn_g': norm_ffn_g, 'final_norm_g': final_norm_g,
        'rw_r_k': rw_r_k, 'rw_lnx_g': rw_lnx_g, 'rw_lnx_b': rw_lnx_b,
        'router_w': router_w, 'router_bias': router_bias,
    }
    bf = lambda a: a.astype(BF16)
    wb = {
        'sb_w_qkv': _split_weight(sb_w_qkv), 'sb_w_o': _split_weight(sb_w_o),
        'rw_w_o': _split_weight(rw_w_o),
        'rw_mu': rw_mu, 'rw_w_r': bf(rw_w_r), 'rw_w_k': bf(rw_w_k), 'rw_w_v': bf(rw_w_v),
        'rw_w0': rw_w0, 'rw_w1': bf(rw_w1), 'rw_w2': bf(rw_w2), 'rw_a0': rw_a0,
        'rw_a1': bf(rw_a1), 'rw_a2': bf(rw_a2), 'rw_g1': bf(rw_g1), 'rw_g2': bf(rw_g2),
        'rw_k_k': rw_k_k, 'rw_k_a': rw_k_a,
        'exp_w_gate': bf(exp_w_gate), 'exp_w_up': bf(exp_w_up), 'exp_w_down': bf(exp_w_down),
    }
    n_prompt = x_prompt.shape[0]
    mod = _ada(jnp.concatenate([c_prompt, c_sample], axis=0), ada_w, ada_b)
    shift0 = jnp.zeros((n_prompt, 1, D_MODEL), x_prompt.dtype)
    s0 = jnp.zeros((n_prompt, HEADS, HEAD_DIM, HEAD_DIM), F32)
    y_p, k_p, v_p, st_p, sh_p = _trunk(x_prompt, mod[:, :n_prompt], p, wb, None, None, shift0, s0)
    y_s, k_s, v_s, st_s, sh_s = _trunk(x_sample, mod[:, n_prompt:], p, wb, cache_sb_k, cache_sb_v,
                                       state_rw_shift, state_rw_wkv)
    return (y_p, y_s, k_p, v_p, k_s, v_s, st_p, sh_p, st_s, sh_s)
```

```python
import functools

import jax
import jax.numpy as jnp
from jax import lax
from jax.experimental import pallas as pl
from jax.experimental.pallas import tpu as pltpu

F32 = jnp.float32
BF16 = jnp.bfloat16

D_MODEL = 1024
HEADS = 16
HEAD_DIM = 64
LANES = 128
SUBLANES = 8
PAIRS = D_MODEL // LANES
N_EXPERTS = 16
N_GROUPS = 4
EXPERTS_PER_GROUP = 4
D_EXPERT = 512
NORM_EPS = 1e-6
LNX_EPS = 64e-5
ZERO_WEIGHT_CARRY = 104.0
TOKEN_TILE = 512
RWKV_PRE_TILE = 256
QKV_TILE = 256
SCAN_CHUNK = 64
SCAN_BLOCK = 256
SCAN_GROUP = 2
MOE_TILE = 512
VMEM_LIMIT = 56 * 1024 * 1024

_NT = (((1,), (1,)), ((), ()))


def _cparams(*sem):
    return pltpu.CompilerParams(dimension_semantics=sem, vmem_limit_bytes=VMEM_LIMIT)


def _sigmoid(x):
    return 1.0 / (1.0 + jnp.exp(-x))


def _split2(x):
    hi = x.astype(BF16)
    lo = (x - hi.astype(F32)).astype(BF16)
    return hi, lo


def _dot(a, b):
    return jnp.dot(a, b, preferred_element_type=F32)


def _dot2(x, w2):
    hi, lo = _split2(x)
    return _dot(jnp.concatenate([hi, lo], axis=1), w2)


def _norm_mod(x, g, sc, sh):
    ms = jnp.mean(x * x, axis=-1, keepdims=True)
    y = x * lax.rsqrt(ms + NORM_EPS) * g
    return y * (1.0 + sc) + sh


def _seq_blocks(seqs, length, tile=TOKEN_TILE):
    if length >= tile:
        assert length % tile == 0
        return 1, tile
    assert tile % length == 0 and seqs % (tile // length) == 0 and length % 16 == 0
    return tile // length, length


def _ada_kernel(c_ref, w_ref, b_ref, o_ref):
    c = c_ref[...]
    s = c * _sigmoid(c)
    o_ref[0] = jnp.dot(s, w_ref[0], precision=lax.Precision.HIGHEST,
                       preferred_element_type=F32) + b_ref[0]


def _ada(c, ada_w, ada_b):
    rows = c.shape[0]
    depth = ada_w.shape[0]
    return pl.pallas_call(
        _ada_kernel,
        out_shape=jax.ShapeDtypeStruct((depth, rows, 6 * D_MODEL), F32),
        grid=(depth, 6),
        in_specs=[pl.BlockSpec((rows, D_MODEL), lambda l, j: (0, 0)),
                  pl.BlockSpec((1, D_MODEL, D_MODEL), lambda l, j: (l, 0, j)),
                  pl.BlockSpec((1, 1, D_MODEL), lambda l, j: (l, 0, j))],
        out_specs=pl.BlockSpec((1, rows, D_MODEL), lambda l, j: (l, 0, j)),
        compiler_params=_cparams("arbitrary", "arbitrary"),
        name="ada_mod",
    )(c, ada_w, ada_b.reshape(depth, 1, 6 * D_MODEL))


def _dot3(x, w_hi, w_lo):
    hi, lo = _split2(x)
    rows = x.shape[0]
    both = _dot(jnp.concatenate([hi, lo], axis=0), w_hi)
    return both[:rows] + both[rows:] + _dot(hi, w_lo)


def _split_weight(w):
    hi = w.astype(BF16)
    return hi, (w - hi.astype(F32)).astype(BF16)


def _qkv_kernel(x_ref, g_ref, sc_ref, sh_ref, wh_ref, wl_ref, k_ref, v_ref, *piece_refs):
    bs, tl, _ = x_ref.shape
    h = _norm_mod(x_ref[...], g_ref[...], sc_ref[...], sh_ref[...]).reshape(bs * tl, D_MODEL)
    qkv = _dot3(h, wh_ref[...], wl_ref[...])
    q = qkv[:, :D_MODEL] * (HEAD_DIM ** -0.5)
    k = qkv[:, D_MODEL:2 * D_MODEL]
    v = qkv[:, 2 * D_MODEL:]
    k_ref[...] = k.reshape(bs, tl, D_MODEL)
    v_ref[...] = v.reshape(bs, tl, D_MODEL)
    pieces = _split2(q) + _split2(k) + _split2(v)
    for ref, piece in zip(piece_refs, pieces):
        ref[...] = piece.reshape(bs, tl, D_MODEL)


def _qkv(x, g, sc, sh, w_hi, w_lo):
    seqs, length, _ = x.shape
    bs, tl = _seq_blocks(seqs, length, QKV_TILE)
    tok = pl.BlockSpec((bs, tl, D_MODEL), lambda s, i: (s, i, 0))
    mod = pl.BlockSpec((bs, 1, D_MODEL), lambda s, i: (s, 0, 0))
    wspec = pl.BlockSpec((D_MODEL, 3 * D_MODEL), lambda s, i: (0, 0))
    f32o = jax.ShapeDtypeStruct((seqs, length, D_MODEL), F32)
    bfo = jax.ShapeDtypeStruct((seqs, length, D_MODEL), BF16)
    return pl.pallas_call(
        _qkv_kernel,
        out_shape=(f32o, f32o) + (bfo,) * 6,
        grid=(seqs // bs, length // tl),
        in_specs=[tok, pl.BlockSpec((1, 1, D_MODEL), lambda s, i: (0, 0, 0)), mod, mod, wspec, wspec],
        out_specs=(tok,) * 8,
        compiler_params=_cparams("arbitrary", "arbitrary"),
        name="sb_qkv",
    )(x, g.reshape(1, 1, D_MODEL), sc, sh, w_hi, w_lo)


def _pair_masks(rows):
    lane = lax.broadcasted_iota(jnp.int32, (rows, LANES), 1)
    return lane < HEAD_DIM


def _suffix_matrix():
    r = lax.broadcasted_iota(jnp.int32, (LANES, LANES), 0)
    c = lax.broadcasted_iota(jnp.int32, (LANES, LANES), 1)
    u = jnp.where(r > c, 1.0, 0.0).astype(BF16)
    return jnp.concatenate([u, u], axis=0)


def _query_pieces(q_hi, q_lo):
    hi, lo = _stack_heads(q_hi), _stack_heads(q_lo)
    return jnp.concatenate([hi, lo, hi], axis=1)


def _key_pieces(k_hi, k_lo):
    return jnp.concatenate([k_hi, k_hi, k_lo], axis=1)


def _sb_sweep(q3, keys, values, state, u2, visible):
    rows = q3[0].shape[0]
    tq = rows // 2
    wide = lambda x: jnp.concatenate([x[:tq], x[tq:]], axis=1)
    zs, sps = [], []
    for p in range(len(q3)):
        z = lax.dot_general(q3[p], _key_pieces(*keys[p]), _NT, preferred_element_type=F32)
        sp = jnp.maximum(z, 0.0) + jnp.log(1.0 + jnp.exp(-jnp.abs(z)))
        if visible is not None:
            sp = jnp.where(visible, sp, 0.0)
        zs.append(z)
        sps.append(sp)
    later_all = _dot2(jnp.concatenate(sps, axis=0), u2)
    out = []
    for p in range(len(q3)):
        carry, acc = state[p]
        later = later_all[p * rows:(p + 1) * rows]
        w = jnp.exp(zs[p] - sps[p] - later - carry)
        if visible is not None:
            w = jnp.where(visible, w, 0.0)
        w_hi, w_lo = _split2(w)
        v_hi, v_lo = _stack_heads(values[p][0]), _stack_heads(values[p][1])
        acc = acc + _dot(jnp.concatenate([wide(w_hi), wide(w_lo), wide(w_hi)], axis=1),
                         jnp.concatenate([v_hi, v_hi, v_lo], axis=0))
        out.append((carry + jnp.sum(sps[p], axis=-1, keepdims=True), acc))
    return tuple(out)


def _stack_heads(x, axis=0):
    mask_a = _pair_masks(x.shape[0])
    return jnp.concatenate([jnp.where(mask_a, x, 0), jnp.where(mask_a, 0, x)], axis=axis)


def _diag_visible(tq):
    row = lax.broadcasted_iota(jnp.int32, (2 * tq, LANES), 0)
    col = lax.broadcasted_iota(jnp.int32, (2 * tq, LANES), 1)
    qpos = jnp.where(row >= tq, row - tq, row)
    return col < qpos


def _pair_lanes(p):
    return slice(p * LANES, (p + 1) * LANES)


def _sweep_earlier_tiles(n_tiles, sweep, state):
    def cond(c):
        return jnp.logical_and(c[0] < n_tiles, jnp.logical_not(c[1]))

    def body(c):
        st = sweep(n_tiles - 1 - c[0], c[2])
        low = st[0][0]
        for p in range(1, len(st)):
            low = jnp.minimum(low, st[p][0])
        return c[0] + 1, jnp.min(low) > ZERO_WEIGHT_CARRY, st

    return lax.while_loop(cond, body, (jnp.int32(0), False, state))[2]


def _attn_prompt_kernel(qh_ref, ql_ref, kh_ref, kl_ref, vh_ref, vl_ref, o_ref):
    tq = qh_ref.shape[1]
    qi = pl.program_id(1)
    u2 = _suffix_matrix()
    q3 = [_query_pieces(qh_ref[0, :, _pair_lanes(p)], ql_ref[0, :, _pair_lanes(p)])
          for p in range(PAIRS)]

    def sweep(start, state, visible):
        rows = pl.ds(pl.multiple_of(start, tq), tq)
        tile = lambda ref, p: ref[0, rows, _pair_lanes(p)]
        return _sb_sweep(q3, [(tile(kh_ref, p), tile(kl_ref, p)) for p in range(PAIRS)],
                         [(tile(vh_ref, p), tile(vl_ref, p)) for p in range(PAIRS)],
                         state, u2, visible)

    zero = (jnp.zeros((2 * tq, LANES), F32), jnp.zeros((tq, LANES), F32))
    state = sweep(qi * tq, (zero,) * PAIRS, _diag_visible(tq))
    state = _sweep_earlier_tiles(qi, lambda j, st: sweep(j * tq, st, None), state)
    for p in range(PAIRS):
        o_ref[0, :, _pair_lanes(p)] = state[p][1]


def _attn_prompt(q, k, v):
    batch, length, _ = q[0].shape
    tq = LANES
    qspec = pl.BlockSpec((1, tq, D_MODEL), lambda b, i: (b, i, 0))
    kspec = pl.BlockSpec((1, length, D_MODEL), lambda b, i: (b, 0, 0), pipeline_mode=pl.Buffered(1))
    return pl.pallas_call(
        _attn_prompt_kernel,
        out_shape=jax.ShapeDtypeStruct((batch, length, D_MODEL), F32),
        grid=(batch, length // tq),
        in_specs=[qspec, qspec, kspec, kspec, kspec, kspec],
        out_specs=qspec,
        compiler_params=_cparams("arbitrary", "arbitrary"),
        name="sb_attn_prompt",
    )(*q, *k, *v)


def _attn_sample_kernel(qh_ref, ql_ref, kh_ref, kl_ref, vh_ref, vl_ref, kc_ref, vc_ref, o_ref):
    tq = qh_ref.shape[1]
    past = kc_ref.shape[1]
    u2 = _suffix_matrix()
    pad = jnp.zeros((LANES - tq, LANES), BF16)
    padded = lambda ref, p: jnp.concatenate([ref[0, :, _pair_lanes(p)], pad], axis=0)
    q3 = [_query_pieces(qh_ref[0, :, _pair_lanes(p)], ql_ref[0, :, _pair_lanes(p)])
          for p in range(PAIRS)]
    zero = (jnp.zeros((2 * tq, LANES), F32), jnp.zeros((tq, LANES), F32))
    state = _sb_sweep(q3, [(padded(kh_ref, p), padded(kl_ref, p)) for p in range(PAIRS)],
                      [(padded(vh_ref, p), padded(vl_ref, p)) for p in range(PAIRS)],
                      (zero,) * PAIRS, u2, _diag_visible(tq))

    def sweep(j, st):
        rows = pl.ds(pl.multiple_of(j * LANES, LANES), LANES)
        return _sb_sweep(q3, [_split2(kc_ref[0, rows, _pair_lanes(p)]) for p in range(PAIRS)],
                         [_split2(vc_ref[0, rows, _pair_lanes(p)]) for p in range(PAIRS)],
                         st, u2, None)

    state = _sweep_earlier_tiles(past // LANES, sweep, state)
    for p in range(PAIRS):
        o_ref[0, :, _pair_lanes(p)] = state[p][1]


def _attn_sample(q, k, v, k_cache, v_cache):
    batch, length, _ = q[0].shape
    past = k_cache.shape[1]
    assert length <= LANES and past % LANES == 0
    qspec = pl.BlockSpec((1, length, D_MODEL), lambda b: (b, 0, 0))
    cspec = pl.BlockSpec((1, past, D_MODEL), lambda b: (b, 0, 0))
    return pl.pallas_call(
        _attn_sample_kernel,
        out_shape=jax.ShapeDtypeStruct((batch, length, D_MODEL), F32),
        grid=(batch,),
        in_specs=[qspec] * 6 + [cspec, cspec],
        out_specs=qspec,
        compiler_params=_cparams("arbitrary"),
        name="sb_attn_sample",
    )(*q, *k, *v, k_cache, v_cache)


def _row_max(rows):
    m = rows[0]
    for r in rows[1:]:
        m = jnp.maximum(m, r)
    return m


def _first_hit(rows, m):
    hits = []
    taken = None
    for r in rows:
        eq = r == m
        hit = eq if taken is None else jnp.logical_and(eq, jnp.logical_not(taken))
        taken = eq if taken is None else jnp.logical_or(taken, eq)
        hits.append(hit)
    return hits


def _route(s_rows, sel_rows):
    neg = jnp.float32(-jnp.inf)
    top1, top2, score = [], [], []
    for g in range(N_GROUPS):
        rows = sel_rows[g * EXPERTS_PER_GROUP:(g + 1) * EXPERTS_PER_GROUP]
        m1 = _row_max(rows)
        h1 = _first_hit(rows, m1)
        rest = [jnp.where(h, neg, r) for h, r in zip(h1, rows)]
        m2 = _row_max(rest)
        h2 = _first_hit(rest, m2)
        top1.append(h1)
        top2.append(h2)
        score.append(m1 + m2)
    best = _row_max(score)
    ghit = _first_hit(score, best)
    picked = []
    for g in range(N_GROUPS):
        for e in range(EXPERTS_PER_GROUP):
            picked.append(jnp.logical_and(ghit[g], jnp.logical_or(top1[g][e], top2[g][e])))
    total = None
    for p, s in zip(picked, s_rows):
        term = jnp.where(p, s, 0.0)
        total = term if total is None else total + term
    gates = [jnp.where(p, s / total, 0.0) for p, s in zip(picked, s_rows)]
    in_group = []
    for e in range(EXPERTS_PER_GROUP):
        row = gates[e]
        for g in range(1, N_GROUPS):
            row = row + gates[g * EXPERTS_PER_GROUP + e]
        in_group.append(row)
    group = jnp.zeros_like(best)
    for g in range(1, N_GROUPS):
        group = jnp.where(ghit[g], float(g), group)
    return in_group, group


def _oproj_kernel(m_ref, x_ref, gate_ref, wh_ref, wl_ref, g_ref, sc_ref, sh_ref, rw_ref, rb_ref,
                  x1_ref, h_ref, route_ref):
    bs, tl, _ = x_ref.shape
    m = m_ref[...].reshape(bs * tl, D_MODEL)
    mix = _dot3(m, wh_ref[...], wl_ref[...]).reshape(bs, tl, D_MODEL)
    x1 = x_ref[...] + gate_ref[...] * mix
    x1_ref[...] = x1
    h = _norm_mod(x1, g_ref[...], sc_ref[...], sh_ref[...])
    h_ref[...] = h
    logits = lax.dot_general(rw_ref[...], h.reshape(bs * tl, D_MODEL), _NT,
                             precision=lax.Precision.HIGHEST,
                             preferred_element_type=F32)
    s = _sigmoid(logits)
    sel = s + rb_ref[...]
    s_rows = [s[e:e + 1, :] for e in range(N_EXPERTS)]
    sel_rows = [sel[e:e + 1, :] for e in range(N_EXPERTS)]
    in_group, group = _route(s_rows, sel_rows)
    pad = [jnp.zeros_like(group)] * (SUBLANES - EXPERTS_PER_GROUP - 1)
    route_ref[0] = jnp.concatenate(in_group + [group] + pad, axis=0)


def _oproj_router(m, x, gate, w_pieces, g, sc, sh, router_wt, router_b):
    seqs, length, _ = x.shape
    bs, tl = _seq_blocks(seqs, length)
    n_tiles = (seqs // bs) * (length // tl)
    per_seq = length // tl
    tok = pl.BlockSpec((bs, tl, D_MODEL), lambda s, i: (s, i, 0))
    mod = pl.BlockSpec((bs, 1, D_MODEL), lambda s, i: (s, 0, 0))
    one = pl.BlockSpec((1, 1, D_MODEL), lambda s, i: (0, 0, 0))
    wspec = pl.BlockSpec((D_MODEL, D_MODEL), lambda s, i: (0, 0))
    f32o = jax.ShapeDtypeStruct((seqs, length, D_MODEL), F32)
    x1, h, route = pl.pallas_call(
        _oproj_kernel,
        out_shape=(f32o, f32o, jax.ShapeDtypeStruct((n_tiles, SUBLANES, bs * tl), F32)),
        grid=(seqs // bs, per_seq),
        in_specs=[tok, tok, mod, wspec, wspec, one, mod, mod,
                  pl.BlockSpec((N_EXPERTS, D_MODEL), lambda s, i: (0, 0)),
                  pl.BlockSpec((N_EXPERTS, 1), lambda s, i: (0, 0))],
        out_specs=(tok, tok,
                   pl.BlockSpec((1, SUBLANES, bs * tl), lambda s, i: (s * per_seq + i, 0, 0))),
        compiler_params=_cparams("arbitrary", "arbitrary"),
        name="oproj_router",
    )(m, x, gate, *w_pieces, g.reshape(1, 1, D_MODEL), sc, sh, router_wt,
      router_b.reshape(N_EXPERTS, 1))
    route = jnp.swapaxes(route, 1, 2).reshape(seqs * length, SUBLANES)
    return x1, h, route[:, :EXPERTS_PER_GROUP], route[:, EXPERTS_PER_GROUP].astype(jnp.int32)


def _dispatch_plan(group, gates, tile):
    n = group.shape[0]
    n_slots = n + N_GROUPS * tile
    onehot = (group[:, None] == jnp.arange(N_GROUPS)[None, :]).astype(jnp.int32)
    counts = jnp.sum(onehot, axis=0)
    padded = (counts + tile - 1) // tile * tile
    ends = jnp.cumsum(padded)
    starts = ends - padded
    rank = jnp.take_along_axis(jnp.cumsum(onehot, axis=0) - onehot, group[:, None], axis=1)[:, 0]
    slot = starts[group] + rank
    src = jnp.zeros((n_slots,), jnp.int32).at[slot].set(jnp.arange(n, dtype=jnp.int32))
    slot_gates = jnp.zeros((n_slots, EXPERTS_PER_GROUP), F32).at[slot].set(gates)
    tile_start = jnp.arange(n_slots // tile, dtype=jnp.int32) * tile
    tile_group = jnp.minimum(jnp.sum(tile_start[:, None] >= ends[None, :], axis=1), N_GROUPS - 1)
    return src, slot.astype(jnp.int32), slot_gates, tile_group.astype(jnp.int32)


def _start_row_gather(idx_ref, base, src_hbm, dst_ref, sem):
    def body(r, _):
        row = idx_ref[base + r]
        pltpu.make_async_copy(src_hbm.at[pl.ds(row, 1), :], dst_ref.at[pl.ds(r, 1), :], sem).start()
        return 0

    lax.fori_loop(0, dst_ref.shape[0], body, 0, unroll=8)


def _wait_row_gather(src_hbm, dst_ref, sem):
    pltpu.make_async_copy(src_hbm.at[pl.ds(0, dst_ref.shape[0]), :], dst_ref, sem).wait()


def _moe_kernel(src_ref, tg_ref, h_hbm, gates_ref, wg_ref, wu_ref, wd_ref, o_ref, hbuf, sem):
    t = pl.program_id(0)
    tile = o_ref.shape[0]
    slot = t % 2

    @pl.when(t == 0)
    def _():
        _start_row_gather(src_ref, 0, h_hbm, hbuf.at[0], sem.at[0])

    @pl.when(t + 1 < pl.num_programs(0))
    def _():
        _start_row_gather(src_ref, (t + 1) * tile, h_hbm, hbuf.at[1 - slot], sem.at[1 - slot])

    _wait_row_gather(h_hbm, hbuf.at[slot], sem.at[slot])
    h = hbuf[slot].astype(BF16)
    gates = gates_ref[...]
    acc = jnp.zeros((tile, D_MODEL), F32)
    for e in range(EXPERTS_PER_GROUP):
        a = _dot(h, wg_ref[e])
        u = _dot(h, wu_ref[e])
        he = (a * _sigmoid(a) * u).astype(BF16)
        acc = acc + gates[:, e:e + 1] * _dot(he, wd_ref[e])
    o_ref[...] = acc


def _moe_experts(h, src, slot_gates, tile_group, wg_bf, wu_bf, wd_bf, tile):
    n_slots = src.shape[0]
    group_w = lambda shape: pl.BlockSpec((EXPERTS_PER_GROUP,) + shape, lambda t, src, tg: (tg[t], 0, 0))
    return pl.pallas_call(
        _moe_kernel,
        out_shape=jax.ShapeDtypeStruct((n_slots, D_MODEL), F32),
        grid_spec=pltpu.PrefetchScalarGridSpec(
            num_scalar_prefetch=2,
            grid=(n_slots // tile,),
            in_specs=[pl.BlockSpec(memory_space=pl.ANY),
                      pl.BlockSpec((tile, EXPERTS_PER_GROUP), lambda t, src, tg: (t, 0)),
                      group_w((D_MODEL, D_EXPERT)), group_w((D_MODEL, D_EXPERT)),
                      group_w((D_EXPERT, D_MODEL))],
            out_specs=pl.BlockSpec((tile, D_MODEL), lambda t, src, tg: (t, 0)),
            scratch_shapes=[pltpu.VMEM((2, tile, D_MODEL), F32), pltpu.SemaphoreType.DMA((2,))]),
        compiler_params=_cparams("arbitrary"),
        name="moe_experts",
    )(src, tile_group, h, slot_gates, wg_bf, wu_bf, wd_bf)


def _moe_combine_kernel(slot_ref, f_hbm, x_ref, gate_ref, fg_ref, o_ref, fbuf, sem, *, final_norm):
    bs, tl, _ = x_ref.shape
    tile = bs * tl
    t = pl.program_id(0) * pl.num_programs(1) + pl.program_id(1)
    n_tiles = pl.num_programs(0) * pl.num_programs(1)
    slot = t % 2

    @pl.when(t == 0)
    def _():
        _start_row_gather(slot_ref, 0, f_hbm, fbuf.at[0], sem.at[0])

    @pl.when(t + 1 < n_tiles)
    def _():
        _start_row_gather(slot_ref, (t + 1) * tile, f_hbm, fbuf.at[1 - slot], sem.at[1 - slot])

    _wait_row_gather(f_hbm, fbuf.at[slot], sem.at[slot])
    x2 = x_ref[...] + gate_ref[...] * fbuf[slot].reshape(bs, tl, D_MODEL)
    if final_norm:
        ms = jnp.mean(x2 * x2, axis=-1, keepdims=True)
        x2 = x2 * lax.rsqrt(ms + NORM_EPS) * fg_ref[...]
    o_ref[...] = x2


def _moe_combine(ffn_sorted, slot, x, gate, final_g, final_norm):
    seqs, length, _ = x.shape
    bs, tl = _seq_blocks(seqs, length)
    tok = pl.BlockSpec((bs, tl, D_MODEL), lambda s, i, slot: (s, i, 0))
    return pl.pallas_call(
        functools.partial(_moe_combine_kernel, final_norm=final_norm),
        out_shape=jax.ShapeDtypeStruct((seqs, length, D_MODEL), F32),
        grid_spec=pltpu.PrefetchScalarGridSpec(
            num_scalar_prefetch=1,
            grid=(seqs // bs, length // tl),
            in_specs=[pl.BlockSpec(memory_space=pl.ANY), tok,
                      pl.BlockSpec((bs, 1, D_MODEL), lambda s, i, slot: (s, 0, 0)),
                      pl.BlockSpec((1, 1, D_MODEL), lambda s, i, slot: (0, 0, 0))],
            out_specs=tok,
            scratch_shapes=[pltpu.VMEM((2, bs * tl, D_MODEL), F32), pltpu.SemaphoreType.DMA((2,))]),
        compiler_params=_cparams("arbitrary", "arbitrary"),
        name="moe_combine",
    )(slot, ffn_sorted, x, gate, final_g.reshape(1, 1, D_MODEL))


def _moe(h, group, gates, x, gate, final_g, wg_bf, wu_bf, wd_bf, final_norm):
    seqs, length, _ = x.shape
    src, slot, slot_gates, tile_group = _dispatch_plan(group, gates, MOE_TILE)
    ffn_sorted = _moe_experts(h.reshape(seqs * length, D_MODEL), src, slot_gates, tile_group,
                              wg_bf, wu_bf, wd_bf, MOE_TILE)
    return _moe_combine(ffn_sorted, slot, x, gate, final_g, final_norm)


def _head_selectors():
    r = lax.broadcasted_iota(jnp.int32, (D_MODEL, LANES), 0)
    c = lax.broadcasted_iota(jnp.int32, (D_MODEL, LANES), 1)
    sel = jnp.where(r // HEAD_DIM == c, 1.0, 0.0).astype(BF16)
    rt = lax.broadcasted_iota(jnp.int32, (LANES, D_MODEL), 0)
    ct = lax.broadcasted_iota(jnp.int32, (LANES, D_MODEL), 1)
    selt = jnp.where(ct // HEAD_DIM == rt, 1.0, 0.0).astype(BF16)
    return jnp.concatenate([sel, sel], axis=0), jnp.concatenate([selt, selt], axis=0)


def _head_sum(x, sel2, selt2):
    return _dot2(_dot2(x, sel2), selt2)


def _rwkv_pre_kernel(x_ref, prev_ref, shift_ref, g_ref, sc_ref, sh_ref, mu_ref,
                     wr_ref, wk_ref, wv_ref, w0_ref, w1_ref, w2_ref, a0_ref, a1_ref, a2_ref,
                     g1_ref, g2_ref, kk_ref, ka_ref,
                     r_ref, w_ref, k_ref, v_ref, na_ref, kb_ref, go_ref, so_ref):
    bs, tl, _ = x_ref.shape
    t = bs * tl
    i = pl.program_id(1)
    g, sc, sh = g_ref[...], sc_ref[...], sh_ref[...]
    h = _norm_mod(x_ref[...], g, sc, sh)
    so_ref[...] = h[:, tl - 1:tl, :]
    h_before = _norm_mod(prev_ref[:, 7:8, :], g, sc, sh)
    first = jnp.where(i == 0, shift_ref[...], h_before)
    row = lax.broadcasted_iota(jnp.int32, (bs, tl, D_MODEL), 1)
    prev = jnp.where(row == 0, first, pltpu.roll(h, 1, 1))
    h = h.reshape(t, D_MODEL)
    xx = prev.reshape(t, D_MODEL) - h
    mixed = [(h + xx * mu_ref[j:j + 1, :]).astype(BF16) for j in range(6)]
    xr, xw, xk, xv, xa, xg = mixed
    r = _dot(xr, wr_ref[...])
    k = _dot(xk, wk_ref[...])
    v = _dot(xv, wv_ref[...])
    wl = w0_ref[...] + _dot(jnp.tanh(_dot(xw, w1_ref[...])).astype(BF16), w2_ref[...])
    neg = -wl
    softplus = jnp.maximum(neg, 0.0) + jnp.log(1.0 + jnp.exp(-jnp.abs(neg)))
    w = -softplus - 0.5
    log_decay = -jnp.exp(w)
    a = _sigmoid(a0_ref[...] + _dot(_dot(xa, a1_ref[...]).astype(BF16), a2_ref[...]))
    gate = _dot(_sigmoid(_dot(xg, g1_ref[...])).astype(BF16), g2_ref[...])
    sel2, selt2 = _head_selectors()
    kk = k * kk_ref[...]
    norm = jnp.maximum(jnp.sqrt(_head_sum(kk * kk, sel2, selt2)), 1e-12)
    kk = kk / norm
    k = k * (1.0 + (a - 1.0) * ka_ref[...])
    shape = (bs, tl, D_MODEL)
    r_ref[...] = r.reshape(shape)
    w_ref[...] = log_decay.reshape(shape)
    k_ref[...] = k.reshape(shape)
    v_ref[...] = v.reshape(shape)
    na_ref[...] = (-kk).reshape(shape)
    kb_ref[...] = (kk * a).reshape(shape)
    go_ref[...] = gate.reshape(shape)


def _rwkv_pre(x, shift0, g, sc, sh, p):
    seqs, length, _ = x.shape
    bs, tl = _seq_blocks(seqs, length, RWKV_PRE_TILE)
    tok = pl.BlockSpec((bs, tl, D_MODEL), lambda s, i: (s, i, 0))
    prev = pl.BlockSpec((bs, 8, D_MODEL), lambda s, i: (s, jnp.maximum(i * (tl // 8) - 1, 0), 0))
    mod = pl.BlockSpec((bs, 1, D_MODEL), lambda s, i: (s, 0, 0))
    one = pl.BlockSpec((1, 1, D_MODEL), lambda s, i: (0, 0, 0))

    def full(a):
        return pl.BlockSpec(a.shape, lambda s, i: (0,) * a.ndim)

    vec = lambda a: a.reshape(1, D_MODEL)
    weights = [p['rw_mu'], p['rw_w_r'], p['rw_w_k'], p['rw_w_v'], vec(p['rw_w0']), p['rw_w1'],
               p['rw_w2'], vec(p['rw_a0']), p['rw_a1'], p['rw_a2'], p['rw_g1'], p['rw_g2'],
               vec(p['rw_k_k']), vec(p['rw_k_a'])]
    f32o = jax.ShapeDtypeStruct((seqs, length, D_MODEL), F32)
    outs = pl.pallas_call(
        _rwkv_pre_kernel,
        out_shape=(f32o,) * 7 + (jax.ShapeDtypeStruct((seqs, 1, D_MODEL), F32),),
        grid=(seqs // bs, length // tl),
        in_specs=[tok, prev, mod, one, mod, mod] + [full(a) for a in weights],
        out_specs=(tok,) * 7 + (mod,),
        compiler_params=_cparams("arbitrary", "arbitrary"),
        name="rwkv_pre",
    )(x, x, shift0, g.reshape(1, 1, D_MODEL), sc, sh, *weights)
    return outs


_TN = (((0,), (0,)), ((), ()))


def _bdot(a, b, dims=None):
    a, b = a.astype(BF16), b.astype(BF16)
    if dims is None:
        return jnp.dot(a, b, preferred_element_type=F32)
    return lax.dot_general(a, b, dims, preferred_element_type=F32)


def _scan_kernel(r_ref, lw_ref, k_ref, v_ref, a_ref, b_ref, s0_ref, y_ref, st_ref, z_ref, *, chunk):
    tc = r_ref.shape[1]
    t2 = 2 * chunk
    c = pl.program_id(1)
    lane_a = _pair_masks(chunk)
    fold = lambda x: jnp.where(lane_a, x[:chunk], x[chunk:])
    stack = lambda x: jnp.concatenate([x, x], axis=0)
    split_heads = lambda x: _stack_heads(x, axis=0)

    def block_diag(mats):
        n = mats[0].shape[0]
        zero = jnp.zeros((n, n), F32)
        return jnp.concatenate(
            [jnp.concatenate([m if j == i else zero for j in range(len(mats))], axis=1)
             for i, m in enumerate(mats)], axis=0)

    @pl.when(c == 0)
    def _():
        for q in range(PAIRS // SCAN_GROUP):
            z_ref[q] = block_diag([s0_ref[0, 2 * SCAN_GROUP * q + h] for h in range(2 * SCAN_GROUP)])

    row = lax.broadcasted_iota(jnp.int32, (t2, t2), 0)
    col = lax.broadcasted_iota(jnp.int32, (t2, t2), 1)
    same_head = (row // chunk) == (col // chunk)
    strict = jnp.logical_and(same_head, (row % chunk) > (col % chunk))
    incl = jnp.logical_and(same_head, (row % chunk) >= (col % chunk))
    tr = lax.broadcasted_iota(jnp.int32, (chunk, chunk), 0)
    tcol = lax.broadcasted_iota(jnp.int32, (chunk, chunk), 1)
    tri = jnp.where(tr >= tcol, 1.0, 0.0).astype(BF16)
    tri2 = jnp.concatenate([tri, tri], axis=1)
    gw = SCAN_GROUP * LANES
    zr = lax.broadcasted_iota(jnp.int32, (gw, gw), 0)
    zc = lax.broadcasted_iota(jnp.int32, (gw, gw), 1)
    z_diag = (zr // HEAD_DIM) == (zc // HEAD_DIM)
    n_doublings = chunk.bit_length() - 1

    def one_chunk(ci, _):
        rows = pl.ds(pl.multiple_of(ci * chunk, chunk), chunk)
        lw, r, k, v, a, b = [ref[0, rows, :] for ref in (lw_ref, r_ref, k_ref, v_ref, a_ref, b_ref)]
        hi, lo = _split2(lw)
        cum = _dot(tri2, jnp.concatenate([hi, lo], axis=0))
        grow = jnp.exp(-cum)
        decay = jnp.exp(cum)
        total = decay[chunk - 1:chunk, :]
        a_t = a * jnp.exp(cum - lw)
        r_t = r * decay
        b_t = b * grow
        k_t = k * grow
        b_end = b_t * total
        k_end = k_t * total
        for q in range(PAIRS // SCAN_GROUP):
            pairs = [SCAN_GROUP * q + j for j in range(SCAN_GROUP)]
            group = slice(q * gw, (q + 1) * gw)
            l_ab, l_ak, l_rb, l_rk = [], [], [], []
            for p in pairs:
                lanes = _pair_lanes(p)
                g = _bdot(jnp.concatenate([split_heads(a_t[:, lanes]), split_heads(r_t[:, lanes])], axis=0),
                          jnp.concatenate([stack(b_t[:, lanes]), stack(k_t[:, lanes])], axis=0),
                          _NT)
                l_ab.append(jnp.where(strict, g[:t2, :t2], 0.0))
                l_ak.append(jnp.where(strict, g[:t2, t2:], 0.0))
                l_rb.append(jnp.where(incl, g[t2:, :t2], 0.0))
                l_rk.append(jnp.where(incl, g[t2:, t2:], 0.0))
            z = z_ref[q]
            applied = _bdot(jnp.concatenate([a_t[:, group], r_t[:, group]], axis=0), z, _NT)
            per_pair = lambda x: [x[:, j * LANES:(j + 1) * LANES] for j in range(SCAN_GROUP)]
            vv = jnp.concatenate([stack(v[:, _pair_lanes(p)]) for p in pairs], axis=0)
            x = (jnp.concatenate([stack(m) for m in per_pair(applied[:chunk])], axis=0)
                 + _bdot(block_diag(l_ak), vv))
            power = block_diag(l_ab)
            for i in range(n_doublings):
                x = x + _bdot(power, x)
                if i + 1 < n_doublings:
                    power = _bdot(power, power)
            u = [fold(x[j * t2:(j + 1) * t2]) for j in range(SCAN_GROUP)]
            uu = jnp.concatenate([stack(m) for m in u], axis=0)
            ys = _bdot(jnp.concatenate([block_diag(l_rb), block_diag(l_rk)], axis=1),
                       jnp.concatenate([uu, vv], axis=0))
            y = applied[chunk:] + jnp.concatenate(
                [fold(ys[j * t2:(j + 1) * t2]) for j in range(SCAN_GROUP)], axis=1)
            y_ref[0, rows, group] = y
            grown = _bdot(jnp.concatenate([jnp.concatenate(u, axis=1), v[:, group]], axis=0),
                          jnp.concatenate([b_end[:, group], k_end[:, group]], axis=0), _TN)
            z_ref[q] = z * total[:, group] + jnp.where(z_diag, grown, 0.0)
        return 0

    lax.fori_loop(0, tc // chunk, one_chunk, 0)

    @pl.when(c == pl.num_programs(1) - 1)
    def _():
        for q in range(PAIRS // SCAN_GROUP):
            z = z_ref[q]
            for h in range(2 * SCAN_GROUP):
                block = slice(h * HEAD_DIM, (h + 1) * HEAD_DIM)
                st_ref[0, 2 * SCAN_GROUP * q + h] = z[block, block]


def _scan(r, lw, k, v, na, kb, s0):
    seqs, length, _ = r.shape
    chunk = min(length, SCAN_CHUNK)
    tc = min(length, SCAN_BLOCK)
    assert length % tc == 0 and tc % chunk == 0 and chunk & (chunk - 1) == 0
    tok = pl.BlockSpec((1, tc, D_MODEL), lambda s, c: (s, c, 0))
    st = pl.BlockSpec((1, HEADS, HEAD_DIM, HEAD_DIM), lambda s, c: (s, 0, 0, 0))
    return pl.pallas_call(
        functools.partial(_scan_kernel, chunk=chunk),
        out_shape=(jax.ShapeDtypeStruct((seqs, length, D_MODEL), F32),
                   jax.ShapeDtypeStruct((seqs, HEADS, HEAD_DIM, HEAD_DIM), F32)),
        grid=(seqs, length // tc),
        in_specs=[tok] * 6 + [st],
        out_specs=(tok, st),
        scratch_shapes=[pltpu.VMEM((PAIRS // SCAN_GROUP, SCAN_GROUP * LANES, SCAN_GROUP * LANES), F32)],
        compiler_params=_cparams("arbitrary", "arbitrary"),
        name="rwkv_scan",
    )(r, lw, k, v, na, kb, s0)


def _rwkv_post_kernel(y_ref, r_ref, k_ref, v_ref, g_ref, lg_ref, lb_ref, rk_ref, o_ref):
    bs, tl, _ = y_ref.shape
    t = bs * tl
    sel2, selt2 = _head_selectors()
    y = y_ref[...].reshape(t, D_MODEL)
    mu = _head_sum(y, sel2, selt2) * (1.0 / HEAD_DIM)
    d = y - mu
    var = _head_sum(d * d, sel2, selt2) * (1.0 / HEAD_DIM)
    yn = d * lax.rsqrt(var + LNX_EPS) * lg_ref[...] + lb_ref[...]
    rk = r_ref[...].reshape(t, D_MODEL) * k_ref[...].reshape(t, D_MODEL) * rk_ref[...]
    bonus = _head_sum(rk, sel2, selt2) * v_ref[...].reshape(t, D_MODEL)
    o_ref[...] = ((yn + bonus) * g_ref[...].reshape(t, D_MODEL)).reshape(bs, tl, D_MODEL)


def _rwkv_post(y, r, k, v, gate, lnx_g, lnx_b, r_k):
    seqs, length, _ = y.shape
    bs, tl = _seq_blocks(seqs, length)
    tok = pl.BlockSpec((bs, tl, D_MODEL), lambda s, i: (s, i, 0))
    vec = pl.BlockSpec((1, D_MODEL), lambda s, i: (0, 0))
    return pl.pallas_call(
        _rwkv_post_kernel,
        out_shape=jax.ShapeDtypeStruct((seqs, length, D_MODEL), F32),
        grid=(seqs // bs, length // tl),
        in_specs=[tok] * 5 + [vec] * 3,
        out_specs=tok,
        compiler_params=_cparams("arbitrary", "arbitrary"),
        name="rwkv_post",
    )(y, r, k, v, gate, lnx_g.reshape(1, D_MODEL), lnx_b.reshape(1, D_MODEL),
      r_k.reshape(1, D_MODEL))


def _trunk(x, mod, p, wb, k_cache, v_cache, shift0, s0):
    seqs, length, _ = x.shape
    part = lambda layer, j: mod[layer, :, j * D_MODEL:(j + 1) * D_MODEL].reshape(seqs, 1, D_MODEL)
    router_wt = p['router_w'].T
    sh1, sc1, g1, sh2, sc2, g2 = [part(0, j) for j in range(6)]
    k, v, qh, ql, kh, kl, vh, vl = _qkv(x, p['norm_mix_g'][0], sc1, sh1, *wb['sb_w_qkv'])
    if k_cache is None:
        attn = _attn_prompt((qh, ql), (kh, kl), (vh, vl))
    else:
        past = k_cache.shape[1]
        attn = _attn_sample((qh, ql), (kh, kl), (vh, vl), k_cache.reshape(seqs, past, D_MODEL),
                            v_cache.reshape(seqs, past, D_MODEL))
    x, h, gates, group = _oproj_router(attn, x, g1, wb['sb_w_o'], p['norm_ffn_g'][0], sc2, sh2,
                                       router_wt, p['router_bias'])
    x = _moe(h, group, gates, x, g2, p['final_norm_g'], wb['exp_w_gate'][0], wb['exp_w_up'][0],
             wb['exp_w_down'][0], final_norm=False)
    sb_k = k.reshape(seqs, length, HEADS, HEAD_DIM)
    sb_v = v.reshape(seqs, length, HEADS, HEAD_DIM)
    sh1, sc1, g1, sh2, sc2, g2 = [part(1, j) for j in range(6)]
    r, w, kq, vv, na, kb, gate, shift = _rwkv_pre(x, shift0, p['norm_mix_g'][1], sc1, sh1, wb)
    y, state = _scan(r, w, kq, vv, na, kb, s0)
    m = _rwkv_post(y, r, kq, vv, gate, p['rw_lnx_g'], p['rw_lnx_b'], p['rw_r_k'])
    x, h, gates, group = _oproj_router(m, x, g1, wb['rw_w_o'], p['norm_ffn_g'][1], sc2, sh2,
                                       router_wt, p['router_bias'])
    y_out = _moe(h, group, gates, x, g2, p['final_norm_g'], wb['exp_w_gate'][1], wb['exp_w_up'][1],
                 wb['exp_w_down'][1], final_norm=True)
    return y_out, sb_k, sb_v, state, shift


def kernel(x_prompt, x_sample, cache_sb_k, cache_sb_v, state_rw_wkv, state_rw_shift,
           c_prompt, c_sample, ada_w, ada_b, norm_mix_g, norm_ffn_g, final_norm_g,
           sb_w_qkv, sb_w_o, rw_mu, rw_w_r, rw_w_k, rw_w_v, rw_w0, rw_w1, rw_w2,
           rw_a0, rw_a1, rw_a2, rw_g1, rw_g2, rw_k_k, rw_k_a, rw_r_k, rw_lnx_g, rw_lnx_b,
           rw_w_o, router_w, router_bias, exp_w_gate, exp_w_up, exp_w_down):
    p = {
        'norm_mix_g': norm_mix_g, 'norm_ffn_g': norm_ffn_g, 'final_norm_g': final_norm_g,
        'rw_r_k': rw_r_k, 'rw_lnx_g': rw_lnx_g, 'rw_lnx_b': rw_lnx_b,
        'router_w': router_w, 'router_bias': router_bias,
    }
    bf = lambda a: a.astype(BF16)
    wb = {
        'sb_w_qkv': _split_weight(sb_w_qkv), 'sb_w_o': _split_weight(sb_w_o),
        'rw_w_o': _split_weight(rw_w_o),
        'rw_mu': rw_mu, 'rw_w_r': bf(rw_w_r), 'rw_w_k': bf(rw_w_k), 'rw_w_v': bf(rw_w_v),
        'rw_w0': rw_w0, 'rw_w1': bf(rw_w1), 'rw_w2': bf(rw_w2), 'rw_a0': rw_a0,
        'rw_a1': bf(rw_a1), 'rw_a2': bf(rw_a2), 'rw_g1': bf(rw_g1), 'rw_g2': bf(rw_g2),
        'rw_k_k': rw_k_k, 'rw_k_a': rw_k_a,
        'exp_w_gate': bf(exp_w_gate), 'exp_w_up': bf(exp_w_up), 'exp_w_down': bf(exp_w_down),
    }
    n_prompt = x_prompt.shape[0]
    mod = _ada(jnp.concatenate([c_prompt, c_sample], axis=0), ada_w, ada_b)
    shift0 = jnp.zeros((n_prompt, 1, D_MODEL), x_prompt.dtype)
    s0 = jnp.zeros((n_prompt, HEADS, HEAD_DIM, HEAD_DIM), F32)
    y_p, k_p, v_p, st_p, sh_p = _trunk(x_prompt, mod[:, :n_prompt], p, wb, None, None, shift0, s0)
    y_s, k_s, v_s, st_s, sh_s = _trunk(x_sample, mod[:, n_prompt:], p, wb, cache_sb_k, cache_sb_v,
                                       state_rw_shift, state_rw_wkv)
    return (y_p, y_s, k_p, v_p, k_s, v_s, st_p, sh_p, st_s, sh_s)
```

```python
import functools

import jax
import jax.numpy as jnp
from jax import lax
from jax.experimental import pallas as pl
from jax.experimental.pallas import tpu as pltpu

F32 = jnp.float32
BF16 = jnp.bfloat16

D_MODEL = 1024
HEADS = 16
HEAD_DIM = 64
LANES = 128
SUBLANES = 8
PAIRS = D_MODEL // LANES
N_EXPERTS = 16
N_GROUPS = 4
EXPERTS_PER_GROUP = 4
D_EXPERT = 512
NORM_EPS = 1e-6
LNX_EPS = 64e-5
ZERO_WEIGHT_CARRY = 104.0
TOKEN_TILE = 512
RWKV_PRE_TILE = 256
QKV_TILE = 256
SCAN_CHUNK = 64
SCAN_BLOCK = 256
SCAN_GROUP = 2
MOE_TILE = 512
VMEM_LIMIT = 56 * 1024 * 1024

_NT = (((1,), (1,)), ((), ()))
_BATCHED = (((2,), (1,)), ((0,), (0,)))
_BATCHED_NT = (((2,), (2,)), ((0,), (0,)))
_BATCHED_TN = (((1,), (1,)), ((0,), (0,)))


def _cparams(*sem):
    return pltpu.CompilerParams(dimension_semantics=sem, vmem_limit_bytes=VMEM_LIMIT)


def _sigmoid(x):
    return 1.0 / (1.0 + jnp.exp(-x))


def _split2(x):
    hi = x.astype(BF16)
    lo = (x - hi.astype(F32)).astype(BF16)
    return hi, lo


def _dot(a, b):
    return jnp.dot(a, b, preferred_element_type=F32)


def _dot2(x, w2):
    hi, lo = _split2(x)
    return _dot(jnp.concatenate([hi, lo], axis=1), w2)


def _norm_mod(x, g, sc, sh):
    ms = jnp.mean(x * x, axis=-1, keepdims=True)
    y = x * lax.rsqrt(ms + NORM_EPS) * g
    return y * (1.0 + sc) + sh


def _seq_blocks(seqs, length, tile=TOKEN_TILE):
    if length >= tile:
        assert length % tile == 0
        return 1, tile
    assert tile % length == 0 and seqs % (tile // length) == 0 and length % 16 == 0
    return tile // length, length


def _ada_kernel(c_ref, w_ref, b_ref, o_ref):
    c = c_ref[...]
    s = c * _sigmoid(c)
    o_ref[0] = jnp.dot(s, w_ref[0], precision=lax.Precision.HIGHEST,
                       preferred_element_type=F32) + b_ref[0]


def _ada(c, ada_w, ada_b):
    rows = c.shape[0]
    depth = ada_w.shape[0]
    return pl.pallas_call(
        _ada_kernel,
        out_shape=jax.ShapeDtypeStruct((depth, rows, 6 * D_MODEL), F32),
        grid=(depth, 6),
        in_specs=[pl.BlockSpec((rows, D_MODEL), lambda l, j: (0, 0)),
                  pl.BlockSpec((1, D_MODEL, D_MODEL), lambda l, j: (l, 0, j)),
                  pl.BlockSpec((1, 1, D_MODEL), lambda l, j: (l, 0, j))],
        out_specs=pl.BlockSpec((1, rows, D_MODEL), lambda l, j: (l, 0, j)),
        compiler_params=_cparams("arbitrary", "arbitrary"),
        name="ada_mod",
    )(c, ada_w, ada_b.reshape(depth, 1, 6 * D_MODEL))


def _dot3(x, w_hi, w_lo):
    hi, lo = _split2(x)
    rows = x.shape[0]
    both = _dot(jnp.concatenate([hi, lo], axis=0), w_hi)
    return both[:rows] + both[rows:] + _dot(hi, w_lo)


def _split_weight(w):
    hi = w.astype(BF16)
    return hi, (w - hi.astype(F32)).astype(BF16)


def _qkv_kernel(x_ref, g_ref, sc_ref, sh_ref, wh_ref, wl_ref, k_ref, v_ref, *piece_refs):
    bs, tl, _ = x_ref.shape
    h = _norm_mod(x_ref[...], g_ref[...], sc_ref[...], sh_ref[...]).reshape(bs * tl, D_MODEL)
    qkv = _dot3(h, wh_ref[...], wl_ref[...])
    q = qkv[:, :D_MODEL] * (HEAD_DIM ** -0.5)
    k = qkv[:, D_MODEL:2 * D_MODEL]
    v = qkv[:, 2 * D_MODEL:]
    k_ref[...] = k.reshape(bs, tl, D_MODEL)
    v_ref[...] = v.reshape(bs, tl, D_MODEL)
    pieces = _split2(q) + _split2(k) + _split2(v)
    for ref, piece in zip(piece_refs, pieces):
        ref[...] = piece.reshape(bs, tl, D_MODEL)


def _qkv(x, g, sc, sh, w_hi, w_lo):
    seqs, length, _ = x.shape
    bs, tl = _seq_blocks(seqs, length, QKV_TILE)
    tok = pl.BlockSpec((bs, tl, D_MODEL), lambda s, i: (s, i, 0))
    mod = pl.BlockSpec((bs, 1, D_MODEL), lambda s, i: (s, 0, 0))
    wspec = pl.BlockSpec((D_MODEL, 3 * D_MODEL), lambda s, i: (0, 0))
    f32o = jax.ShapeDtypeStruct((seqs, length, D_MODEL), F32)
    bfo = jax.ShapeDtypeStruct((seqs, length, D_MODEL), BF16)
    return pl.pallas_call(
        _qkv_kernel,
        out_shape=(f32o, f32o) + (bfo,) * 6,
        grid=(seqs // bs, length // tl),
        in_specs=[tok, pl.BlockSpec((1, 1, D_MODEL), lambda s, i: (0, 0, 0)), mod, mod, wspec, wspec],
        out_specs=(tok,) * 8,
        compiler_params=_cparams("arbitrary", "arbitrary"),
        name="sb_qkv",
    )(x, g.reshape(1, 1, D_MODEL), sc, sh, w_hi, w_lo)


def _pair_masks(rows):
    lane = lax.broadcasted_iota(jnp.int32, (rows, LANES), 1)
    return lane < HEAD_DIM


def _suffix_matrix():
    r = lax.broadcasted_iota(jnp.int32, (LANES, LANES), 0)
    c = lax.broadcasted_iota(jnp.int32, (LANES, LANES), 1)
    u = jnp.where(r > c, 1.0, 0.0).astype(BF16)
    return jnp.concatenate([u, u], axis=0)


def _query_pieces(q_hi, q_lo):
    hi, lo = _stack_heads(q_hi), _stack_heads(q_lo)
    return jnp.concatenate([hi, lo, hi], axis=1)


def _key_pieces(k_hi, k_lo):
    return jnp.concatenate([k_hi, k_hi, k_lo], axis=1)


def _sb_sweep(q3, keys, values, state, u2, visible):
    n = len(q3)
    rows = q3[0].shape[0]
    tq = rows // 2
    wide = lambda x: jnp.concatenate([x[:tq], x[tq:]], axis=1)
    z_all = lax.dot_general(jnp.stack(q3), jnp.stack([_key_pieces(*keys[p]) for p in range(n)]),
                            _BATCHED_NT, preferred_element_type=F32)
    zs, sps = [], []
    for p in range(n):
        z = z_all[p]
        sp = jnp.maximum(z, 0.0) + jnp.log(1.0 + jnp.exp(-jnp.abs(z)))
        if visible is not None:
            sp = jnp.where(visible, sp, 0.0)
        zs.append(z)
        sps.append(sp)
    later_all = _dot2(jnp.concatenate(sps, axis=0), u2)
    weights, vals = [], []
    for p in range(n):
        later = later_all[p * rows:(p + 1) * rows]
        w = jnp.exp(zs[p] - sps[p] - later - state[p][0])
        if visible is not None:
            w = jnp.where(visible, w, 0.0)
        w_hi, w_lo = _split2(w)
        v_hi, v_lo = _stack_heads(values[p][0]), _stack_heads(values[p][1])
        weights.append(jnp.concatenate([wide(w_hi), wide(w_lo), wide(w_hi)], axis=1))
        vals.append(jnp.concatenate([v_hi, v_hi, v_lo], axis=0))
    gained = lax.dot_general(jnp.stack(weights), jnp.stack(vals), _BATCHED,
                             preferred_element_type=F32)
    return tuple((state[p][0] + jnp.sum(sps[p], axis=-1, keepdims=True), state[p][1] + gained[p])
                 for p in range(n))


def _stack_heads(x, axis=0):
    mask_a = _pair_masks(x.shape[0])
    return jnp.concatenate([jnp.where(mask_a, x, 0), jnp.where(mask_a, 0, x)], axis=axis)


def _diag_visible(tq):
    row = lax.broadcasted_iota(jnp.int32, (2 * tq, LANES), 0)
    col = lax.broadcasted_iota(jnp.int32, (2 * tq, LANES), 1)
    qpos = jnp.where(row >= tq, row - tq, row)
    return col < qpos


def _pair_lanes(p):
    return slice(p * LANES, (p + 1) * LANES)


def _sweep_earlier_tiles(n_tiles, sweep, state):
    def cond(c):
        return jnp.logical_and(c[0] < n_tiles, jnp.logical_not(c[1]))

    def body(c):
        st = sweep(n_tiles - 1 - c[0], c[2])
        low = st[0][0]
        for p in range(1, len(st)):
            low = jnp.minimum(low, st[p][0])
        return c[0] + 1, jnp.min(low) > ZERO_WEIGHT_CARRY, st

    return lax.while_loop(cond, body, (jnp.int32(0), False, state))[2]


def _attn_prompt_kernel(qh_ref, ql_ref, kh_ref, kl_ref, vh_ref, vl_ref, o_ref):
    tq = qh_ref.shape[1]
    qi = pl.program_id(1)
    u2 = _suffix_matrix()
    q3 = [_query_pieces(qh_ref[0, :, _pair_lanes(p)], ql_ref[0, :, _pair_lanes(p)])
          for p in range(PAIRS)]

    def sweep(start, state, visible):
        rows = pl.ds(pl.multiple_of(start, tq), tq)
        tile = lambda ref, p: ref[0, rows, _pair_lanes(p)]
        return _sb_sweep(q3, [(tile(kh_ref, p), tile(kl_ref, p)) for p in range(PAIRS)],
                         [(tile(vh_ref, p), tile(vl_ref, p)) for p in range(PAIRS)],
                         state, u2, visible)

    zero = (jnp.zeros((2 * tq, LANES), F32), jnp.zeros((tq, LANES), F32))
    state = sweep(qi * tq, (zero,) * PAIRS, _diag_visible(tq))
    state = _sweep_earlier_tiles(qi, lambda j, st: sweep(j * tq, st, None), state)
    for p in range(PAIRS):
        o_ref[0, :, _pair_lanes(p)] = state[p][1]


def _attn_prompt(q, k, v):
    batch, length, _ = q[0].shape
    tq = LANES
    qspec = pl.BlockSpec((1, tq, D_MODEL), lambda b, i: (b, i, 0))
    kspec = pl.BlockSpec((1, length, D_MODEL), lambda b, i: (b, 0, 0), pipeline_mode=pl.Buffered(1))
    return pl.pallas_call(
        _attn_prompt_kernel,
        out_shape=jax.ShapeDtypeStruct((batch, length, D_MODEL), F32),
        grid=(batch, length // tq),
        in_specs=[qspec, qspec, kspec, kspec, kspec, kspec],
        out_specs=qspec,
        compiler_params=_cparams("arbitrary", "arbitrary"),
        name="sb_attn_prompt",
    )(*q, *k, *v)


def _attn_sample_kernel(qh_ref, ql_ref, kh_ref, kl_ref, vh_ref, vl_ref, kc_ref, vc_ref, o_ref):
    tq = qh_ref.shape[1]
    past = kc_ref.shape[1]
    u2 = _suffix_matrix()
    pad = jnp.zeros((LANES - tq, LANES), BF16)
    padded = lambda ref, p: jnp.concatenate([ref[0, :, _pair_lanes(p)], pad], axis=0)
    q3 = [_query_pieces(qh_ref[0, :, _pair_lanes(p)], ql_ref[0, :, _pair_lanes(p)])
          for p in range(PAIRS)]
    zero = (jnp.zeros((2 * tq, LANES), F32), jnp.zeros((tq, LANES), F32))
    state = _sb_sweep(q3, [(padded(kh_ref, p), padded(kl_ref, p)) for p in range(PAIRS)],
                      [(padded(vh_ref, p), padded(vl_ref, p)) for p in range(PAIRS)],
                      (zero,) * PAIRS, u2, _diag_visible(tq))

    def sweep(j, st):
        rows = pl.ds(pl.multiple_of(j * LANES, LANES), LANES)
        return _sb_sweep(q3, [_split2(kc_ref[0, rows, _pair_lanes(p)]) for p in range(PAIRS)],
                         [_split2(vc_ref[0, rows, _pair_lanes(p)]) for p in range(PAIRS)],
                         st, u2, None)

    state = _sweep_earlier_tiles(past // LANES, sweep, state)
    for p in range(PAIRS):
        o_ref[0, :, _pair_lanes(p)] = state[p][1]


def _attn_sample(q, k, v, k_cache, v_cache):
    batch, length, _ = q[0].shape
    past = k_cache.shape[1]
    assert length <= LANES and past % LANES == 0
    qspec = pl.BlockSpec((1, length, D_MODEL), lambda b: (b, 0, 0))
    cspec = pl.BlockSpec((1, past, D_MODEL), lambda b: (b, 0, 0))
    return pl.pallas_call(
        _attn_sample_kernel,
        out_shape=jax.ShapeDtypeStruct((batch, length, D_MODEL), F32),
        grid=(batch,),
        in_specs=[qspec] * 6 + [cspec, cspec],
        out_specs=qspec,
        compiler_params=_cparams("arbitrary"),
        name="sb_attn_sample",
    )(*q, *k, *v, k_cache, v_cache)


def _row_max(rows):
    m = rows[0]
    for r in rows[1:]:
        m = jnp.maximum(m, r)
    return m


def _first_hit(rows, m):
    hits = []
    taken = None
    for r in rows:
        eq = r == m
        hit = eq if taken is None else jnp.logical_and(eq, jnp.logical_not(taken))
        taken = eq if taken is None else jnp.logical_or(taken, eq)
        hits.append(hit)
    return hits


def _route(s_rows, sel_rows):
    neg = jnp.float32(-jnp.inf)
    top1, top2, score = [], [], []
    for g in range(N_GROUPS):
        rows = sel_rows[g * EXPERTS_PER_GROUP:(g + 1) * EXPERTS_PER_GROUP]
        m1 = _row_max(rows)
        h1 = _first_hit(rows, m1)
        rest = [jnp.where(h, neg, r) for h, r in zip(h1, rows)]
        m2 = _row_max(rest)
        h2 = _first_hit(rest, m2)
        top1.append(h1)
        top2.append(h2)
        score.append(m1 + m2)
    best = _row_max(score)
    ghit = _first_hit(score, best)
    picked = []
    for g in range(N_GROUPS):
        for e in range(EXPERTS_PER_GROUP):
            picked.append(jnp.logical_and(ghit[g], jnp.logical_or(top1[g][e], top2[g][e])))
    total = None
    for p, s in zip(picked, s_rows):
        term = jnp.where(p, s, 0.0)
        total = term if total is None else total + term
    gates = [jnp.where(p, s / total, 0.0) for p, s in zip(picked, s_rows)]
    in_group = []
    for e in range(EXPERTS_PER_GROUP):
        row = gates[e]
        for g in range(1, N_GROUPS):
            row = row + gates[g * EXPERTS_PER_GROUP + e]
        in_group.append(row)
    group = jnp.zeros_like(best)
    for g in range(1, N_GROUPS):
        group = jnp.where(ghit[g], float(g), group)
    return in_group, group


def _oproj_kernel(m_ref, x_ref, gate_ref, wh_ref, wl_ref, g_ref, sc_ref, sh_ref, rw_ref, rb_ref,
                  x1_ref, h_ref, route_ref):
    bs, tl, _ = x_ref.shape
    m = m_ref[...].reshape(bs * tl, D_MODEL)
    mix = _dot3(m, wh_ref[...], wl_ref[...]).reshape(bs, tl, D_MODEL)
    x1 = x_ref[...] + gate_ref[...] * mix
    x1_ref[...] = x1
    h = _norm_mod(x1, g_ref[...], sc_ref[...], sh_ref[...])
    h_ref[...] = h
    logits = lax.dot_general(rw_ref[...], h.reshape(bs * tl, D_MODEL), _NT,
                             precision=lax.Precision.HIGHEST,
                             preferred_element_type=F32)
    s = _sigmoid(logits)
    sel = s + rb_ref[...]
    s_rows = [s[e:e + 1, :] for e in range(N_EXPERTS)]
    sel_rows = [sel[e:e + 1, :] for e in range(N_EXPERTS)]
    in_group, group = _route(s_rows, sel_rows)
    pad = [jnp.zeros_like(group)] * (SUBLANES - EXPERTS_PER_GROUP - 1)
    route_ref[0] = jnp.concatenate(in_group + [group] + pad, axis=0)


def _oproj_router(m, x, gate, w_pieces, g, sc, sh, router_wt, router_b):
    seqs, length, _ = x.shape
    bs, tl = _seq_blocks(seqs, length)
    n_tiles = (seqs // bs) * (length // tl)
    per_seq = length // tl
    tok = pl.BlockSpec((bs, tl, D_MODEL), lambda s, i: (s, i, 0))
    mod = pl.BlockSpec((bs, 1, D_MODEL), lambda s, i: (s, 0, 0))
    one = pl.BlockSpec((1, 1, D_MODEL), lambda s, i: (0, 0, 0))
    wspec = pl.BlockSpec((D_MODEL, D_MODEL), lambda s, i: (0, 0))
    f32o = jax.ShapeDtypeStruct((seqs, length, D_MODEL), F32)
    x1, h, route = pl.pallas_call(
        _oproj_kernel,
        out_shape=(f32o, f32o, jax.ShapeDtypeStruct((n_tiles, SUBLANES, bs * tl), F32)),
        grid=(seqs // bs, per_seq),
        in_specs=[tok, tok, mod, wspec, wspec, one, mod, mod,
                  pl.BlockSpec((N_EXPERTS, D_MODEL), lambda s, i: (0, 0)),
                  pl.BlockSpec((N_EXPERTS, 1), lambda s, i: (0, 0))],
        out_specs=(tok, tok,
                   pl.BlockSpec((1, SUBLANES, bs * tl), lambda s, i: (s * per_seq + i, 0, 0))),
        compiler_params=_cparams("arbitrary", "arbitrary"),
        name="oproj_router",
    )(m, x, gate, *w_pieces, g.reshape(1, 1, D_MODEL), sc, sh, router_wt,
      router_b.reshape(N_EXPERTS, 1))
    route = jnp.swapaxes(route, 1, 2).reshape(seqs * length, SUBLANES)
    return x1, h, route[:, :EXPERTS_PER_GROUP], route[:, EXPERTS_PER_GROUP].astype(jnp.int32)


def _dispatch_plan(group, gates, tile):
    n = group.shape[0]
    n_slots = n + N_GROUPS * tile
    onehot = (group[:, None] == jnp.arange(N_GROUPS)[None, :]).astype(jnp.int32)
    counts = jnp.sum(onehot, axis=0)
    padded = (counts + tile - 1) // tile * tile
    ends = jnp.cumsum(padded)
    starts = ends - padded
    rank = jnp.take_along_axis(jnp.cumsum(onehot, axis=0) - onehot, group[:, None], axis=1)[:, 0]
    slot = starts[group] + rank
    src = jnp.zeros((n_slots,), jnp.int32).at[slot].set(jnp.arange(n, dtype=jnp.int32))
    slot_gates = jnp.zeros((n_slots, EXPERTS_PER_GROUP), F32).at[slot].set(gates)
    tile_start = jnp.arange(n_slots // tile, dtype=jnp.int32) * tile
    tile_group = jnp.minimum(jnp.sum(tile_start[:, None] >= ends[None, :], axis=1), N_GROUPS - 1)
    return src, slot.astype(jnp.int32), slot_gates, tile_group.astype(jnp.int32)


def _start_row_gather(idx_ref, base, src_hbm, dst_ref, sem):
    def body(r, _):
        row = idx_ref[base + r]
        pltpu.make_async_copy(src_hbm.at[pl.ds(row, 1), :], dst_ref.at[pl.ds(r, 1), :], sem).start()
        return 0

    lax.fori_loop(0, dst_ref.shape[0], body, 0, unroll=8)


def _wait_row_gather(src_hbm, dst_ref, sem):
    pltpu.make_async_copy(src_hbm.at[pl.ds(0, dst_ref.shape[0]), :], dst_ref, sem).wait()


def _moe_kernel(src_ref, tg_ref, h_hbm, gates_ref, wg_ref, wu_ref, wd_ref, o_ref, hbuf, sem):
    t = pl.program_id(0)
    tile = o_ref.shape[0]
    slot = t % 2

    @pl.when(t == 0)
    def _():
        _start_row_gather(src_ref, 0, h_hbm, hbuf.at[0], sem.at[0])

    @pl.when(t + 1 < pl.num_programs(0))
    def _():
        _start_row_gather(src_ref, (t + 1) * tile, h_hbm, hbuf.at[1 - slot], sem.at[1 - slot])

    _wait_row_gather(h_hbm, hbuf.at[slot], sem.at[slot])
    h = hbuf[slot].astype(BF16)
    gates = gates_ref[...]
    acc = jnp.zeros((tile, D_MODEL), F32)
    for e in range(EXPERTS_PER_GROUP):
        a = _dot(h, wg_ref[e])
        u = _dot(h, wu_ref[e])
        he = (a * _sigmoid(a) * u).astype(BF16)
        acc = acc + gates[:, e:e + 1] * _dot(he, wd_ref[e])
    o_ref[...] = acc


def _moe_experts(h, src, slot_gates, tile_group, wg_bf, wu_bf, wd_bf, tile):
    n_slots = src.shape[0]
    group_w = lambda shape: pl.BlockSpec((EXPERTS_PER_GROUP,) + shape, lambda t, src, tg: (tg[t], 0, 0))
    return pl.pallas_call(
        _moe_kernel,
        out_shape=jax.ShapeDtypeStruct((n_slots, D_MODEL), F32),
        grid_spec=pltpu.PrefetchScalarGridSpec(
            num_scalar_prefetch=2,
            grid=(n_slots // tile,),
            in_specs=[pl.BlockSpec(memory_space=pl.ANY),
                      pl.BlockSpec((tile, EXPERTS_PER_GROUP), lambda t, src, tg: (t, 0)),
                      group_w((D_MODEL, D_EXPERT)), group_w((D_MODEL, D_EXPERT)),
                      group_w((D_EXPERT, D_MODEL))],
            out_specs=pl.BlockSpec((tile, D_MODEL), lambda t, src, tg: (t, 0)),
            scratch_shapes=[pltpu.VMEM((2, tile, D_MODEL), F32), pltpu.SemaphoreType.DMA((2,))]),
        compiler_params=_cparams("arbitrary"),
        name="moe_experts",
    )(src, tile_group, h, slot_gates, wg_bf, wu_bf, wd_bf)


def _moe_combine_kernel(slot_ref, f_hbm, x_ref, gate_ref, fg_ref, o_ref, fbuf, sem, *, final_norm):
    bs, tl, _ = x_ref.shape
    tile = bs * tl
    t = pl.program_id(0) * pl.num_programs(1) + pl.program_id(1)
    n_tiles = pl.num_programs(0) * pl.num_programs(1)
    slot = t % 2

    @pl.when(t == 0)
    def _():
        _start_row_gather(slot_ref, 0, f_hbm, fbuf.at[0], sem.at[0])

    @pl.when(t + 1 < n_tiles)
    def _():
        _start_row_gather(slot_ref, (t + 1) * tile, f_hbm, fbuf.at[1 - slot], sem.at[1 - slot])

    _wait_row_gather(f_hbm, fbuf.at[slot], sem.at[slot])
    x2 = x_ref[...] + gate_ref[...] * fbuf[slot].reshape(bs, tl, D_MODEL)
    if final_norm:
        ms = jnp.mean(x2 * x2, axis=-1, keepdims=True)
        x2 = x2 * lax.rsqrt(ms + NORM_EPS) * fg_ref[...]
    o_ref[...] = x2


def _moe_combine(ffn_sorted, slot, x, gate, final_g, final_norm):
    seqs, length, _ = x.shape
    bs, tl = _seq_blocks(seqs, length)
    tok = pl.BlockSpec((bs, tl, D_MODEL), lambda s, i, slot: (s, i, 0))
    return pl.pallas_call(
        functools.partial(_moe_combine_kernel, final_norm=final_norm),
        out_shape=jax.ShapeDtypeStruct((seqs, length, D_MODEL), F32),
        grid_spec=pltpu.PrefetchScalarGridSpec(
            num_scalar_prefetch=1,
            grid=(seqs // bs, length // tl),
            in_specs=[pl.BlockSpec(memory_space=pl.ANY), tok,
                      pl.BlockSpec((bs, 1, D_MODEL), lambda s, i, slot: (s, 0, 0)),
                      pl.BlockSpec((1, 1, D_MODEL), lambda s, i, slot: (0, 0, 0))],
            out_specs=tok,
            scratch_shapes=[pltpu.VMEM((2, bs * tl, D_MODEL), F32), pltpu.SemaphoreType.DMA((2,))]),
        compiler_params=_cparams("arbitrary", "arbitrary"),
        name="moe_combine",
    )(slot, ffn_sorted, x, gate, final_g.reshape(1, 1, D_MODEL))


def _moe(h, group, gates, x, gate, final_g, wg_bf, wu_bf, wd_bf, final_norm):
    seqs, length, _ = x.shape
    src, slot, slot_gates, tile_group = _dispatch_plan(group, gates, MOE_TILE)
    ffn_sorted = _moe_experts(h.reshape(seqs * length, D_MODEL), src, slot_gates, tile_group,
                              wg_bf, wu_bf, wd_bf, MOE_TILE)
    return _moe_combine(ffn_sorted, slot, x, gate, final_g, final_norm)


def _head_selectors():
    r = lax.broadcasted_iota(jnp.int32, (D_MODEL, LANES), 0)
    c = lax.broadcasted_iota(jnp.int32, (D_MODEL, LANES), 1)
    sel = jnp.where(r // HEAD_DIM == c, 1.0, 0.0).astype(BF16)
    rt = lax.broadcasted_iota(jnp.int32, (LANES, D_MODEL), 0)
    ct = lax.broadcasted_iota(jnp.int32, (LANES, D_MODEL), 1)
    selt = jnp.where(ct // HEAD_DIM == rt, 1.0, 0.0).astype(BF16)
    return jnp.concatenate([sel, sel], axis=0), jnp.concatenate([selt, selt], axis=0)


def _head_sum(x, sel2, selt2):
    return _dot2(_dot2(x, sel2), selt2)


def _rwkv_pre_kernel(x_ref, prev_ref, shift_ref, g_ref, sc_ref, sh_ref, mu_ref,
                     wr_ref, wk_ref, wv_ref, w0_ref, w1_ref, w2_ref, a0_ref, a1_ref, a2_ref,
                     g1_ref, g2_ref, kk_ref, ka_ref,
                     r_ref, w_ref, k_ref, v_ref, na_ref, kb_ref, go_ref, so_ref):
    bs, tl, _ = x_ref.shape
    t = bs * tl
    i = pl.program_id(1)
    g, sc, sh = g_ref[...], sc_ref[...], sh_ref[...]
    h = _norm_mod(x_ref[...], g, sc, sh)
    so_ref[...] = h[:, tl - 1:tl, :]
    h_before = _norm_mod(prev_ref[:, 7:8, :], g, sc, sh)
    first = jnp.where(i == 0, shift_ref[...], h_before)
    row = lax.broadcasted_iota(jnp.int32, (bs, tl, D_MODEL), 1)
    prev = jnp.where(row == 0, first, pltpu.roll(h, 1, 1))
    h = h.reshape(t, D_MODEL)
    xx = prev.reshape(t, D_MODEL) - h
    mixed = [(h + xx * mu_ref[j:j + 1, :]).astype(BF16) for j in range(6)]
    xr, xw, xk, xv, xa, xg = mixed
    r = _dot(xr, wr_ref[...])
    k = _dot(xk, wk_ref[...])
    v = _dot(xv, wv_ref[...])
    wl = w0_ref[...] + _dot(jnp.tanh(_dot(xw, w1_ref[...])).astype(BF16), w2_ref[...])
    neg = -wl
    softplus = jnp.maximum(neg, 0.0) + jnp.log(1.0 + jnp.exp(-jnp.abs(neg)))
    w = -softplus - 0.5
    log_decay = -jnp.exp(w)
    a = _sigmoid(a0_ref[...] + _dot(_dot(xa, a1_ref[...]).astype(BF16), a2_ref[...]))
    gate = _dot(_sigmoid(_dot(xg, g1_ref[...])).astype(BF16), g2_ref[...])
    sel2, selt2 = _head_selectors()
    kk = k * kk_ref[...]
    norm = jnp.maximum(jnp.sqrt(_head_sum(kk * kk, sel2, selt2)), 1e-12)
    kk = kk / norm
    k = k * (1.0 + (a - 1.0) * ka_ref[...])
    shape = (bs, tl, D_MODEL)
    r_ref[...] = r.reshape(shape)
    w_ref[...] = log_decay.reshape(shape)
    k_ref[...] = k.reshape(shape)
    v_ref[...] = v.reshape(shape)
    na_ref[...] = (-kk).reshape(shape)
    kb_ref[...] = (kk * a).reshape(shape)
    go_ref[...] = gate.reshape(shape)


def _rwkv_pre(x, shift0, g, sc, sh, p):
    seqs, length, _ = x.shape
    bs, tl = _seq_blocks(seqs, length, RWKV_PRE_TILE)
    tok = pl.BlockSpec((bs, tl, D_MODEL), lambda s, i: (s, i, 0))
    prev = pl.BlockSpec((bs, 8, D_MODEL), lambda s, i: (s, jnp.maximum(i * (tl // 8) - 1, 0), 0))
    mod = pl.BlockSpec((bs, 1, D_MODEL), lambda s, i: (s, 0, 0))
    one = pl.BlockSpec((1, 1, D_MODEL), lambda s, i: (0, 0, 0))

    def full(a):
        return pl.BlockSpec(a.shape, lambda s, i: (0,) * a.ndim)

    vec = lambda a: a.reshape(1, D_MODEL)
    weights = [p['rw_mu'], p['rw_w_r'], p['rw_w_k'], p['rw_w_v'], vec(p['rw_w0']), p['rw_w1'],
               p['rw_w2'], vec(p['rw_a0']), p['rw_a1'], p['rw_a2'], p['rw_g1'], p['rw_g2'],
               vec(p['rw_k_k']), vec(p['rw_k_a'])]
    f32o = jax.ShapeDtypeStruct((seqs, length, D_MODEL), F32)
    outs = pl.pallas_call(
        _rwkv_pre_kernel,
        out_shape=(f32o,) * 7 + (jax.ShapeDtypeStruct((seqs, 1, D_MODEL), F32),),
        grid=(seqs // bs, length // tl),
        in_specs=[tok, prev, mod, one, mod, mod] + [full(a) for a in weights],
        out_specs=(tok,) * 7 + (mod,),
        compiler_params=_cparams("arbitrary", "arbitrary"),
        name="rwkv_pre",
    )(x, x, shift0, g.reshape(1, 1, D_MODEL), sc, sh, *weights)
    return outs


def _bdot(a, b, dims=None):
    a, b = a.astype(BF16), b.astype(BF16)
    if dims is None:
        return jnp.dot(a, b, preferred_element_type=F32)
    return lax.dot_general(a, b, dims, preferred_element_type=F32)


def _scan_kernel(r_ref, lw_ref, k_ref, v_ref, a_ref, b_ref, s0_ref, y_ref, st_ref, z_ref, *, chunk):
    tc = r_ref.shape[1]
    t2 = 2 * chunk
    c = pl.program_id(1)
    lane_a = _pair_masks(chunk)
    fold = lambda x: jnp.where(lane_a, x[:chunk], x[chunk:])
    stack = lambda x: jnp.concatenate([x, x], axis=0)
    split_heads = lambda x: _stack_heads(x, axis=0)

    def block_diag(mats):
        n = mats[0].shape[0]
        zero = jnp.zeros((n, n), F32)
        return jnp.concatenate(
            [jnp.concatenate([m if j == i else zero for j in range(len(mats))], axis=1)
             for i, m in enumerate(mats)], axis=0)

    @pl.when(c == 0)
    def _():
        for q in range(PAIRS // SCAN_GROUP):
            z_ref[q] = block_diag([s0_ref[0, 2 * SCAN_GROUP * q + h] for h in range(2 * SCAN_GROUP)])

    row = lax.broadcasted_iota(jnp.int32, (t2, t2), 0)
    col = lax.broadcasted_iota(jnp.int32, (t2, t2), 1)
    same_head = (row // chunk) == (col // chunk)
    strict = jnp.logical_and(same_head, (row % chunk) > (col % chunk))
    incl = jnp.logical_and(same_head, (row % chunk) >= (col % chunk))
    tr = lax.broadcasted_iota(jnp.int32, (chunk, chunk), 0)
    tcol = lax.broadcasted_iota(jnp.int32, (chunk, chunk), 1)
    tri = jnp.where(tr >= tcol, 1.0, 0.0).astype(BF16)
    tri2 = jnp.concatenate([tri, tri], axis=1)
    gw = SCAN_GROUP * LANES
    zr = lax.broadcasted_iota(jnp.int32, (gw, gw), 0)
    zc = lax.broadcasted_iota(jnp.int32, (gw, gw), 1)
    z_diag = (zr // HEAD_DIM) == (zc // HEAD_DIM)
    n_doublings = chunk.bit_length() - 1

    def one_chunk(ci, _):
        rows = pl.ds(pl.multiple_of(ci * chunk, chunk), chunk)
        lw, r, k, v, a, b = [ref[0, rows, :] for ref in (lw_ref, r_ref, k_ref, v_ref, a_ref, b_ref)]
        hi, lo = _split2(lw)
        cum = _dot(tri2, jnp.concatenate([hi, lo], axis=0))
        grow = jnp.exp(-cum)
        decay = jnp.exp(cum)
        total = decay[chunk - 1:chunk, :]
        a_t = a * jnp.exp(cum - lw)
        r_t = r * decay
        b_t = b * grow
        k_t = k * grow
        b_end = b_t * total
        k_end = k_t * total
        n_groups = PAIRS // SCAN_GROUP
        pair = lambda x, p: x[:, _pair_lanes(p)]
        grp = lambda x, q: x[:, q * gw:(q + 1) * gw]
        members = lambda q: range(SCAN_GROUP * q, SCAN_GROUP * (q + 1))
        per_group = lambda f: jnp.stack([f(q) for q in range(n_groups)])
        g = _bdot(jnp.stack([jnp.concatenate([split_heads(pair(a_t, p)), split_heads(pair(r_t, p))], axis=0)
                             for p in range(PAIRS)]),
                  jnp.stack([jnp.concatenate([stack(pair(b_t, p)), stack(pair(k_t, p))], axis=0)
                             for p in range(PAIRS)]), _BATCHED_NT)
        l_ab = jnp.where(strict, g[:, :t2, :t2], 0.0)
        l_ak = jnp.where(strict, g[:, :t2, t2:], 0.0)
        l_rb = jnp.where(incl, g[:, t2:, :t2], 0.0)
        l_rk = jnp.where(incl, g[:, t2:, t2:], 0.0)
        diag = lambda m: per_group(lambda q: block_diag([m[p] for p in members(q)]))
        z = z_ref[...]
        applied = _bdot(per_group(lambda q: jnp.concatenate([grp(a_t, q), grp(r_t, q)], axis=0)),
                        z, _BATCHED_NT)
        vv = per_group(lambda q: jnp.concatenate([stack(pair(v, p)) for p in members(q)], axis=0))
        x = per_group(lambda q: jnp.concatenate(
            [stack(applied[q, :chunk, j * LANES:(j + 1) * LANES]) for j in range(SCAN_GROUP)],
            axis=0)) + _bdot(diag(l_ak), vv, _BATCHED)
        power = diag(l_ab)
        for i in range(n_doublings):
            x = x + _bdot(power, x, _BATCHED)
            if i + 1 < n_doublings:
                power = _bdot(power, power, _BATCHED)
        unstack = lambda m, q: jnp.concatenate(
            [fold(m[q, j * t2:(j + 1) * t2]) for j in range(SCAN_GROUP)], axis=1)
        u = [unstack(x, q) for q in range(n_groups)]
        uu = per_group(lambda q: jnp.concatenate(
            [stack(u[q][:, j * LANES:(j + 1) * LANES]) for j in range(SCAN_GROUP)], axis=0))
        ys = _bdot(jnp.concatenate([diag(l_rb), diag(l_rk)], axis=2),
                   jnp.concatenate([uu, vv], axis=1), _BATCHED)
        y_ref[0, rows, :] = jnp.concatenate(
            [applied[q, chunk:] + unstack(ys, q) for q in range(n_groups)], axis=1)
        grown = _bdot(per_group(lambda q: jnp.concatenate([u[q], grp(v, q)], axis=0)),
                      per_group(lambda q: jnp.concatenate([grp(b_end, q), grp(k_end, q)], axis=0)),
                      _BATCHED_TN)
        z_ref[...] = (z * per_group(lambda q: grp(total, q)) + jnp.where(z_diag, grown, 0.0))
        return 0

    lax.fori_loop(0, tc // chunk, one_chunk, 0)

    @pl.when(c == pl.num_programs(1) - 1)
    def _():
        for q in range(PAIRS // SCAN_GROUP):
            z = z_ref[q]
            for h in range(2 * SCAN_GROUP):
                block = slice(h * HEAD_DIM, (h + 1) * HEAD_DIM)
                st_ref[0, 2 * SCAN_GROUP * q + h] = z[block, block]


def _scan(r, lw, k, v, na, kb, s0):
    seqs, length, _ = r.shape
    chunk = min(length, SCAN_CHUNK)
    tc = min(length, SCAN_BLOCK)
    assert length % tc == 0 and tc % chunk == 0 and chunk & (chunk - 1) == 0
    tok = pl.BlockSpec((1, tc, D_MODEL), lambda s, c: (s, c, 0))
    st = pl.BlockSpec((1, HEADS, HEAD_DIM, HEAD_DIM), lambda s, c: (s, 0, 0, 0))
    return pl.pallas_call(
        functools.partial(_scan_kernel, chunk=chunk),
        out_shape=(jax.ShapeDtypeStruct((seqs, length, D_MODEL), F32),
                   jax.ShapeDtypeStruct((seqs, HEADS, HEAD_DIM, HEAD_DIM), F32)),
        grid=(seqs, length // tc),
        in_specs=[tok] * 6 + [st],
        out_specs=(tok, st),
        scratch_shapes=[pltpu.VMEM((PAIRS // SCAN_GROUP, SCAN_GROUP * LANES, SCAN_GROUP * LANES), F32)],
        compiler_params=_cparams("arbitrary", "arbitrary"),
        name="rwkv_scan",
    )(r, lw, k, v, na, kb, s0)


def _rwkv_post_kernel(y_ref, r_ref, k_ref, v_ref, g_ref, lg_ref, lb_ref, rk_ref, o_ref):
    bs, tl, _ = y_ref.shape
    t = bs * tl
    sel2, selt2 = _head_selectors()
    y = y_ref[...].reshape(t, D_MODEL)
    mu = _head_sum(y, sel2, selt2) * (1.0 / HEAD_DIM)
    d = y - mu
    var = _head_sum(d * d, sel2, selt2) * (1.0 / HEAD_DIM)
    yn = d * lax.rsqrt(var + LNX_EPS) * lg_ref[...] + lb_ref[...]
    rk = r_ref[...].reshape(t, D_MODEL) * k_ref[...].reshape(t, D_MODEL) * rk_ref[...]
    bonus = _head_sum(rk, sel2, selt2) * v_ref[...].reshape(t, D_MODEL)
    o_ref[...] = ((yn + bonus) * g_ref[...].reshape(t, D_MODEL)).reshape(bs, tl, D_MODEL)


def _rwkv_post(y, r, k, v, gate, lnx_g, lnx_b, r_k):
    seqs, length, _ = y.shape
    bs, tl = _seq_blocks(seqs, length)
    tok = pl.BlockSpec((bs, tl, D_MODEL), lambda s, i: (s, i, 0))
    vec = pl.BlockSpec((1, D_MODEL), lambda s, i: (0, 0))
    return pl.pallas_call(
        _rwkv_post_kernel,
        out_shape=jax.ShapeDtypeStruct((seqs, length, D_MODEL), F32),
        grid=(seqs // bs, length // tl),
        in_specs=[tok] * 5 + [vec] * 3,
        out_specs=tok,
        compiler_params=_cparams("arbitrary", "arbitrary"),
        name="rwkv_post",
    )(y, r, k, v, gate, lnx_g.reshape(1, D_MODEL), lnx_b.reshape(1, D_MODEL),
      r_k.reshape(1, D_MODEL))


def _trunk(x, mod, p, wb, k_cache, v_cache, shift0, s0):
    seqs, length, _ = x.shape
    part = lambda layer, j: mod[layer, :, j * D_MODEL:(j + 1) * D_MODEL].reshape(seqs, 1, D_MODEL)
    router_wt = p['router_w'].T
    sh1, sc1, g1, sh2, sc2, g2 = [part(0, j) for j in range(6)]
    k, v, qh, ql, kh, kl, vh, vl = _qkv(x, p['norm_mix_g'][0], sc1, sh1, *wb['sb_w_qkv'])
    if k_cache is None:
        attn = _attn_prompt((qh, ql), (kh, kl), (vh, vl))
    else:
        past = k_cache.shape[1]
        attn = _attn_sample((qh, ql), (kh, kl), (vh, vl), k_cache.reshape(seqs, past, D_MODEL),
                            v_cache.reshape(seqs, past, D_MODEL))
    x, h, gates, group = _oproj_router(attn, x, g1, wb['sb_w_o'], p['norm_ffn_g'][0], sc2, sh2,
                                       router_wt, p['router_bias'])
    x = _moe(h, group, gates, x, g2, p['final_norm_g'], wb['exp_w_gate'][0], wb['exp_w_up'][0],
             wb['exp_w_down'][0], final_norm=False)
    sb_k = k.reshape(seqs, length, HEADS, HEAD_DIM)
    sb_v = v.reshape(seqs, length, HEADS, HEAD_DIM)
    sh1, sc1, g1, sh2, sc2, g2 = [part(1, j) for j in range(6)]
    r, w, kq, vv, na, kb, gate, shift = _rwkv_pre(x, shift0, p['norm_mix_g'][1], sc1, sh1, wb)
    y, state = _scan(r, w, kq, vv, na, kb, s0)
    m = _rwkv_post(y, r, kq, vv, gate, p['rw_lnx_g'], p['rw_lnx_b'], p['rw_r_k'])
    x, h, gates, group = _oproj_router(m, x, g1, wb['rw_w_o'], p['norm_ffn_g'][1], sc2, sh2,
                                       router_wt, p['router_bias'])
    y_out = _moe(h, group, gates, x, g2, p['final_norm_g'], wb['exp_w_gate'][1], wb['exp_w_up'][1],
                 wb['exp_w_down'][1], final_norm=True)
    return y_out, sb_k, sb_v, state, shift


def kernel(x_prompt, x_sample, cache_sb_k, cache_sb_v, state_rw_wkv, state_rw_shift,
           c_prompt, c_sample, ada_w, ada_b, norm_mix_g, norm_ffn_g, final_norm_g,
           sb_w_qkv, sb_w_o, rw_mu, rw_w_r, rw_w_k, rw_w_v, rw_w0, rw_w1, rw_w2,
           rw_a0, rw_a1, rw_a2, rw_g1, rw_g2, rw_k_k, rw_k_a, rw_r_k, rw_lnx_g, rw_lnx_b,
           rw_w_o, router_w, router_bias, exp_w_gate, exp_w_up, exp_w_down):
    p = {
        'norm_mix_g': norm_mix_g, 'norm_ffn_g': norm_ffn_g, 'final_norm_g': final_norm_g,
        'rw_r_k': rw_r_k, 'rw_lnx_g': rw_lnx_g, 'rw_lnx_b': rw_lnx_b,
        'router_w': router_w, 'router_bias': router_bias,
    }
    bf = lambda a: a.astype(BF16)
    wb = {
        'sb_w_qkv': _split_weight(sb_w_qkv), 'sb_w_o': _split_weight(sb_w_o),
        'rw_w_o': _split_weight(rw_w_o),
        'rw_mu': rw_mu, 'rw_w_r': bf(rw_w_r), 'rw_w_k': bf(rw_w_k), 'rw_w_v': bf(rw_w_v),
        'rw_w0': rw_w0, 'rw_w1': bf(rw_w1), 'rw_w2': bf(rw_w2), 'rw_a0': rw_a0,
        'rw_a1': bf(rw_a1), 'rw_a2': bf(rw_a2), 'rw_g1': bf(rw_g1), 'rw_g2': bf(rw_g2),
        'rw_k_k': rw_k_k, 'rw_k_a': rw_k_a,
        'exp_w_gate': bf(exp_w_gate), 'exp_w_up': bf(exp_w_up), 'exp_w_down': bf(exp_w_down),
    }
    n_prompt = x_prompt.shape[0]
    mod = _ada(jnp.concatenate([c_prompt, c_sample], axis=0), ada_w, ada_b)
    shift0 = jnp.zeros((n_prompt, 1, D_MODEL), x_prompt.dtype)
    s0 = jnp.zeros((n_prompt, HEADS, HEAD_DIM, HEAD_DIM), F32)
    y_p, k_p, v_p, st_p, sh_p = _trunk(x_prompt, mod[:, :n_prompt], p, wb, None, None, shift0, s0)
    y_s, k_s, v_s, st_s, sh_s = _trunk(x_sample, mod[:, n_prompt:], p, wb, cache_sb_k, cache_sb_v,
                                       state_rw_shift, state_rw_wkv)
    return (y_p, y_s, k_p, v_p, k_s, v_s, st_p, sh_p, st_s, sh_s)
```

```python
import functools

import jax
import jax.numpy as jnp
from jax import lax
from jax.experimental import pallas as pl
from jax.experimental.pallas import tpu as pltpu

F32 = jnp.float32
BF16 = jnp.bfloat16

D_MODEL = 1024
HEADS = 16
HEAD_DIM = 64
LANES = 128
SUBLANES = 8
PAIRS = D_MODEL // LANES
N_EXPERTS = 16
N_GROUPS = 4
EXPERTS_PER_GROUP = 4
D_EXPERT = 512
NORM_EPS = 1e-6
LNX_EPS = 64e-5
ZERO_WEIGHT_CARRY = 104.0
TOKEN_TILE = 512
RWKV_PRE_TILE = 256
QKV_TILE = 256
SCAN_CHUNK = 64
SCAN_BLOCK = 256
SCAN_GROUP = 2
MOE_TILE = 512
VMEM_LIMIT = 56 * 1024 * 1024

_NT = (((1,), (1,)), ((), ()))
_BATCHED = (((2,), (1,)), ((0,), (0,)))
_BATCHED_NT = (((2,), (2,)), ((0,), (0,)))
_BATCHED_TN = (((1,), (1,)), ((0,), (0,)))


def _cparams(*sem):
    return pltpu.CompilerParams(dimension_semantics=sem, vmem_limit_bytes=VMEM_LIMIT)


def _sigmoid(x):
    return 1.0 / (1.0 + jnp.exp(-x))


def _split2(x):
    hi = x.astype(BF16)
    lo = (x - hi.astype(F32)).astype(BF16)
    return hi, lo


def _dot(a, b):
    return jnp.dot(a, b, preferred_element_type=F32)


def _dot2(x, w2):
    hi, lo = _split2(x)
    return _dot(jnp.concatenate([hi, lo], axis=1), w2)


def _norm_mod(x, g, sc, sh):
    ms = jnp.mean(x * x, axis=-1, keepdims=True)
    y = x * lax.rsqrt(ms + NORM_EPS) * g
    return y * (1.0 + sc) + sh


def _seq_blocks(seqs, length, tile=TOKEN_TILE):
    if length >= tile:
        assert length % tile == 0
        return 1, tile
    assert tile % length == 0 and seqs % (tile // length) == 0 and length % 16 == 0
    return tile // length, length


def _ada_kernel(c_ref, w_ref, b_ref, o_ref):
    c = c_ref[...]
    s = c * _sigmoid(c)
    o_ref[0] = jnp.dot(s, w_ref[0], precision=lax.Precision.HIGHEST,
                       preferred_element_type=F32) + b_ref[0]


def _ada(c, ada_w, ada_b):
    rows = c.shape[0]
    depth = ada_w.shape[0]
    return pl.pallas_call(
        _ada_kernel,
        out_shape=jax.ShapeDtypeStruct((depth, rows, 6 * D_MODEL), F32),
        grid=(depth, 6),
        in_specs=[pl.BlockSpec((rows, D_MODEL), lambda l, j: (0, 0)),
                  pl.BlockSpec((1, D_MODEL, D_MODEL), lambda l, j: (l, 0, j)),
                  pl.BlockSpec((1, 1, D_MODEL), lambda l, j: (l, 0, j))],
        out_specs=pl.BlockSpec((1, rows, D_MODEL), lambda l, j: (l, 0, j)),
        compiler_params=_cparams("arbitrary", "arbitrary"),
        name="ada_mod",
    )(c, ada_w, ada_b.reshape(depth, 1, 6 * D_MODEL))


def _dot3(x, w_hi, w_lo):
    hi, lo = _split2(x)
    rows = x.shape[0]
    both = _dot(jnp.concatenate([hi, lo], axis=0), w_hi)
    return both[:rows] + both[rows:] + _dot(hi, w_lo)


def _split_weight(w):
    hi = w.astype(BF16)
    return hi, (w - hi.astype(F32)).astype(BF16)


def _qkv_kernel(x_ref, g_ref, sc_ref, sh_ref, wh_ref, wl_ref, k_ref, v_ref, *piece_refs):
    bs, tl, _ = x_ref.shape
    h = _norm_mod(x_ref[...], g_ref[...], sc_ref[...], sh_ref[...]).reshape(bs * tl, D_MODEL)
    qkv = _dot3(h, wh_ref[...], wl_ref[...])
    q = qkv[:, :D_MODEL] * (HEAD_DIM ** -0.5)
    k = qkv[:, D_MODEL:2 * D_MODEL]
    v = qkv[:, 2 * D_MODEL:]
    k_ref[...] = k.reshape(bs, tl, D_MODEL)
    v_ref[...] = v.reshape(bs, tl, D_MODEL)
    pieces = _split2(q) + _split2(k) + _split2(v)
    for ref, piece in zip(piece_refs, pieces):
        ref[...] = piece.reshape(bs, tl, D_MODEL)


def _qkv(x, g, sc, sh, w_hi, w_lo):
    seqs, length, _ = x.shape
    bs, tl = _seq_blocks(seqs, length, QKV_TILE)
    tok = pl.BlockSpec((bs, tl, D_MODEL), lambda s, i: (s, i, 0))
    mod = pl.BlockSpec((bs, 1, D_MODEL), lambda s, i: (s, 0, 0))
    wspec = pl.BlockSpec((D_MODEL, 3 * D_MODEL), lambda s, i: (0, 0))
    f32o = jax.ShapeDtypeStruct((seqs, length, D_MODEL), F32)
    bfo = jax.ShapeDtypeStruct((seqs, length, D_MODEL), BF16)
    return pl.pallas_call(
        _qkv_kernel,
        out_shape=(f32o, f32o) + (bfo,) * 6,
        grid=(seqs // bs, length // tl),
        in_specs=[tok, pl.BlockSpec((1, 1, D_MODEL), lambda s, i: (0, 0, 0)), mod, mod, wspec, wspec],
        out_specs=(tok,) * 8,
        compiler_params=_cparams("arbitrary", "arbitrary"),
        name="sb_qkv",
    )(x, g.reshape(1, 1, D_MODEL), sc, sh, w_hi, w_lo)


def _pair_masks(rows):
    lane = lax.broadcasted_iota(jnp.int32, (rows, LANES), 1)
    return lane < HEAD_DIM


def _suffix_matrix():
    r = lax.broadcasted_iota(jnp.int32, (LANES, LANES), 0)
    c = lax.broadcasted_iota(jnp.int32, (LANES, LANES), 1)
    u = jnp.where(r > c, 1.0, 0.0).astype(BF16)
    return jnp.concatenate([u, u], axis=0)


def _query_pieces(q_hi, q_lo):
    hi, lo = _stack_heads(q_hi), _stack_heads(q_lo)
    return jnp.concatenate([hi, lo, hi], axis=1)


def _key_pieces(k_hi, k_lo):
    return jnp.concatenate([k_hi, k_hi, k_lo], axis=1)


def _sb_sweep(q3, keys, values, state, u2, visible):
    n = len(q3)
    rows = q3[0].shape[0]
    tq = rows // 2
    wide = lambda x: jnp.concatenate([x[:tq], x[tq:]], axis=1)
    z_all = lax.dot_general(jnp.stack(q3), jnp.stack([_key_pieces(*keys[p]) for p in range(n)]),
                            _BATCHED_NT, preferred_element_type=F32)
    zs, sps = [], []
    for p in range(n):
        z = z_all[p]
        sp = jnp.maximum(z, 0.0) + jnp.log(1.0 + jnp.exp(-jnp.abs(z)))
        if visible is not None:
            sp = jnp.where(visible, sp, 0.0)
        zs.append(z)
        sps.append(sp)
    later_all = _dot2(jnp.concatenate(sps, axis=0), u2)
    weights, vals = [], []
    for p in range(n):
        later = later_all[p * rows:(p + 1) * rows]
        w = jnp.exp(zs[p] - sps[p] - later - state[p][0])
        if visible is not None:
            w = jnp.where(visible, w, 0.0)
        w_hi, w_lo = _split2(w)
        v_hi, v_lo = _stack_heads(values[p][0]), _stack_heads(values[p][1])
        weights.append(jnp.concatenate([wide(w_hi), wide(w_lo), wide(w_hi)], axis=1))
        vals.append(jnp.concatenate([v_hi, v_hi, v_lo], axis=0))
    gained = lax.dot_general(jnp.stack(weights), jnp.stack(vals), _BATCHED,
                             preferred_element_type=F32)
    return tuple((state[p][0] + jnp.sum(sps[p], axis=-1, keepdims=True), state[p][1] + gained[p])
                 for p in range(n))


def _stack_heads(x, axis=0):
    mask_a = _pair_masks(x.shape[0])
    return jnp.concatenate([jnp.where(mask_a, x, 0), jnp.where(mask_a, 0, x)], axis=axis)


def _diag_visible(tq):
    row = lax.broadcasted_iota(jnp.int32, (2 * tq, LANES), 0)
    col = lax.broadcasted_iota(jnp.int32, (2 * tq, LANES), 1)
    qpos = jnp.where(row >= tq, row - tq, row)
    return col < qpos


def _pair_lanes(p):
    return slice(p * LANES, (p + 1) * LANES)


def _sweep_earlier_tiles(n_tiles, sweep, state):
    def cond(c):
        return jnp.logical_and(c[0] < n_tiles, jnp.logical_not(c[1]))

    def body(c):
        st = sweep(n_tiles - 1 - c[0], c[2])
        low = st[0][0]
        for p in range(1, len(st)):
            low = jnp.minimum(low, st[p][0])
        return c[0] + 1, jnp.min(low) > ZERO_WEIGHT_CARRY, st

    swept, _, state = lax.while_loop(cond, body, (jnp.int32(0), False, state))
    return state, swept


def _attn_prompt_kernel(qh_ref, ql_ref, kh_ref, kl_ref, vh_ref, vl_ref, o_ref):
    tq = qh_ref.shape[1]
    qi = pl.program_id(1)
    u2 = _suffix_matrix()
    q3 = [_query_pieces(qh_ref[0, :, _pair_lanes(p)], ql_ref[0, :, _pair_lanes(p)])
          for p in range(PAIRS)]

    def sweep(start, state, visible):
        rows = pl.ds(pl.multiple_of(start, tq), tq)
        tile = lambda ref, p: ref[0, rows, _pair_lanes(p)]
        return _sb_sweep(q3, [(tile(kh_ref, p), tile(kl_ref, p)) for p in range(PAIRS)],
                         [(tile(vh_ref, p), tile(vl_ref, p)) for p in range(PAIRS)],
                         state, u2, visible)

    zero = (jnp.zeros((2 * tq, LANES), F32), jnp.zeros((tq, LANES), F32))
    state = sweep(qi * tq, (zero,) * PAIRS, _diag_visible(tq))
    state, _ = _sweep_earlier_tiles(qi, lambda j, st: sweep(j * tq, st, None), state)
    for p in range(PAIRS):
        o_ref[0, :, _pair_lanes(p)] = state[p][1]


def _attn_prompt(q, k, v):
    batch, length, _ = q[0].shape
    tq = LANES
    qspec = pl.BlockSpec((1, tq, D_MODEL), lambda b, i: (b, i, 0))
    kspec = pl.BlockSpec((1, length, D_MODEL), lambda b, i: (b, 0, 0), pipeline_mode=pl.Buffered(1))
    return pl.pallas_call(
        _attn_prompt_kernel,
        out_shape=jax.ShapeDtypeStruct((batch, length, D_MODEL), F32),
        grid=(batch, length // tq),
        in_specs=[qspec, qspec, kspec, kspec, kspec, kspec],
        out_specs=qspec,
        compiler_params=_cparams("arbitrary", "arbitrary"),
        name="sb_attn_prompt",
    )(*q, *k, *v)


def _attn_sample_kernel(qh_ref, ql_ref, kh_ref, kl_ref, vh_ref, vl_ref, kc_hbm, vc_hbm, o_ref,
                        kbuf, vbuf, sem):
    b = pl.program_id(0)
    tq = qh_ref.shape[1]
    n_tiles = kc_hbm.shape[1] // LANES
    u2 = _suffix_matrix()
    pad = jnp.zeros((LANES - tq, LANES), BF16)
    padded = lambda ref, p: jnp.concatenate([ref[0, :, _pair_lanes(p)], pad], axis=0)

    def tile_copies(j, slot):
        rows = pl.ds(j * LANES, LANES)
        return (pltpu.make_async_copy(kc_hbm.at[b, rows], kbuf.at[slot], sem.at[0, slot]),
                pltpu.make_async_copy(vc_hbm.at[b, rows], vbuf.at[slot], sem.at[1, slot]))

    def start(j, slot):
        for copy in tile_copies(j, slot):
            copy.start()

    def wait(slot):
        for copy in tile_copies(0, slot):
            copy.wait()

    start(n_tiles - 1, 0)
    q3 = [_query_pieces(qh_ref[0, :, _pair_lanes(p)], ql_ref[0, :, _pair_lanes(p)])
          for p in range(PAIRS)]
    zero = (jnp.zeros((2 * tq, LANES), F32), jnp.zeros((tq, LANES), F32))
    state = _sb_sweep(q3, [(padded(kh_ref, p), padded(kl_ref, p)) for p in range(PAIRS)],
                      [(padded(vh_ref, p), padded(vl_ref, p)) for p in range(PAIRS)],
                      (zero,) * PAIRS, u2, _diag_visible(tq))

    def sweep(j, st):
        slot = (n_tiles - 1 - j) % 2

        @pl.when(j > 0)
        def _():
            start(j - 1, 1 - slot)

        wait(slot)
        pair_tile = lambda buf, p: jnp.concatenate([buf[slot, :, 2 * p, :], buf[slot, :, 2 * p + 1, :]],
                                                   axis=-1)
        return _sb_sweep(q3, [_split2(pair_tile(kbuf, p)) for p in range(PAIRS)],
                         [_split2(pair_tile(vbuf, p)) for p in range(PAIRS)], st, u2, None)

    state, swept = _sweep_earlier_tiles(n_tiles, sweep, state)

    @pl.when(swept < n_tiles)
    def _():
        wait(swept % 2)

    for p in range(PAIRS):
        o_ref[0, :, _pair_lanes(p)] = state[p][1]


def _attn_sample(q, k, v, k_cache, v_cache):
    batch, length, _ = q[0].shape
    past = k_cache.shape[1]
    assert length <= LANES and past % LANES == 0
    qspec = pl.BlockSpec((1, length, D_MODEL), lambda b: (b, 0, 0))
    hbm = pl.BlockSpec(memory_space=pl.ANY)
    tile_buf = pltpu.VMEM((2, LANES, HEADS, HEAD_DIM), F32)
    return pl.pallas_call(
        _attn_sample_kernel,
        out_shape=jax.ShapeDtypeStruct((batch, length, D_MODEL), F32),
        grid=(batch,),
        in_specs=[qspec] * 6 + [hbm, hbm],
        out_specs=qspec,
        scratch_shapes=[tile_buf, tile_buf, pltpu.SemaphoreType.DMA((2, 2))],
        compiler_params=_cparams("arbitrary"),
        name="sb_attn_sample",
    )(*q, *k, *v, k_cache, v_cache)


def _row_max(rows):
    m = rows[0]
    for r in rows[1:]:
        m = jnp.maximum(m, r)
    return m


def _first_hit(rows, m):
    hits = []
    taken = None
    for r in rows:
        eq = r == m
        hit = eq if taken is None else jnp.logical_and(eq, jnp.logical_not(taken))
        taken = eq if taken is None else jnp.logical_or(taken, eq)
        hits.append(hit)
    return hits


def _route(s_rows, sel_rows):
    neg = jnp.float32(-jnp.inf)
    top1, top2, score = [], [], []
    for g in range(N_GROUPS):
        rows = sel_rows[g * EXPERTS_PER_GROUP:(g + 1) * EXPERTS_PER_GROUP]
        m1 = _row_max(rows)
        h1 = _first_hit(rows, m1)
        rest = [jnp.where(h, neg, r) for h, r in zip(h1, rows)]
        m2 = _row_max(rest)
        h2 = _first_hit(rest, m2)
        top1.append(h1)
        top2.append(h2)
        score.append(m1 + m2)
    best = _row_max(score)
    ghit = _first_hit(score, best)
    picked = []
    for g in range(N_GROUPS):
        for e in range(EXPERTS_PER_GROUP):
            picked.append(jnp.logical_and(ghit[g], jnp.logical_or(top1[g][e], top2[g][e])))
    total = None
    for p, s in zip(picked, s_rows):
        term = jnp.where(p, s, 0.0)
        total = term if total is None else total + term
    gates = [jnp.where(p, s / total, 0.0) for p, s in zip(picked, s_rows)]
    in_group = []
    for e in range(EXPERTS_PER_GROUP):
        row = gates[e]
        for g in range(1, N_GROUPS):
            row = row + gates[g * EXPERTS_PER_GROUP + e]
        in_group.append(row)
    group = jnp.zeros_like(best)
    for g in range(1, N_GROUPS):
        group = jnp.where(ghit[g], float(g), group)
    return in_group, group


def _oproj_kernel(m_ref, x_ref, gate_ref, wh_ref, wl_ref, g_ref, sc_ref, sh_ref, rw_ref, rb_ref,
                  x1_ref, h_ref, route_ref):
    bs, tl, _ = x_ref.shape
    m = m_ref[...].reshape(bs * tl, D_MODEL)
    mix = _dot3(m, wh_ref[...], wl_ref[...]).reshape(bs, tl, D_MODEL)
    x1 = x_ref[...] + gate_ref[...] * mix
    x1_ref[...] = x1
    h = _norm_mod(x1, g_ref[...], sc_ref[...], sh_ref[...])
    h_ref[...] = h
    logits = lax.dot_general(rw_ref[...], h.reshape(bs * tl, D_MODEL), _NT,
                             precision=lax.Precision.HIGHEST,
                             preferred_element_type=F32)
    s = _sigmoid(logits)
    sel = s + rb_ref[...]
    s_rows = [s[e:e + 1, :] for e in range(N_EXPERTS)]
    sel_rows = [sel[e:e + 1, :] for e in range(N_EXPERTS)]
    in_group, group = _route(s_rows, sel_rows)
    pad = [jnp.zeros_like(group)] * (SUBLANES - EXPERTS_PER_GROUP - 1)
    route_ref[0] = jnp.concatenate(in_group + [group] + pad, axis=0)


def _oproj_router(m, x, gate, w_pieces, g, sc, sh, router_wt, router_b):
    seqs, length, _ = x.shape
    bs, tl = _seq_blocks(seqs, length)
    n_tiles = (seqs // bs) * (length // tl)
    per_seq = length // tl
    tok = pl.BlockSpec((bs, tl, D_MODEL), lambda s, i: (s, i, 0))
    mod = pl.BlockSpec((bs, 1, D_MODEL), lambda s, i: (s, 0, 0))
    one = pl.BlockSpec((1, 1, D_MODEL), lambda s, i: (0, 0, 0))
    wspec = pl.BlockSpec((D_MODEL, D_MODEL), lambda s, i: (0, 0))
    f32o = jax.ShapeDtypeStruct((seqs, length, D_MODEL), F32)
    x1, h, route = pl.pallas_call(
        _oproj_kernel,
        out_shape=(f32o, f32o, jax.ShapeDtypeStruct((n_tiles, SUBLANES, bs * tl), F32)),
        grid=(seqs // bs, per_seq),
        in_specs=[tok, tok, mod, wspec, wspec, one, mod, mod,
                  pl.BlockSpec((N_EXPERTS, D_MODEL), lambda s, i: (0, 0)),
                  pl.BlockSpec((N_EXPERTS, 1), lambda s, i: (0, 0))],
        out_specs=(tok, tok,
                   pl.BlockSpec((1, SUBLANES, bs * tl), lambda s, i: (s * per_seq + i, 0, 0))),
        compiler_params=_cparams("arbitrary", "arbitrary"),
        name="oproj_router",
    )(m, x, gate, *w_pieces, g.reshape(1, 1, D_MODEL), sc, sh, router_wt,
      router_b.reshape(N_EXPERTS, 1))
    route = jnp.swapaxes(route, 1, 2).reshape(seqs * length, SUBLANES)
    return x1, h, route[:, :EXPERTS_PER_GROUP], route[:, EXPERTS_PER_GROUP].astype(jnp.int32)


def _dispatch_plan(group, gates, tile):
    n = group.shape[0]
    n_slots = n + N_GROUPS * tile
    onehot = (group[:, None] == jnp.arange(N_GROUPS)[None, :]).astype(jnp.int32)
    counts = jnp.sum(onehot, axis=0)
    padded = (counts + tile - 1) // tile * tile
    ends = jnp.cumsum(padded)
    starts = ends - padded
    rank = jnp.take_along_axis(jnp.cumsum(onehot, axis=0) - onehot, group[:, None], axis=1)[:, 0]
    slot = starts[group] + rank
    src = jnp.zeros((n_slots,), jnp.int32).at[slot].set(jnp.arange(n, dtype=jnp.int32))
    slot_gates = jnp.zeros((n_slots, EXPERTS_PER_GROUP), F32).at[slot].set(gates)
    tile_start = jnp.arange(n_slots // tile, dtype=jnp.int32) * tile
    tile_group = jnp.minimum(jnp.sum(tile_start[:, None] >= ends[None, :], axis=1), N_GROUPS - 1)
    return src, slot.astype(jnp.int32), slot_gates, tile_group.astype(jnp.int32)


def _start_row_gather(idx_ref, base, src_hbm, dst_ref, sem):
    def body(r, _):
        row = idx_ref[base + r]
        pltpu.make_async_copy(src_hbm.at[pl.ds(row, 1), :], dst_ref.at[pl.ds(r, 1), :], sem).start()
        return 0

    lax.fori_loop(0, dst_ref.shape[0], body, 0, unroll=8)


def _wait_row_gather(src_hbm, dst_ref, sem):
    pltpu.make_async_copy(src_hbm.at[pl.ds(0, dst_ref.shape[0]), :], dst_ref, sem).wait()


def _moe_kernel(src_ref, tg_ref, h_hbm, gates_ref, wg_ref, wu_ref, wd_ref, o_ref, hbuf, sem):
    t = pl.program_id(0)
    tile = o_ref.shape[0]
    slot = t % 2

    @pl.when(t == 0)
    def _():
        _start_row_gather(src_ref, 0, h_hbm, hbuf.at[0], sem.at[0])

    @pl.when(t + 1 < pl.num_programs(0))
    def _():
        _start_row_gather(src_ref, (t + 1) * tile, h_hbm, hbuf.at[1 - slot], sem.at[1 - slot])

    _wait_row_gather(h_hbm, hbuf.at[slot], sem.at[slot])
    h = hbuf[slot].astype(BF16)
    gates = gates_ref[...]
    acc = jnp.zeros((tile, D_MODEL), F32)
    for e in range(EXPERTS_PER_GROUP):
        a = _dot(h, wg_ref[e])
        u = _dot(h, wu_ref[e])
        he = (a * _sigmoid(a) * u).astype(BF16)
        acc = acc + gates[:, e:e + 1] * _dot(he, wd_ref[e])
    o_ref[...] = acc


def _moe_experts(h, src, slot_gates, tile_group, wg_bf, wu_bf, wd_bf, tile):
    n_slots = src.shape[0]
    group_w = lambda shape: pl.BlockSpec((EXPERTS_PER_GROUP,) + shape, lambda t, src, tg: (tg[t], 0, 0))
    return pl.pallas_call(
        _moe_kernel,
        out_shape=jax.ShapeDtypeStruct((n_slots, D_MODEL), F32),
        grid_spec=pltpu.PrefetchScalarGridSpec(
            num_scalar_prefetch=2,
            grid=(n_slots // tile,),
            in_specs=[pl.BlockSpec(memory_space=pl.ANY),
                      pl.BlockSpec((tile, EXPERTS_PER_GROUP), lambda t, src, tg: (t, 0)),
                      group_w((D_MODEL, D_EXPERT)), group_w((D_MODEL, D_EXPERT)),
                      group_w((D_EXPERT, D_MODEL))],
            out_specs=pl.BlockSpec((tile, D_MODEL), lambda t, src, tg: (t, 0)),
            scratch_shapes=[pltpu.VMEM((2, tile, D_MODEL), F32), pltpu.SemaphoreType.DMA((2,))]),
        compiler_params=_cparams("arbitrary"),
        name="moe_experts",
    )(src, tile_group, h, slot_gates, wg_bf, wu_bf, wd_bf)


def _moe_combine_kernel(slot_ref, f_hbm, x_ref, gate_ref, fg_ref, o_ref, fbuf, sem, *, final_norm):
    bs, tl, _ = x_ref.shape
    tile = bs * tl
    t = pl.program_id(0) * pl.num_programs(1) + pl.program_id(1)
    n_tiles = pl.num_programs(0) * pl.num_programs(1)
    slot = t % 2

    @pl.when(t == 0)
    def _():
        _start_row_gather(slot_ref, 0, f_hbm, fbuf.at[0], sem.at[0])

    @pl.when(t + 1 < n_tiles)
    def _():
        _start_row_gather(slot_ref, (t + 1) * tile, f_hbm, fbuf.at[1 - slot], sem.at[1 - slot])

    _wait_row_gather(f_hbm, fbuf.at[slot], sem.at[slot])
    x2 = x_ref[...] + gate_ref[...] * fbuf[slot].reshape(bs, tl, D_MODEL)
    if final_norm:
        ms = jnp.mean(x2 * x2, axis=-1, keepdims=True)
        x2 = x2 * lax.rsqrt(ms + NORM_EPS) * fg_ref[...]
    o_ref[...] = x2


def _moe_combine(ffn_sorted, slot, x, gate, final_g, final_norm):
    seqs, length, _ = x.shape
    bs, tl = _seq_blocks(seqs, length)
    tok = pl.BlockSpec((bs, tl, D_MODEL), lambda s, i, slot: (s, i, 0))
    return pl.pallas_call(
        functools.partial(_moe_combine_kernel, final_norm=final_norm),
        out_shape=jax.ShapeDtypeStruct((seqs, length, D_MODEL), F32),
        grid_spec=pltpu.PrefetchScalarGridSpec(
            num_scalar_prefetch=1,
            grid=(seqs // bs, length // tl),
            in_specs=[pl.BlockSpec(memory_space=pl.ANY), tok,
                      pl.BlockSpec((bs, 1, D_MODEL), lambda s, i, slot: (s, 0, 0)),
                      pl.BlockSpec((1, 1, D_MODEL), lambda s, i, slot: (0, 0, 0))],
            out_specs=tok,
            scratch_shapes=[pltpu.VMEM((2, bs * tl, D_MODEL), F32), pltpu.SemaphoreType.DMA((2,))]),
        compiler_params=_cparams("arbitrary", "arbitrary"),
        name="moe_combine",
    )(slot, ffn_sorted, x, gate, final_g.reshape(1, 1, D_MODEL))


def _moe(h, group, gates, x, gate, final_g, wg_bf, wu_bf, wd_bf, final_norm):
    seqs, length, _ = x.shape
    tile = min(MOE_TILE, max(LANES, seqs * length // (2 * N_GROUPS)))
    src, slot, slot_gates, tile_group = _dispatch_plan(group, gates, tile)
    ffn_sorted = _moe_experts(h.reshape(seqs * length, D_MODEL), src, slot_gates, tile_group,
                              wg_bf, wu_bf, wd_bf, tile)
    return _moe_combine(ffn_sorted, slot, x, gate, final_g, final_norm)


def _head_selectors():
    r = lax.broadcasted_iota(jnp.int32, (D_MODEL, LANES), 0)
    c = lax.broadcasted_iota(jnp.int32, (D_MODEL, LANES), 1)
    sel = jnp.where(r // HEAD_DIM == c, 1.0, 0.0).astype(BF16)
    rt = lax.broadcasted_iota(jnp.int32, (LANES, D_MODEL), 0)
    ct = lax.broadcasted_iota(jnp.int32, (LANES, D_MODEL), 1)
    selt = jnp.where(ct // HEAD_DIM == rt, 1.0, 0.0).astype(BF16)
    return jnp.concatenate([sel, sel], axis=0), jnp.concatenate([selt, selt], axis=0)


def _head_sum(x, sel2, selt2):
    return _dot2(_dot2(x, sel2), selt2)


def _rwkv_pre_kernel(x_ref, prev_ref, shift_ref, g_ref, sc_ref, sh_ref, mu_ref,
                     wr_ref, wk_ref, wv_ref, w0_ref, w1_ref, w2_ref, a0_ref, a1_ref, a2_ref,
                     g1_ref, g2_ref, kk_ref, ka_ref,
                     r_ref, w_ref, k_ref, v_ref, na_ref, kb_ref, go_ref, so_ref):
    bs, tl, _ = x_ref.shape
    t = bs * tl
    i = pl.program_id(1)
    g, sc, sh = g_ref[...], sc_ref[...], sh_ref[...]
    h = _norm_mod(x_ref[...], g, sc, sh)
    so_ref[...] = h[:, tl - 1:tl, :]
    h_before = _norm_mod(prev_ref[:, 7:8, :], g, sc, sh)
    first = jnp.where(i == 0, shift_ref[...], h_before)
    row = lax.broadcasted_iota(jnp.int32, (bs, tl, D_MODEL), 1)
    prev = jnp.where(row == 0, first, pltpu.roll(h, 1, 1))
    h = h.reshape(t, D_MODEL)
    xx = prev.reshape(t, D_MODEL) - h
    mixed = [(h + xx * mu_ref[j:j + 1, :]).astype(BF16) for j in range(6)]
    xr, xw, xk, xv, xa, xg = mixed
    r = _dot(xr, wr_ref[...])
    k = _dot(xk, wk_ref[...])
    v = _dot(xv, wv_ref[...])
    wl = w0_ref[...] + _dot(jnp.tanh(_dot(xw, w1_ref[...])).astype(BF16), w2_ref[...])
    neg = -wl
    softplus = jnp.maximum(neg, 0.0) + jnp.log(1.0 + jnp.exp(-jnp.abs(neg)))
    w = -softplus - 0.5
    log_decay = -jnp.exp(w)
    a = _sigmoid(a0_ref[...] + _dot(_dot(xa, a1_ref[...]).astype(BF16), a2_ref[...]))
    gate = _dot(_sigmoid(_dot(xg, g1_ref[...])).astype(BF16), g2_ref[...])
    sel2, selt2 = _head_selectors()
    kk = k * kk_ref[...]
    norm = jnp.maximum(jnp.sqrt(_head_sum(kk * kk, sel2, selt2)), 1e-12)
    kk = kk / norm
    k = k * (1.0 + (a - 1.0) * ka_ref[...])
    shape = (bs, tl, D_MODEL)
    r_ref[...] = r.reshape(shape)
    w_ref[...] = log_decay.reshape(shape)
    k_ref[...] = k.reshape(shape)
    v_ref[...] = v.reshape(shape)
    na_ref[...] = (-kk).reshape(shape)
    kb_ref[...] = (kk * a).reshape(shape)
    go_ref[...] = gate.reshape(shape)


def _rwkv_pre(x, shift0, g, sc, sh, p):
    seqs, length, _ = x.shape
    bs, tl = _seq_blocks(seqs, length, RWKV_PRE_TILE)
    tok = pl.BlockSpec((bs, tl, D_MODEL), lambda s, i: (s, i, 0))
    prev = pl.BlockSpec((bs, 8, D_MODEL), lambda s, i: (s, jnp.maximum(i * (tl // 8) - 1, 0), 0))
    mod = pl.BlockSpec((bs, 1, D_MODEL), lambda s, i: (s, 0, 0))
    one = pl.BlockSpec((1, 1, D_MODEL), lambda s, i: (0, 0, 0))

    def full(a):
        return pl.BlockSpec(a.shape, lambda s, i: (0,) * a.ndim)

    vec = lambda a: a.reshape(1, D_MODEL)
    weights = [p['rw_mu'], p['rw_w_r'], p['rw_w_k'], p['rw_w_v'], vec(p['rw_w0']), p['rw_w1'],
               p['rw_w2'], vec(p['rw_a0']), p['rw_a1'], p['rw_a2'], p['rw_g1'], p['rw_g2'],
               vec(p['rw_k_k']), vec(p['rw_k_a'])]
    f32o = jax.ShapeDtypeStruct((seqs, length, D_MODEL), F32)
    outs = pl.pallas_call(
        _rwkv_pre_kernel,
        out_shape=(f32o,) * 7 + (jax.ShapeDtypeStruct((seqs, 1, D_MODEL), F32),),
        grid=(seqs // bs, length // tl),
        in_specs=[tok, prev, mod, one, mod, mod] + [full(a) for a in weights],
        out_specs=(tok,) * 7 + (mod,),
        compiler_params=_cparams("arbitrary", "arbitrary"),
        name="rwkv_pre",
    )(x, x, shift0, g.reshape(1, 1, D_MODEL), sc, sh, *weights)
    return outs


def _bdot(a, b, dims=None):
    a, b = a.astype(BF16), b.astype(BF16)
    if dims is None:
        return jnp.dot(a, b, preferred_element_type=F32)
    return lax.dot_general(a, b, dims, preferred_element_type=F32)


def _scan_kernel(r_ref, lw_ref, k_ref, v_ref, a_ref, b_ref, s0_ref, y_ref, st_ref, z_ref, *, chunk):
    tc = r_ref.shape[1]
    t2 = 2 * chunk
    c = pl.program_id(1)
    lane_a = _pair_masks(chunk)
    fold = lambda x: jnp.where(lane_a, x[:chunk], x[chunk:])
    stack = lambda x: jnp.concatenate([x, x], axis=0)
    split_heads = lambda x: _stack_heads(x, axis=0)

    def block_diag(mats):
        n = mats[0].shape[0]
        zero = jnp.zeros((n, n), F32)
        return jnp.concatenate(
            [jnp.concatenate([m if j == i else zero for j in range(len(mats))], axis=1)
             for i, m in enumerate(mats)], axis=0)

    @pl.when(c == 0)
    def _():
        for q in range(PAIRS // SCAN_GROUP):
            z_ref[q] = block_diag([s0_ref[0, 2 * SCAN_GROUP * q + h] for h in range(2 * SCAN_GROUP)])

    row = lax.broadcasted_iota(jnp.int32, (t2, t2), 0)
    col = lax.broadcasted_iota(jnp.int32, (t2, t2), 1)
    same_head = (row // chunk) == (col // chunk)
    strict = jnp.logical_and(same_head, (row % chunk) > (col % chunk))
    incl = jnp.logical_and(same_head, (row % chunk) >= (col % chunk))
    tr = lax.broadcasted_iota(jnp.int32, (chunk, chunk), 0)
    tcol = lax.broadcasted_iota(jnp.int32, (chunk, chunk), 1)
    tri = jnp.where(tr >= tcol, 1.0, 0.0).astype(BF16)
    tri2 = jnp.concatenate([tri, tri], axis=1)
    gw = SCAN_GROUP * LANES
    zr = lax.broadcasted_iota(jnp.int32, (gw, gw), 0)
    zc = lax.broadcasted_iota(jnp.int32, (gw, gw), 1)
    z_diag = (zr // HEAD_DIM) == (zc // HEAD_DIM)
    n_doublings = chunk.bit_length() - 1

    def one_chunk(ci, _):
        rows = pl.ds(pl.multiple_of(ci * chunk, chunk), chunk)
        lw, r, k, v, a, b = [ref[0, rows, :] for ref in (lw_ref, r_ref, k_ref, v_ref, a_ref, b_ref)]
        hi, lo = _split2(lw)
        cum = _dot(tri2, jnp.concatenate([hi, lo], axis=0))
        grow = jnp.exp(-cum)
        decay = jnp.exp(cum)
        total = decay[chunk - 1:chunk, :]
        a_t = a * jnp.exp(cum - lw)
        r_t = r * decay
        b_t = b * grow
        k_t = k * grow
        b_end = b_t * total
        k_end = k_t * total
        n_groups = PAIRS // SCAN_GROUP
        pair = lambda x, p: x[:, _pair_lanes(p)]
        grp = lambda x, q: x[:, q * gw:(q + 1) * gw]
        members = lambda q: range(SCAN_GROUP * q, SCAN_GROUP * (q + 1))
        per_group = lambda f: jnp.stack([f(q) for q in range(n_groups)])
        g = _bdot(jnp.stack([jnp.concatenate([split_heads(pair(a_t, p)), split_heads(pair(r_t, p))], axis=0)
                             for p in range(PAIRS)]),
                  jnp.stack([jnp.concatenate([stack(pair(b_t, p)), stack(pair(k_t, p))], axis=0)
                             for p in range(PAIRS)]), _BATCHED_NT)
        l_ab = jnp.where(strict, g[:, :t2, :t2], 0.0)
        l_ak = jnp.where(strict, g[:, :t2, t2:], 0.0)
        l_rb = jnp.where(incl, g[:, t2:, :t2], 0.0)
        l_rk = jnp.where(incl, g[:, t2:, t2:], 0.0)
        diag = lambda m: per_group(lambda q: block_diag([m[p] for p in members(q)]))
        z = z_ref[...]
        applied = _bdot(per_group(lambda q: jnp.concatenate([grp(a_t, q), grp(r_t, q)], axis=0)),
                        z, _BATCHED_NT)
        vv = per_group(lambda q: jnp.concatenate([stack(pair(v, p)) for p in members(q)], axis=0))
        x = per_group(lambda q: jnp.concatenate(
            [stack(applied[q, :chunk, j * LANES:(j + 1) * LANES]) for j in range(SCAN_GROUP)],
            axis=0)) + _bdot(diag(l_ak), vv, _BATCHED)
        power = diag(l_ab)
        for i in range(n_doublings):
            x = x + _bdot(power, x, _BATCHED)
            if i + 1 < n_doublings:
                power = _bdot(power, power, _BATCHED)
        unstack = lambda m, q: jnp.concatenate(
            [fold(m[q, j * t2:(j + 1) * t2]) for j in range(SCAN_GROUP)], axis=1)
        u = [unstack(x, q) for q in range(n_groups)]
        uu = per_group(lambda q: jnp.concatenate(
            [stack(u[q][:, j * LANES:(j + 1) * LANES]) for j in range(SCAN_GROUP)], axis=0))
        ys = _bdot(jnp.concatenate([diag(l_rb), diag(l_rk)], axis=2),
                   jnp.concatenate([uu, vv], axis=1), _BATCHED)
        y_ref[0, rows, :] = jnp.concatenate(
            [applied[q, chunk:] + unstack(ys, q) for q in range(n_groups)], axis=1)
        grown = _bdot(per_group(lambda q: jnp.concatenate([u[q], grp(v, q)], axis=0)),
                      per_group(lambda q: jnp.concatenate([grp(b_end, q), grp(k_end, q)], axis=0)),
                      _BATCHED_TN)
        z_ref[...] = (z * per_group(lambda q: grp(total, q)) + jnp.where(z_diag, grown, 0.0))
        return 0

    lax.fori_loop(0, tc // chunk, one_chunk, 0)

    @pl.when(c == pl.num_programs(1) - 1)
    def _():
        for q in range(PAIRS // SCAN_GROUP):
            z = z_ref[q]
            for h in range(2 * SCAN_GROUP):
                block = slice(h * HEAD_DIM, (h + 1) * HEAD_DIM)
                st_ref[0, 2 * SCAN_GROUP * q + h] = z[block, block]


def _scan(r, lw, k, v, na, kb, s0):
    seqs, length, _ = r.shape
    chunk = min(length, SCAN_CHUNK)
    tc = min(length, SCAN_BLOCK)
    assert length % tc == 0 and tc % chunk == 0 and chunk & (chunk - 1) == 0
    tok = pl.BlockSpec((1, tc, D_MODEL), lambda s, c: (s, c, 0))
    st = pl.BlockSpec((1, HEADS, HEAD_DIM, HEAD_DIM), lambda s, c: (s, 0, 0, 0))
    return pl.pallas_call(
        functools.partial(_scan_kernel, chunk=chunk),
        out_shape=(jax.ShapeDtypeStruct((seqs, length, D_MODEL), F32),
                   jax.ShapeDtypeStruct((seqs, HEADS, HEAD_DIM, HEAD_DIM), F32)),
        grid=(seqs, length // tc),
        in_specs=[tok] * 6 + [st],
        out_specs=(tok, st),
        scratch_shapes=[pltpu.VMEM((PAIRS // SCAN_GROUP, SCAN_GROUP * LANES, SCAN_GROUP * LANES), F32)],
        compiler_params=_cparams("arbitrary", "arbitrary"),
        name="rwkv_scan",
    )(r, lw, k, v, na, kb, s0)


def _rwkv_post_kernel(y_ref, r_ref, k_ref, v_ref, g_ref, lg_ref, lb_ref, rk_ref, o_ref):
    bs, tl, _ = y_ref.shape
    t = bs * tl
    sel2, selt2 = _head_selectors()
    y = y_ref[...].reshape(t, D_MODEL)
    mu = _head_sum(y, sel2, selt2) * (1.0 / HEAD_DIM)
    d = y - mu
    var = _head_sum(d * d, sel2, selt2) * (1.0 / HEAD_DIM)
    yn = d * lax.rsqrt(var + LNX_EPS) * lg_ref[...] + lb_ref[...]
    rk = r_ref[...].reshape(t, D_MODEL) * k_ref[...].reshape(t, D_MODEL) * rk_ref[...]
    bonus = _head_sum(rk, sel2, selt2) * v_ref[...].reshape(t, D_MODEL)
    o_ref[...] = ((yn + bonus) * g_ref[...].reshape(t, D_MODEL)).reshape(bs, tl, D_MODEL)


def _rwkv_post(y, r, k, v, gate, lnx_g, lnx_b, r_k):
    seqs, length, _ = y.shape
    bs, tl = _seq_blocks(seqs, length)
    tok = pl.BlockSpec((bs, tl, D_MODEL), lambda s, i: (s, i, 0))
    vec = pl.BlockSpec((1, D_MODEL), lambda s, i: (0, 0))
    return pl.pallas_call(
        _rwkv_post_kernel,
        out_shape=jax.ShapeDtypeStruct((seqs, length, D_MODEL), F32),
        grid=(seqs // bs, length // tl),
        in_specs=[tok] * 5 + [vec] * 3,
        out_specs=tok,
        compiler_params=_cparams("arbitrary", "arbitrary"),
        name="rwkv_post",
    )(y, r, k, v, gate, lnx_g.reshape(1, D_MODEL), lnx_b.reshape(1, D_MODEL),
      r_k.reshape(1, D_MODEL))


def _trunk(x, mod, p, wb, k_cache, v_cache, shift0, s0):
    seqs, length, _ = x.shape
    part = lambda layer, j: mod[layer, :, j * D_MODEL:(j + 1) * D_MODEL].reshape(seqs, 1, D_MODEL)
    router_wt = p['router_w'].T
    sh1, sc1, g1, sh2, sc2, g2 = [part(0, j) for j in range(6)]
    k, v, qh, ql, kh, kl, vh, vl = _qkv(x, p['norm_mix_g'][0], sc1, sh1, *wb['sb_w_qkv'])
    if k_cache is None:
        attn = _attn_prompt((qh, ql), (kh, kl), (vh, vl))
    else:
        attn = _attn_sample((qh, ql), (kh, kl), (vh, vl), k_cache, v_cache)
    x, h, gates, group = _oproj_router(attn, x, g1, wb['sb_w_o'], p['norm_ffn_g'][0], sc2, sh2,
                                       router_wt, p['router_bias'])
    x = _moe(h, group, gates, x, g2, p['final_norm_g'], wb['exp_w_gate'][0], wb['exp_w_up'][0],
             wb['exp_w_down'][0], final_norm=False)
    sb_k = k.reshape(seqs, length, HEADS, HEAD_DIM)
    sb_v = v.reshape(seqs, length, HEADS, HEAD_DIM)
    sh1, sc1, g1, sh2, sc2, g2 = [part(1, j) for j in range(6)]
    r, w, kq, vv, na, kb, gate, shift = _rwkv_pre(x, shift0, p['norm_mix_g'][1], sc1, sh1, wb)
    y, state = _scan(r, w, kq, vv, na, kb, s0)
    m = _rwkv_post(y, r, kq, vv, gate, p['rw_lnx_g'], p['rw_lnx_b'], p['rw_r_k'])
    x, h, gates, group = _oproj_router(m, x, g1, wb['rw_w_o'], p['norm_ffn_g'][1], sc2, sh2,
                                       router_wt, p['router_bias'])
    y_out = _moe(h, group, gates, x, g2, p['final_norm_g'], wb['exp_w_gate'][1], wb['exp_w_up'][1],
                 wb['exp_w_down'][1], final_norm=True)
    return y_out, sb_k, sb_v, state, shift


def kernel(x_prompt, x_sample, cache_sb_k, cache_sb_v, state_rw_wkv, state_rw_shift,
           c_prompt, c_sample, ada_w, ada_b, norm_mix_g, norm_ffn_g, final_norm_g,
           sb_w_qkv, sb_w_o, rw_mu, rw_w_r, rw_w_k, rw_w_v, rw_w0, rw_w1, rw_w2,
           rw_a0, rw_a1, rw_a2, rw_g1, rw_g2, rw_k_k, rw_k_a, rw_r_k, rw_lnx_g, rw_lnx_b,
           rw_w_o, router_w, router_bias, exp_w_gate, exp_w_up, exp_w_down):
    p = {
        'norm_mix_g': norm_mix_g, 'norm_ffn_g': norm_ffn_g, 'final_norm_g': final_norm_g,
        'rw_r_k': rw_r_k, 'rw_lnx_g': rw_lnx_g, 'rw_lnx_b': rw_lnx_b,
        'router_w': router_w, 'router_bias': router_bias,
    }
    bf = lambda a: a.astype(BF16)
    wb = {
        'sb_w_qkv': _split_weight(sb_w_qkv), 'sb_w_o': _split_weight(sb_w_o),
        'rw_w_o': _split_weight(rw_w_o),
        'rw_mu': rw_mu, 'rw_w_r': bf(rw_w_r), 'rw_w_k': bf(rw_w_k), 'rw_w_v': bf(rw_w_v),
        'rw_w0': rw_w0, 'rw_w1': bf(rw_w1), 'rw_w2': bf(rw_w2), 'rw_a0': rw_a0,
        'rw_a1': bf(rw_a1), 'rw_a2': bf(rw_a2), 'rw_g1': bf(rw_g1), 'rw_g2': bf(rw_g2),
        'rw_k_k': rw_k_k, 'rw_k_a': rw_k_a,
        'exp_w_gate': bf(exp_w_gate), 'exp_w_up': bf(exp_w_up), 'exp_w_down': bf(exp_w_down),
    }
    n_prompt = x_prompt.shape[0]
    mod = _ada(jnp.concatenate([c_prompt, c_sample], axis=0), ada_w, ada_b)
    shift0 = jnp.zeros((n_prompt, 1, D_MODEL), x_prompt.dtype)
    s0 = jnp.zeros((n_prompt, HEADS, HEAD_DIM, HEAD_DIM), F32)
    y_p, k_p, v_p, st_p, sh_p = _trunk(x_prompt, mod[:, :n_prompt], p, wb, None, None, shift0, s0)
    y_s, k_s, v_s, st_s, sh_s = _trunk(x_sample, mod[:, n_prompt:], p, wb, cache_sb_k, cache_sb_v,
                                       state_rw_shift, state_rw_wkv)
    return (y_p, y_s, k_p, v_p, k_s, v_s, st_p, sh_p, st_s, sh_s)
```

```python
import functools

import jax
import jax.numpy as jnp
from jax import lax
from jax.experimental import pallas as pl
from jax.experimental.pallas import tpu as pltpu

F32 = jnp.float32
BF16 = jnp.bfloat16

D_MODEL = 1024
HEADS = 16
HEAD_DIM = 64
LANES = 128
SUBLANES = 8
PAIRS = D_MODEL // LANES
N_EXPERTS = 16
N_GROUPS = 4
EXPERTS_PER_GROUP = 4
D_EXPERT = 512
NORM_EPS = 1e-6
LNX_EPS = 64e-5
ZERO_WEIGHT_CARRY = 104.0
TOKEN_TILE = 512
RWKV_PRE_TILE = 256
QKV_TILE = 256
SCAN_CHUNK = 64
SCAN_BLOCK = 256
SCAN_GROUP = 2
MOE_TILE = 512
VMEM_LIMIT = 56 * 1024 * 1024

_NT = (((1,), (1,)), ((), ()))
_BATCHED = (((2,), (1,)), ((0,), (0,)))
_BATCHED_NT = (((2,), (2,)), ((0,), (0,)))
_BATCHED_TN = (((1,), (1,)), ((0,), (0,)))


def _cparams(*sem):
    return pltpu.CompilerParams(dimension_semantics=sem, vmem_limit_bytes=VMEM_LIMIT)


def _sigmoid(x):
    return 1.0 / (1.0 + jnp.exp(-x))


def _split2(x):
    hi = x.astype(BF16)
    lo = (x - hi.astype(F32)).astype(BF16)
    return hi, lo


def _dot(a, b):
    return jnp.dot(a, b, preferred_element_type=F32)


def _dot2(x, w2):
    hi, lo = _split2(x)
    return _dot(jnp.concatenate([hi, lo], axis=1), w2)


def _norm_mod(x, g, sc, sh):
    ms = jnp.mean(x * x, axis=-1, keepdims=True)
    y = x * lax.rsqrt(ms + NORM_EPS) * g
    return y * (1.0 + sc) + sh


def _seq_blocks(seqs, length, tile=TOKEN_TILE):
    if length >= tile:
        assert length % tile == 0
        return 1, tile
    assert tile % length == 0 and seqs % (tile // length) == 0 and length % 16 == 0
    return tile // length, length


def _ada_kernel(c_ref, w_ref, b_ref, o_ref):
    c = c_ref[...]
    s = c * _sigmoid(c)
    o_ref[0] = jnp.dot(s, w_ref[0], precision=lax.Precision.HIGHEST,
                       preferred_element_type=F32) + b_ref[0]


def _ada(c, ada_w, ada_b):
    rows = c.shape[0]
    depth = ada_w.shape[0]
    return pl.pallas_call(
        _ada_kernel,
        out_shape=jax.ShapeDtypeStruct((depth, rows, 6 * D_MODEL), F32),
        grid=(depth, 6),
        in_specs=[pl.BlockSpec((rows, D_MODEL), lambda l, j: (0, 0)),
                  pl.BlockSpec((1, D_MODEL, D_MODEL), lambda l, j: (l, 0, j)),
                  pl.BlockSpec((1, 1, D_MODEL), lambda l, j: (l, 0, j))],
        out_specs=pl.BlockSpec((1, rows, D_MODEL), lambda l, j: (l, 0, j)),
        compiler_params=_cparams("arbitrary", "arbitrary"),
        name="ada_mod",
    )(c, ada_w, ada_b.reshape(depth, 1, 6 * D_MODEL))


def _dot3(x, w_hi, w_lo):
    hi, lo = _split2(x)
    rows = x.shape[0]
    both = _dot(jnp.concatenate([hi, lo], axis=0), w_hi)
    return both[:rows] + both[rows:] + _dot(hi, w_lo)


def _split_weight(w):
    hi = w.astype(BF16)
    return hi, (w - hi.astype(F32)).astype(BF16)


def _qkv_kernel(x_ref, g_ref, sc_ref, sh_ref, wh_ref, wl_ref, k_ref, v_ref, *piece_refs):
    bs, tl, _ = x_ref.shape
    h = _norm_mod(x_ref[...], g_ref[...], sc_ref[...], sh_ref[...]).reshape(bs * tl, D_MODEL)
    qkv = _dot3(h, wh_ref[...], wl_ref[...])
    q = qkv[:, :D_MODEL] * (HEAD_DIM ** -0.5)
    k = qkv[:, D_MODEL:2 * D_MODEL]
    v = qkv[:, 2 * D_MODEL:]
    k_ref[...] = k.reshape(bs, tl, D_MODEL)
    v_ref[...] = v.reshape(bs, tl, D_MODEL)
    pieces = _split2(q) + _split2(k) + _split2(v)
    for ref, piece in zip(piece_refs, pieces):
        ref[...] = piece.reshape(bs, tl, D_MODEL)


def _qkv(x, g, sc, sh, w_hi, w_lo):
    seqs, length, _ = x.shape
    bs, tl = _seq_blocks(seqs, length, QKV_TILE)
    tok = pl.BlockSpec((bs, tl, D_MODEL), lambda s, i: (s, i, 0))
    mod = pl.BlockSpec((bs, 1, D_MODEL), lambda s, i: (s, 0, 0))
    wspec = pl.BlockSpec((D_MODEL, 3 * D_MODEL), lambda s, i: (0, 0))
    f32o = jax.ShapeDtypeStruct((seqs, length, D_MODEL), F32)
    bfo = jax.ShapeDtypeStruct((seqs, length, D_MODEL), BF16)
    return pl.pallas_call(
        _qkv_kernel,
        out_shape=(f32o, f32o) + (bfo,) * 6,
        grid=(seqs // bs, length // tl),
        in_specs=[tok, pl.BlockSpec((1, 1, D_MODEL), lambda s, i: (0, 0, 0)), mod, mod, wspec, wspec],
        out_specs=(tok,) * 8,
        compiler_params=_cparams("arbitrary", "arbitrary"),
        name="sb_qkv",
    )(x, g.reshape(1, 1, D_MODEL), sc, sh, w_hi, w_lo)


def _pair_masks(rows):
    lane = lax.broadcasted_iota(jnp.int32, (rows, LANES), 1)
    return lane < HEAD_DIM


def _suffix_matrix():
    r = lax.broadcasted_iota(jnp.int32, (LANES, LANES), 0)
    c = lax.broadcasted_iota(jnp.int32, (LANES, LANES), 1)
    u = jnp.where(r > c, 1.0, 0.0).astype(BF16)
    return jnp.concatenate([u, u], axis=0)


def _query_pieces(q_hi, q_lo):
    hi, lo = _stack_heads(q_hi), _stack_heads(q_lo)
    return jnp.concatenate([hi, lo, hi], axis=1)


def _key_pieces(k_hi, k_lo):
    return jnp.concatenate([k_hi, k_hi, k_lo], axis=1)


def _sb_sweep(q3, keys, values, state, u2, visible):
    n = len(q3)
    rows = q3[0].shape[0]
    tq = rows // 2
    wide = lambda x: jnp.concatenate([x[:tq], x[tq:]], axis=1)
    z_all = lax.dot_general(jnp.stack(q3), jnp.stack([_key_pieces(*keys[p]) for p in range(n)]),
                            _BATCHED_NT, preferred_element_type=F32)
    zs, sps = [], []
    for p in range(n):
        z = z_all[p]
        sp = jnp.maximum(z, 0.0) + jnp.log(1.0 + jnp.exp(-jnp.abs(z)))
        if visible is not None:
            sp = jnp.where(visible, sp, 0.0)
        zs.append(z)
        sps.append(sp)
    later_all = _dot2(jnp.concatenate(sps, axis=0), u2)
    weights, vals = [], []
    for p in range(n):
        later = later_all[p * rows:(p + 1) * rows]
        w = jnp.exp(zs[p] - sps[p] - later - state[p][0])
        if visible is not None:
            w = jnp.where(visible, w, 0.0)
        w_hi, w_lo = _split2(w)
        v_hi, v_lo = _stack_heads(values[p][0]), _stack_heads(values[p][1])
        weights.append(jnp.concatenate([wide(w_hi), wide(w_lo), wide(w_hi)], axis=1))
        vals.append(jnp.concatenate([v_hi, v_hi, v_lo], axis=0))
    gained = lax.dot_general(jnp.stack(weights), jnp.stack(vals), _BATCHED,
                             preferred_element_type=F32)
    return tuple((state[p][0] + jnp.sum(sps[p], axis=-1, keepdims=True), state[p][1] + gained[p])
                 for p in range(n))


def _stack_heads(x, axis=0):
    mask_a = _pair_masks(x.shape[0])
    return jnp.concatenate([jnp.where(mask_a, x, 0), jnp.where(mask_a, 0, x)], axis=axis)


def _diag_visible(tq):
    row = lax.broadcasted_iota(jnp.int32, (2 * tq, LANES), 0)
    col = lax.broadcasted_iota(jnp.int32, (2 * tq, LANES), 1)
    qpos = jnp.where(row >= tq, row - tq, row)
    return col < qpos


def _pair_lanes(p):
    return slice(p * LANES, (p + 1) * LANES)


def _sweep_earlier_tiles(n_tiles, sweep, state):
    def cond(c):
        return jnp.logical_and(c[0] < n_tiles, jnp.logical_not(c[1]))

    def body(c):
        st = sweep(n_tiles - 1 - c[0], c[2])
        low = st[0][0]
        for p in range(1, len(st)):
            low = jnp.minimum(low, st[p][0])
        return c[0] + 1, jnp.min(low) > ZERO_WEIGHT_CARRY, st

    return lax.while_loop(cond, body, (jnp.int32(0), False, state))[2]


def _attn_prompt_kernel(qh_ref, ql_ref, kh_ref, kl_ref, vh_ref, vl_ref, o_ref):
    tq = qh_ref.shape[1]
    qi = pl.program_id(1)
    u2 = _suffix_matrix()
    q3 = [_query_pieces(qh_ref[0, :, _pair_lanes(p)], ql_ref[0, :, _pair_lanes(p)])
          for p in range(PAIRS)]

    def sweep(start, state, visible):
        rows = pl.ds(pl.multiple_of(start, tq), tq)
        tile = lambda ref, p: ref[0, rows, _pair_lanes(p)]
        return _sb_sweep(q3, [(tile(kh_ref, p), tile(kl_ref, p)) for p in range(PAIRS)],
                         [(tile(vh_ref, p), tile(vl_ref, p)) for p in range(PAIRS)],
                         state, u2, visible)

    zero = (jnp.zeros((2 * tq, LANES), F32), jnp.zeros((tq, LANES), F32))
    state = sweep(qi * tq, (zero,) * PAIRS, _diag_visible(tq))
    state = _sweep_earlier_tiles(qi, lambda j, st: sweep(j * tq, st, None), state)
    for p in range(PAIRS):
        o_ref[0, :, _pair_lanes(p)] = state[p][1]


def _attn_prompt(q, k, v):
    batch, length, _ = q[0].shape
    tq = LANES
    qspec = pl.BlockSpec((1, tq, D_MODEL), lambda b, i: (b, i, 0))
    kspec = pl.BlockSpec((1, length, D_MODEL), lambda b, i: (b, 0, 0), pipeline_mode=pl.Buffered(1))
    return pl.pallas_call(
        _attn_prompt_kernel,
        out_shape=jax.ShapeDtypeStruct((batch, length, D_MODEL), F32),
        grid=(batch, length // tq),
        in_specs=[qspec, qspec, kspec, kspec, kspec, kspec],
        out_specs=qspec,
        compiler_params=_cparams("arbitrary", "arbitrary"),
        name="sb_attn_prompt",
    )(*q, *k, *v)


def _attn_sample_kernel(qh_ref, ql_ref, kh_ref, kl_ref, vh_ref, vl_ref, kc_ref, vc_ref, o_ref):
    tq = qh_ref.shape[1]
    past = kc_ref.shape[1]
    u2 = _suffix_matrix()
    pad = jnp.zeros((LANES - tq, LANES), BF16)
    padded = lambda ref, p: jnp.concatenate([ref[0, :, _pair_lanes(p)], pad], axis=0)
    q3 = [_query_pieces(qh_ref[0, :, _pair_lanes(p)], ql_ref[0, :, _pair_lanes(p)])
          for p in range(PAIRS)]
    zero = (jnp.zeros((2 * tq, LANES), F32), jnp.zeros((tq, LANES), F32))
    state = _sb_sweep(q3, [(padded(kh_ref, p), padded(kl_ref, p)) for p in range(PAIRS)],
                      [(padded(vh_ref, p), padded(vl_ref, p)) for p in range(PAIRS)],
                      (zero,) * PAIRS, u2, _diag_visible(tq))

    def sweep(j, st):
        rows = pl.ds(pl.multiple_of(j * LANES, LANES), LANES)
        return _sb_sweep(q3, [_split2(kc_ref[0, rows, _pair_lanes(p)]) for p in range(PAIRS)],
                         [_split2(vc_ref[0, rows, _pair_lanes(p)]) for p in range(PAIRS)],
                         st, u2, None)

    state = _sweep_earlier_tiles(past // LANES, sweep, state)
    for p in range(PAIRS):
        o_ref[0, :, _pair_lanes(p)] = state[p][1]


def _attn_sample(q, k, v, k_cache, v_cache):
    batch, length, _ = q[0].shape
    past = k_cache.shape[1]
    assert length <= LANES and past % LANES == 0
    qspec = pl.BlockSpec((1, length, D_MODEL), lambda b: (b, 0, 0))
    cspec = pl.BlockSpec((1, past, D_MODEL), lambda b: (b, 0, 0))
    return pl.pallas_call(
        _attn_sample_kernel,
        out_shape=jax.ShapeDtypeStruct((batch, length, D_MODEL), F32),
        grid=(batch,),
        in_specs=[qspec] * 6 + [cspec, cspec],
        out_specs=qspec,
        compiler_params=_cparams("arbitrary"),
        name="sb_attn_sample",
    )(*q, *k, *v, k_cache, v_cache)


def _row_max(rows):
    m = rows[0]
    for r in rows[1:]:
        m = jnp.maximum(m, r)
    return m


def _first_hit(rows, m):
    hits = []
    taken = None
    for r in rows:
        eq = r == m
        hit = eq if taken is None else jnp.logical_and(eq, jnp.logical_not(taken))
        taken = eq if taken is None else jnp.logical_or(taken, eq)
        hits.append(hit)
    return hits


def _route(s_rows, sel_rows):
    neg = jnp.float32(-jnp.inf)
    top1, top2, score = [], [], []
    for g in range(N_GROUPS):
        rows = sel_rows[g * EXPERTS_PER_GROUP:(g + 1) * EXPERTS_PER_GROUP]
        m1 = _row_max(rows)
        h1 = _first_hit(rows, m1)
        rest = [jnp.where(h, neg, r) for h, r in zip(h1, rows)]
        m2 = _row_max(rest)
        h2 = _first_hit(rest, m2)
        top1.append(h1)
        top2.append(h2)
        score.append(m1 + m2)
    best = _row_max(score)
    ghit = _first_hit(score, best)
    picked = []
    for g in range(N_GROUPS):
        for e in range(EXPERTS_PER_GROUP):
            picked.append(jnp.logical_and(ghit[g], jnp.logical_or(top1[g][e], top2[g][e])))
    total = None
    for p, s in zip(picked, s_rows):
        term = jnp.where(p, s, 0.0)
        total = term if total is None else total + term
    gates = [jnp.where(p, s / total, 0.0) for p, s in zip(picked, s_rows)]
    in_group = []
    for e in range(EXPERTS_PER_GROUP):
        row = gates[e]
        for g in range(1, N_GROUPS):
            row = row + gates[g * EXPERTS_PER_GROUP + e]
        in_group.append(row)
    group = jnp.zeros_like(best)
    for g in range(1, N_GROUPS):
        group = jnp.where(ghit[g], float(g), group)
    return in_group, group


def _oproj_kernel(m_ref, x_ref, gate_ref, wh_ref, wl_ref, g_ref, sc_ref, sh_ref, rw_ref, rb_ref,
                  x1_ref, h_ref, route_ref):
    bs, tl, _ = x_ref.shape
    m = m_ref[...].reshape(bs * tl, D_MODEL)
    mix = _dot3(m, wh_ref[...], wl_ref[...]).reshape(bs, tl, D_MODEL)
    x1 = x_ref[...] + gate_ref[...] * mix
    x1_ref[...] = x1
    h = _norm_mod(x1, g_ref[...], sc_ref[...], sh_ref[...])
    h_ref[...] = h
    logits = lax.dot_general(rw_ref[...], h.reshape(bs * tl, D_MODEL), _NT,
                             precision=lax.Precision.HIGHEST,
                             preferred_element_type=F32)
    s = _sigmoid(logits)
    sel = s + rb_ref[...]
    s_rows = [s[e:e + 1, :] for e in range(N_EXPERTS)]
    sel_rows = [sel[e:e + 1, :] for e in range(N_EXPERTS)]
    in_group, group = _route(s_rows, sel_rows)
    pad = [jnp.zeros_like(group)] * (SUBLANES - EXPERTS_PER_GROUP - 1)
    route_ref[0] = jnp.concatenate(in_group + [group] + pad, axis=0)


def _oproj_router(m, x, gate, w_pieces, g, sc, sh, router_wt, router_b):
    seqs, length, _ = x.shape
    bs, tl = _seq_blocks(seqs, length)
    n_tiles = (seqs // bs) * (length // tl)
    per_seq = length // tl
    tok = pl.BlockSpec((bs, tl, D_MODEL), lambda s, i: (s, i, 0))
    mod = pl.BlockSpec((bs, 1, D_MODEL), lambda s, i: (s, 0, 0))
    one = pl.BlockSpec((1, 1, D_MODEL), lambda s, i: (0, 0, 0))
    wspec = pl.BlockSpec((D_MODEL, D_MODEL), lambda s, i: (0, 0))
    f32o = jax.ShapeDtypeStruct((seqs, length, D_MODEL), F32)
    x1, h, route = pl.pallas_call(
        _oproj_kernel,
        out_shape=(f32o, f32o, jax.ShapeDtypeStruct((n_tiles, SUBLANES, bs * tl), F32)),
        grid=(seqs // bs, per_seq),
        in_specs=[tok, tok, mod, wspec, wspec, one, mod, mod,
                  pl.BlockSpec((N_EXPERTS, D_MODEL), lambda s, i: (0, 0)),
                  pl.BlockSpec((N_EXPERTS, 1), lambda s, i: (0, 0))],
        out_specs=(tok, tok,
                   pl.BlockSpec((1, SUBLANES, bs * tl), lambda s, i: (s * per_seq + i, 0, 0))),
        compiler_params=_cparams("arbitrary", "arbitrary"),
        name="oproj_router",
    )(m, x, gate, *w_pieces, g.reshape(1, 1, D_MODEL), sc, sh, router_wt,
      router_b.reshape(N_EXPERTS, 1))
    route = jnp.swapaxes(route, 1, 2).reshape(seqs * length, SUBLANES)
    return x1, h, route[:, :EXPERTS_PER_GROUP], route[:, EXPERTS_PER_GROUP].astype(jnp.int32)


def _dispatch_plan(group, gates, tile):
    n = group.shape[0]
    n_slots = n + N_GROUPS * tile
    onehot = (group[:, None] == jnp.arange(N_GROUPS)[None, :]).astype(jnp.int32)
    counts = jnp.sum(onehot, axis=0)
    padded = (counts + tile - 1) // tile * tile
    ends = jnp.cumsum(padded)
    starts = ends - padded
    rank = jnp.take_along_axis(jnp.cumsum(onehot, axis=0) - onehot, group[:, None], axis=1)[:, 0]
    slot = starts[group] + rank
    src = jnp.zeros((n_slots,), jnp.int32).at[slot].set(jnp.arange(n, dtype=jnp.int32))
    slot_gates = jnp.zeros((n_slots, EXPERTS_PER_GROUP), F32).at[slot].set(gates)
    tile_start = jnp.arange(n_slots // tile, dtype=jnp.int32) * tile
    tile_group = jnp.minimum(jnp.sum(tile_start[:, None] >= ends[None, :], axis=1), N_GROUPS - 1)
    return src, slot.astype(jnp.int32), slot_gates, tile_group.astype(jnp.int32)


def _start_row_gather(idx_ref, base, src_hbm, dst_ref, sem):
    def body(r, _):
        row = idx_ref[base + r]
        pltpu.make_async_copy(src_hbm.at[pl.ds(row, 1), :], dst_ref.at[pl.ds(r, 1), :], sem).start()
        return 0

    lax.fori_loop(0, dst_ref.shape[0], body, 0, unroll=8)


def _wait_row_gather(src_hbm, dst_ref, sem):
    pltpu.make_async_copy(src_hbm.at[pl.ds(0, dst_ref.shape[0]), :], dst_ref, sem).wait()


def _moe_kernel(src_ref, tg_ref, h_hbm, gates_ref, wg_ref, wu_ref, wd_ref, o_ref, hbuf, sem):
    t = pl.program_id(0)
    last = pl.num_programs(0) - 1
    tile = o_ref.shape[0]
    slot = t % 2

    @pl.when(t == 0)
    def _():
        _start_row_gather(src_ref, 0, h_hbm, hbuf.at[0], sem.at[0])

    _wait_row_gather(h_hbm, hbuf.at[slot], sem.at[slot])
    h = hbuf[slot].astype(BF16)
    gates = gates_ref[...]
    acc = jnp.zeros((tile, D_MODEL), F32)
    ahead = jnp.minimum(t + 1, last) * tile
    share = tile // EXPERTS_PER_GROUP
    for e in range(EXPERTS_PER_GROUP):
        for r in range(e * share, (e + 1) * share):
            pltpu.make_async_copy(h_hbm.at[pl.ds(src_ref[ahead + r], 1), :],
                                  hbuf.at[1 - slot, pl.ds(r, 1), :], sem.at[1 - slot]).start()
        a = _dot(h, wg_ref[e])
        u = _dot(h, wu_ref[e])
        he = (a * _sigmoid(a) * u).astype(BF16)
        acc = acc + gates[:, e:e + 1] * _dot(he, wd_ref[e])
    o_ref[...] = acc

    @pl.when(t == last)
    def _():
        _wait_row_gather(h_hbm, hbuf.at[1 - slot], sem.at[1 - slot])


def _moe_experts(h, src, slot_gates, tile_group, wg_bf, wu_bf, wd_bf, tile):
    n_slots = src.shape[0]
    group_w = lambda shape: pl.BlockSpec((EXPERTS_PER_GROUP,) + shape, lambda t, src, tg: (tg[t], 0, 0))
    return pl.pallas_call(
        _moe_kernel,
        out_shape=jax.ShapeDtypeStruct((n_slots, D_MODEL), F32),
        grid_spec=pltpu.PrefetchScalarGridSpec(
            num_scalar_prefetch=2,
            grid=(n_slots // tile,),
            in_specs=[pl.BlockSpec(memory_space=pl.ANY),
                      pl.BlockSpec((tile, EXPERTS_PER_GROUP), lambda t, src, tg: (t, 0)),
                      group_w((D_MODEL, D_EXPERT)), group_w((D_MODEL, D_EXPERT)),
                      group_w((D_EXPERT, D_MODEL))],
            out_specs=pl.BlockSpec((tile, D_MODEL), lambda t, src, tg: (t, 0)),
            scratch_shapes=[pltpu.VMEM((2, tile, D_MODEL), F32), pltpu.SemaphoreType.DMA((2,))]),
        compiler_params=_cparams("arbitrary"),
        name="moe_experts",
    )(src, tile_group, h, slot_gates, wg_bf, wu_bf, wd_bf)


def _moe_combine_kernel(slot_ref, f_hbm, x_ref, gate_ref, fg_ref, o_ref, fbuf, sem, *, final_norm):
    bs, tl, _ = x_ref.shape
    tile = bs * tl
    t = pl.program_id(0) * pl.num_programs(1) + pl.program_id(1)
    n_tiles = pl.num_programs(0) * pl.num_programs(1)
    slot = t % 2

    @pl.when(t == 0)
    def _():
        _start_row_gather(slot_ref, 0, f_hbm, fbuf.at[0], sem.at[0])

    @pl.when(t + 1 < n_tiles)
    def _():
        _start_row_gather(slot_ref, (t + 1) * tile, f_hbm, fbuf.at[1 - slot], sem.at[1 - slot])

    _wait_row_gather(f_hbm, fbuf.at[slot], sem.at[slot])
    x2 = x_ref[...] + gate_ref[...] * fbuf[slot].reshape(bs, tl, D_MODEL)
    if final_norm:
        ms = jnp.mean(x2 * x2, axis=-1, keepdims=True)
        x2 = x2 * lax.rsqrt(ms + NORM_EPS) * fg_ref[...]
    o_ref[...] = x2


def _moe_combine(ffn_sorted, slot, x, gate, final_g, final_norm):
    seqs, length, _ = x.shape
    bs, tl = _seq_blocks(seqs, length)
    tok = pl.BlockSpec((bs, tl, D_MODEL), lambda s, i, slot: (s, i, 0))
    return pl.pallas_call(
        functools.partial(_moe_combine_kernel, final_norm=final_norm),
        out_shape=jax.ShapeDtypeStruct((seqs, length, D_MODEL), F32),
        grid_spec=pltpu.PrefetchScalarGridSpec(
            num_scalar_prefetch=1,
            grid=(seqs // bs, length // tl),
            in_specs=[pl.BlockSpec(memory_space=pl.ANY), tok,
                      pl.BlockSpec((bs, 1, D_MODEL), lambda s, i, slot: (s, 0, 0)),
                      pl.BlockSpec((1, 1, D_MODEL), lambda s, i, slot: (0, 0, 0))],
            out_specs=tok,
            scratch_shapes=[pltpu.VMEM((2, bs * tl, D_MODEL), F32), pltpu.SemaphoreType.DMA((2,))]),
        compiler_params=_cparams("arbitrary", "arbitrary"),
        name="moe_combine",
    )(slot, ffn_sorted, x, gate, final_g.reshape(1, 1, D_MODEL))


def _moe(h, group, gates, x, gate, final_g, wg_bf, wu_bf, wd_bf, final_norm):
    seqs, length, _ = x.shape
    tile = min(MOE_TILE, max(LANES, seqs * length // (2 * N_GROUPS)))
    src, slot, slot_gates, tile_group = _dispatch_plan(group, gates, tile)
    ffn_sorted = _moe_experts(h.reshape(seqs * length, D_MODEL), src, slot_gates, tile_group,
                              wg_bf, wu_bf, wd_bf, tile)
    return _moe_combine(ffn_sorted, slot, x, gate, final_g, final_norm)


def _head_selectors():
    r = lax.broadcasted_iota(jnp.int32, (D_MODEL, LANES), 0)
    c = lax.broadcasted_iota(jnp.int32, (D_MODEL, LANES), 1)
    sel = jnp.where(r // HEAD_DIM == c, 1.0, 0.0).astype(BF16)
    rt = lax.broadcasted_iota(jnp.int32, (LANES, D_MODEL), 0)
    ct = lax.broadcasted_iota(jnp.int32, (LANES, D_MODEL), 1)
    selt = jnp.where(ct // HEAD_DIM == rt, 1.0, 0.0).astype(BF16)
    return jnp.concatenate([sel, sel], axis=0), jnp.concatenate([selt, selt], axis=0)


def _head_sum(x, sel2, selt2):
    return _dot2(_dot2(x, sel2), selt2)


def _rwkv_pre_kernel(x_ref, prev_ref, shift_ref, g_ref, sc_ref, sh_ref, mu_ref,
                     wr_ref, wk_ref, wv_ref, w0_ref, w1_ref, w2_ref, a0_ref, a1_ref, a2_ref,
                     g1_ref, g2_ref, kk_ref, ka_ref,
                     r_ref, w_ref, k_ref, v_ref, na_ref, kb_ref, go_ref, so_ref):
    bs, tl, _ = x_ref.shape
    t = bs * tl
    i = pl.program_id(1)
    g, sc, sh = g_ref[...], sc_ref[...], sh_ref[...]
    h = _norm_mod(x_ref[...], g, sc, sh)
    so_ref[...] = h[:, tl - 1:tl, :]
    h_before = _norm_mod(prev_ref[:, 7:8, :], g, sc, sh)
    first = jnp.where(i == 0, shift_ref[...], h_before)
    row = lax.broadcasted_iota(jnp.int32, (bs, tl, D_MODEL), 1)
    prev = jnp.where(row == 0, first, pltpu.roll(h, 1, 1))
    h = h.reshape(t, D_MODEL)
    xx = prev.reshape(t, D_MODEL) - h
    mixed = [(h + xx * mu_ref[j:j + 1, :]).astype(BF16) for j in range(6)]
    xr, xw, xk, xv, xa, xg = mixed
    r = _dot(xr, wr_ref[...])
    k = _dot(xk, wk_ref[...])
    v = _dot(xv, wv_ref[...])
    wl = w0_ref[...] + _dot(jnp.tanh(_dot(xw, w1_ref[...])).astype(BF16), w2_ref[...])
    neg = -wl
    softplus = jnp.maximum(neg, 0.0) + jnp.log(1.0 + jnp.exp(-jnp.abs(neg)))
    w = -softplus - 0.5
    log_decay = -jnp.exp(w)
    a = _sigmoid(a0_ref[...] + _dot(_dot(xa, a1_ref[...]).astype(BF16), a2_ref[...]))
    gate = _dot(_sigmoid(_dot(xg, g1_ref[...])).astype(BF16), g2_ref[...])
    sel2, selt2 = _head_selectors()
    kk = k * kk_ref[...]
    norm = jnp.maximum(jnp.sqrt(_head_sum(kk * kk, sel2, selt2)), 1e-12)
    kk = kk / norm
    k = k * (1.0 + (a - 1.0) * ka_ref[...])
    shape = (bs, tl, D_MODEL)
    r_ref[...] = r.reshape(shape)
    w_ref[...] = log_decay.reshape(shape)
    k_ref[...] = k.reshape(shape)
    v_ref[...] = v.reshape(shape)
    na_ref[...] = (-kk).reshape(shape)
    kb_ref[...] = (kk * a).reshape(shape)
    go_ref[...] = gate.reshape(shape)


def _rwkv_pre(x, shift0, g, sc, sh, p):
    seqs, length, _ = x.shape
    bs, tl = _seq_blocks(seqs, length, RWKV_PRE_TILE)
    tok = pl.BlockSpec((bs, tl, D_MODEL), lambda s, i: (s, i, 0))
    prev = pl.BlockSpec((bs, 8, D_MODEL), lambda s, i: (s, jnp.maximum(i * (tl // 8) - 1, 0), 0))
    mod = pl.BlockSpec((bs, 1, D_MODEL), lambda s, i: (s, 0, 0))
    one = pl.BlockSpec((1, 1, D_MODEL), lambda s, i: (0, 0, 0))

    def full(a):
        return pl.BlockSpec(a.shape, lambda s, i: (0,) * a.ndim)

    vec = lambda a: a.reshape(1, D_MODEL)
    weights = [p['rw_mu'], p['rw_w_r'], p['rw_w_k'], p['rw_w_v'], vec(p['rw_w0']), p['rw_w1'],
               p['rw_w2'], vec(p['rw_a0']), p['rw_a1'], p['rw_a2'], p['rw_g1'], p['rw_g2'],
               vec(p['rw_k_k']), vec(p['rw_k_a'])]
    f32o = jax.ShapeDtypeStruct((seqs, length, D_MODEL), F32)
    outs = pl.pallas_call(
        _rwkv_pre_kernel,
        out_shape=(f32o,) * 7 + (jax.ShapeDtypeStruct((seqs, 1, D_MODEL), F32),),
        grid=(seqs // bs, length // tl),
        in_specs=[tok, prev, mod, one, mod, mod] + [full(a) for a in weights],
        out_specs=(tok,) * 7 + (mod,),
        compiler_params=_cparams("arbitrary", "arbitrary"),
        name="rwkv_pre",
    )(x, x, shift0, g.reshape(1, 1, D_MODEL), sc, sh, *weights)
    return outs


def _bdot(a, b, dims=None):
    a, b = a.astype(BF16), b.astype(BF16)
    if dims is None:
        return jnp.dot(a, b, preferred_element_type=F32)
    return lax.dot_general(a, b, dims, preferred_element_type=F32)


def _scan_kernel(r_ref, lw_ref, k_ref, v_ref, a_ref, b_ref, s0_ref, y_ref, st_ref, z_ref, *, chunk):
    tc = r_ref.shape[1]
    t2 = 2 * chunk
    c = pl.program_id(1)
    lane_a = _pair_masks(chunk)
    fold = lambda x: jnp.where(lane_a, x[:chunk], x[chunk:])
    stack = lambda x: jnp.concatenate([x, x], axis=0)
    split_heads = lambda x: _stack_heads(x, axis=0)

    def block_diag(mats):
        n = mats[0].shape[0]
        zero = jnp.zeros((n, n), F32)
        return jnp.concatenate(
            [jnp.concatenate([m if j == i else zero for j in range(len(mats))], axis=1)
             for i, m in enumerate(mats)], axis=0)

    @pl.when(c == 0)
    def _():
        for q in range(PAIRS // SCAN_GROUP):
            z_ref[q] = block_diag([s0_ref[0, 2 * SCAN_GROUP * q + h] for h in range(2 * SCAN_GROUP)])

    row = lax.broadcasted_iota(jnp.int32, (t2, t2), 0)
    col = lax.broadcasted_iota(jnp.int32, (t2, t2), 1)
    same_head = (row // chunk) == (col // chunk)
    strict = jnp.logical_and(same_head, (row % chunk) > (col % chunk))
    incl = jnp.logical_and(same_head, (row % chunk) >= (col % chunk))
    tr = lax.broadcasted_iota(jnp.int32, (chunk, chunk), 0)
    tcol = lax.broadcasted_iota(jnp.int32, (chunk, chunk), 1)
    tri = jnp.where(tr >= tcol, 1.0, 0.0).astype(BF16)
    tri2 = jnp.concatenate([tri, tri], axis=1)
    gw = SCAN_GROUP * LANES
    zr = lax.broadcasted_iota(jnp.int32, (gw, gw), 0)
    zc = lax.broadcasted_iota(jnp.int32, (gw, gw), 1)
    z_diag = (zr // HEAD_DIM) == (zc // HEAD_DIM)
    n_doublings = chunk.bit_length() - 1

    def one_chunk(ci, _):
        rows = pl.ds(pl.multiple_of(ci * chunk, chunk), chunk)
        lw, r, k, v, a, b = [ref[0, rows, :] for ref in (lw_ref, r_ref, k_ref, v_ref, a_ref, b_ref)]
        hi, lo = _split2(lw)
        cum = _dot(tri2, jnp.concatenate([hi, lo], axis=0))
        grow = jnp.exp(-cum)
        decay = jnp.exp(cum)
        total = decay[chunk - 1:chunk, :]
        a_t = a * jnp.exp(cum - lw)
        r_t = r * decay
        b_t = b * grow
        k_t = k * grow
        b_end = b_t * total
        k_end = k_t * total
        n_groups = PAIRS // SCAN_GROUP
        pair = lambda x, p: x[:, _pair_lanes(p)]
        grp = lambda x, q: x[:, q * gw:(q + 1) * gw]
        members = lambda q: range(SCAN_GROUP * q, SCAN_GROUP * (q + 1))
        per_group = lambda f: jnp.stack([f(q) for q in range(n_groups)])
        g = _bdot(jnp.stack([jnp.concatenate([split_heads(pair(a_t, p)), split_heads(pair(r_t, p))], axis=0)
                             for p in range(PAIRS)]),
                  jnp.stack([jnp.concatenate([stack(pair(b_t, p)), stack(pair(k_t, p))], axis=0)
                             for p in range(PAIRS)]), _BATCHED_NT)
        l_ab = jnp.where(strict, g[:, :t2, :t2], 0.0)
        l_ak = jnp.where(strict, g[:, :t2, t2:], 0.0)
        l_rb = jnp.where(incl, g[:, t2:, :t2], 0.0)
        l_rk = jnp.where(incl, g[:, t2:, t2:], 0.0)
        diag = lambda m: per_group(lambda q: block_diag([m[p] for p in members(q)]))
        z = z_ref[...]
        applied = _bdot(per_group(lambda q: jnp.concatenate([grp(a_t, q), grp(r_t, q)], axis=0)),
                        z, _BATCHED_NT)
        vv = per_group(lambda q: jnp.concatenate([stack(pair(v, p)) for p in members(q)], axis=0))
        x = per_group(lambda q: jnp.concatenate(
            [stack(applied[q, :chunk, j * LANES:(j + 1) * LANES]) for j in range(SCAN_GROUP)],
            axis=0)) + _bdot(diag(l_ak), vv, _BATCHED)
        power = diag(l_ab)
        for i in range(n_doublings):
            x = x + _bdot(power, x, _BATCHED)
            if i + 1 < n_doublings:
                power = _bdot(power, power, _BATCHED)
        unstack = lambda m, q: jnp.concatenate(
            [fold(m[q, j * t2:(j + 1) * t2]) for j in range(SCAN_GROUP)], axis=1)
        u = [unstack(x, q) for q in range(n_groups)]
        uu = per_group(lambda q: jnp.concatenate(
            [stack(u[q][:, j * LANES:(j + 1) * LANES]) for j in range(SCAN_GROUP)], axis=0))
        ys = _bdot(jnp.concatenate([diag(l_rb), diag(l_rk)], axis=2),
                   jnp.concatenate([uu, vv], axis=1), _BATCHED)
        y_ref[0, rows, :] = jnp.concatenate(
            [applied[q, chunk:] + unstack(ys, q) for q in range(n_groups)], axis=1)
        grown = _bdot(per_group(lambda q: jnp.concatenate([u[q], grp(v, q)], axis=0)),
                      per_group(lambda q: jnp.concatenate([grp(b_end, q), grp(k_end, q)], axis=0)),
                      _BATCHED_TN)
        z_ref[...] = (z * per_group(lambda q: grp(total, q)) + jnp.where(z_diag, grown, 0.0))
        return 0

    lax.fori_loop(0, tc // chunk, one_chunk, 0)

    @pl.when(c == pl.num_programs(1) - 1)
    def _():
        for q in range(PAIRS // SCAN_GROUP):
            z = z_ref[q]
            for h in range(2 * SCAN_GROUP):
                block = slice(h * HEAD_DIM, (h + 1) * HEAD_DIM)
                st_ref[0, 2 * SCAN_GROUP * q + h] = z[block, block]


def _scan(r, lw, k, v, na, kb, s0):
    seqs, length, _ = r.shape
    chunk = min(length, SCAN_CHUNK)
    tc = min(length, SCAN_BLOCK)
    assert length % tc == 0 and tc % chunk == 0 and chunk & (chunk - 1) == 0
    tok = pl.BlockSpec((1, tc, D_MODEL), lambda s, c: (s, c, 0))
    st = pl.BlockSpec((1, HEADS, HEAD_DIM, HEAD_DIM), lambda s, c: (s, 0, 0, 0))
    return pl.pallas_call(
        functools.partial(_scan_kernel, chunk=chunk),
        out_shape=(jax.ShapeDtypeStruct((seqs, length, D_MODEL), F32),
                   jax.ShapeDtypeStruct((seqs, HEADS, HEAD_DIM, HEAD_DIM), F32)),
        grid=(seqs, length // tc),
        in_specs=[tok] * 6 + [st],
        out_specs=(tok, st),
        scratch_shapes=[pltpu.VMEM((PAIRS // SCAN_GROUP, SCAN_GROUP * LANES, SCAN_GROUP * LANES), F32)],
        compiler_params=_cparams("arbitrary", "arbitrary"),
        name="rwkv_scan",
    )(r, lw, k, v, na, kb, s0)


def _rwkv_post_kernel(y_ref, r_ref, k_ref, v_ref, g_ref, lg_ref, lb_ref, rk_ref, o_ref):
    bs, tl, _ = y_ref.shape
    t = bs * tl
    sel2, selt2 = _head_selectors()
    y = y_ref[...].reshape(t, D_MODEL)
    mu = _head_sum(y, sel2, selt2) * (1.0 / HEAD_DIM)
    d = y - mu
    var = _head_sum(d * d, sel2, selt2) * (1.0 / HEAD_DIM)
    yn = d * lax.rsqrt(var + LNX_EPS) * lg_ref[...] + lb_ref[...]
    rk = r_ref[...].reshape(t, D_MODEL) * k_ref[...].reshape(t, D_MODEL) * rk_ref[...]
    bonus = _head_sum(rk, sel2, selt2) * v_ref[...].reshape(t, D_MODEL)
    o_ref[...] = ((yn + bonus) * g_ref[...].reshape(t, D_MODEL)).reshape(bs, tl, D_MODEL)


def _rwkv_post(y, r, k, v, gate, lnx_g, lnx_b, r_k):
    seqs, length, _ = y.shape
    bs, tl = _seq_blocks(seqs, length)
    tok = pl.BlockSpec((bs, tl, D_MODEL), lambda s, i: (s, i, 0))
    vec = pl.BlockSpec((1, D_MODEL), lambda s, i: (0, 0))
    return pl.pallas_call(
        _rwkv_post_kernel,
        out_shape=jax.ShapeDtypeStruct((seqs, length, D_MODEL), F32),
        grid=(seqs // bs, length // tl),
        in_specs=[tok] * 5 + [vec] * 3,
        out_specs=tok,
        compiler_params=_cparams("arbitrary", "arbitrary"),
        name="rwkv_post",
    )(y, r, k, v, gate, lnx_g.reshape(1, D_MODEL), lnx_b.reshape(1, D_MODEL),
      r_k.reshape(1, D_MODEL))


def _trunk(x, mod, p, wb, k_cache, v_cache, shift0, s0):
    seqs, length, _ = x.shape
    part = lambda layer, j: mod[layer, :, j * D_MODEL:(j + 1) * D_MODEL].reshape(seqs, 1, D_MODEL)
    router_wt = p['router_w'].T
    sh1, sc1, g1, sh2, sc2, g2 = [part(0, j) for j in range(6)]
    k, v, qh, ql, kh, kl, vh, vl = _qkv(x, p['norm_mix_g'][0], sc1, sh1, *wb['sb_w_qkv'])
    if k_cache is None:
        attn = _attn_prompt((qh, ql), (kh, kl), (vh, vl))
    else:
        past = k_cache.shape[1]
        attn = _attn_sample((qh, ql), (kh, kl), (vh, vl), k_cache.reshape(seqs, past, D_MODEL),
                            v_cache.reshape(seqs, past, D_MODEL))
    x, h, gates, group = _oproj_router(attn, x, g1, wb['sb_w_o'], p['norm_ffn_g'][0], sc2, sh2,
                                       router_wt, p['router_bias'])
    x = _moe(h, group, gates, x, g2, p['final_norm_g'], wb['exp_w_gate'][0], wb['exp_w_up'][0],
             wb['exp_w_down'][0], final_norm=False)
    sb_k = k.reshape(seqs, length, HEADS, HEAD_DIM)
    sb_v = v.reshape(seqs, length, HEADS, HEAD_DIM)
    sh1, sc1, g1, sh2, sc2, g2 = [part(1, j) for j in range(6)]
    r, w, kq, vv, na, kb, gate, shift = _rwkv_pre(x, shift0, p['norm_mix_g'][1], sc1, sh1, wb)
    y, state = _scan(r, w, kq, vv, na, kb, s0)
    m = _rwkv_post(y, r, kq, vv, gate, p['rw_lnx_g'], p['rw_lnx_b'], p['rw_r_k'])
    x, h, gates, group = _oproj_router(m, x, g1, wb['rw_w_o'], p['norm_ffn_g'][1], sc2, sh2,
                                       router_wt, p['router_bias'])
    y_out = _moe(h, group, gates, x, g2, p['final_norm_g'], wb['exp_w_gate'][1], wb['exp_w_up'][1],
                 wb['exp_w_down'][1], final_norm=True)
    return y_out, sb_k, sb_v, state, shift


def kernel(x_prompt, x_sample, cache_sb_k, cache_sb_v, state_rw_wkv, state_rw_shift,
           c_prompt, c_sample, ada_w, ada_b, norm_mix_g, norm_ffn_g, final_norm_g,
           sb_w_qkv, sb_w_o, rw_mu, rw_w_r, rw_w_k, rw_w_v, rw_w0, rw_w1, rw_w2,
           rw_a0, rw_a1, rw_a2, rw_g1, rw_g2, rw_k_k, rw_k_a, rw_r_k, rw_lnx_g, rw_lnx_b,
           rw_w_o, router_w, router_bias, exp_w_gate, exp_w_up, exp_w_down):
    p = {
        'norm_mix_g': norm_mix_g, 'norm_ffn_g': norm_ffn_g, 'final_norm_g': final_norm_g,
        'rw_r_k': rw_r_k, 'rw_lnx_g': rw_lnx_g, 'rw_lnx_b': rw_lnx_b,
        'router_w': router_w, 'router_bias': router_bias,
    }
    bf = lambda a: a.astype(BF16)
    wb = {
        'sb_w_qkv': _split_weight(sb_w_qkv), 'sb_w_o': _split_weight(sb_w_o),
        'rw_w_o': _split_weight(rw_w_o),
        'rw_mu': rw_mu, 'rw_w_r': bf(rw_w_r), 'rw_w_k': bf(rw_w_k), 'rw_w_v': bf(rw_w_v),
        'rw_w0': rw_w0, 'rw_w1': bf(rw_w1), 'rw_w2': bf(rw_w2), 'rw_a0': rw_a0,
        'rw_a1': bf(rw_a1), 'rw_a2': bf(rw_a2), 'rw_g1': bf(rw_g1), 'rw_g2': bf(rw_g2),
        'rw_k_k': rw_k_k, 'rw_k_a': rw_k_a,
        'exp_w_gate': bf(exp_w_gate), 'exp_w_up': bf(exp_w_up), 'exp_w_down': bf(exp_w_down),
    }
    n_prompt = x_prompt.shape[0]
    mod = _ada(jnp.concatenate([c_prompt, c_sample], axis=0), ada_w, ada_b)
    shift0 = jnp.zeros((n_prompt, 1, D_MODEL), x_prompt.dtype)
    s0 = jnp.zeros((n_prompt, HEADS, HEAD_DIM, HEAD_DIM), F32)
    y_p, k_p, v_p, st_p, sh_p = _trunk(x_prompt, mod[:, :n_prompt], p, wb, None, None, shift0, s0)
    y_s, k_s, v_s, st_s, sh_s = _trunk(x_sample, mod[:, n_prompt:], p, wb, cache_sb_k, cache_sb_v,
                                       state_rw_shift, state_rw_wkv)
    return (y_p, y_s, k_p, v_p, k_s, v_s, st_p, sh_p, st_s, sh_s)
```

```python
import functools

import jax
import jax.numpy as jnp
from jax import lax
from jax.experimental import pallas as pl
from jax.experimental.pallas import tpu as pltpu

F32 = jnp.float32
BF16 = jnp.bfloat16

D_MODEL = 1024
HEADS = 16
HEAD_DIM = 64
LANES = 128
SUBLANES = 8
PAIRS = D_MODEL // LANES
N_EXPERTS = 16
N_GROUPS = 4
EXPERTS_PER_GROUP = 4
D_EXPERT = 512
NORM_EPS = 1e-6
LNX_EPS = 64e-5
ZERO_WEIGHT_CARRY = 104.0
TOKEN_TILE = 512
RWKV_PRE_TILE = 256
QKV_TILE = 256
SCAN_CHUNK = 64
SCAN_BLOCK = 256
SCAN_GROUP = 2
MOE_TILE = 512
VMEM_LIMIT = 56 * 1024 * 1024

_NT = (((1,), (1,)), ((), ()))
_BATCHED = (((2,), (1,)), ((0,), (0,)))
_BATCHED_NT = (((2,), (2,)), ((0,), (0,)))
_BATCHED_TN = (((1,), (1,)), ((0,), (0,)))


def _cparams(*sem):
    return pltpu.CompilerParams(dimension_semantics=sem, vmem_limit_bytes=VMEM_LIMIT)


def _sigmoid(x):
    return 1.0 / (1.0 + jnp.exp(-x))


def _split2(x):
    hi = x.astype(BF16)
    lo = (x - hi.astype(F32)).astype(BF16)
    return hi, lo


def _dot(a, b):
    return jnp.dot(a, b, preferred_element_type=F32)


def _dot2(x, w2):
    hi, lo = _split2(x)
    return _dot(jnp.concatenate([hi, lo], axis=1), w2)


def _norm_mod(x, g, sc, sh):
    ms = jnp.mean(x * x, axis=-1, keepdims=True)
    y = x * lax.rsqrt(ms + NORM_EPS) * g
    return y * (1.0 + sc) + sh


def _seq_blocks(seqs, length, tile=TOKEN_TILE):
    if length >= tile:
        assert length % tile == 0
        return 1, tile
    assert tile % length == 0 and seqs % (tile // length) == 0 and length % 16 == 0
    return tile // length, length


def _ada_kernel(c_ref, w_ref, b_ref, o_ref):
    c = c_ref[...]
    s = c * _sigmoid(c)
    o_ref[0] = jnp.dot(s, w_ref[0], precision=lax.Precision.HIGHEST,
                       preferred_element_type=F32) + b_ref[0]


def _ada(c, ada_w, ada_b):
    rows = c.shape[0]
    depth = ada_w.shape[0]
    return pl.pallas_call(
        _ada_kernel,
        out_shape=jax.ShapeDtypeStruct((depth, rows, 6 * D_MODEL), F32),
        grid=(depth, 6),
        in_specs=[pl.BlockSpec((rows, D_MODEL), lambda l, j: (0, 0)),
                  pl.BlockSpec((1, D_MODEL, D_MODEL), lambda l, j: (l, 0, j)),
                  pl.BlockSpec((1, 1, D_MODEL), lambda l, j: (l, 0, j))],
        out_specs=pl.BlockSpec((1, rows, D_MODEL), lambda l, j: (l, 0, j)),
        compiler_params=_cparams("arbitrary", "arbitrary"),
        name="ada_mod",
    )(c, ada_w, ada_b.reshape(depth, 1, 6 * D_MODEL))


def _dot3(x, w_hi, w_lo):
    hi, lo = _split2(x)
    rows = x.shape[0]
    both = _dot(jnp.concatenate([hi, lo], axis=0), w_hi)
    return both[:rows] + both[rows:] + _dot(hi, w_lo)


def _split_weight(w):
    hi = w.astype(BF16)
    return hi, (w - hi.astype(F32)).astype(BF16)


def _qkv_kernel(x_ref, g_ref, sc_ref, sh_ref, wh_ref, wl_ref, k_ref, v_ref, *piece_refs):
    bs, tl, _ = x_ref.shape
    h = _norm_mod(x_ref[...], g_ref[...], sc_ref[...], sh_ref[...]).reshape(bs * tl, D_MODEL)
    qkv = _dot3(h, wh_ref[...], wl_ref[...])
    q = qkv[:, :D_MODEL] * (HEAD_DIM ** -0.5)
    k = qkv[:, D_MODEL:2 * D_MODEL]
    v = qkv[:, 2 * D_MODEL:]
    k_ref[...] = k.reshape(bs, tl, D_MODEL)
    v_ref[...] = v.reshape(bs, tl, D_MODEL)
    pieces = _split2(q) + _split2(k) + _split2(v)
    for ref, piece in zip(piece_refs, pieces):
        ref[...] = piece.reshape(bs, tl, D_MODEL)


def _qkv(x, g, sc, sh, w_hi, w_lo):
    seqs, length, _ = x.shape
    bs, tl = _seq_blocks(seqs, length, QKV_TILE)
    tok = pl.BlockSpec((bs, tl, D_MODEL), lambda s, i: (s, i, 0))
    mod = pl.BlockSpec((bs, 1, D_MODEL), lambda s, i: (s, 0, 0))
    wspec = pl.BlockSpec((D_MODEL, 3 * D_MODEL), lambda s, i: (0, 0))
    f32o = jax.ShapeDtypeStruct((seqs, length, D_MODEL), F32)
    bfo = jax.ShapeDtypeStruct((seqs, length, D_MODEL), BF16)
    return pl.pallas_call(
        _qkv_kernel,
        out_shape=(f32o, f32o) + (bfo,) * 6,
        grid=(seqs // bs, length // tl),
        in_specs=[tok, pl.BlockSpec((1, 1, D_MODEL), lambda s, i: (0, 0, 0)), mod, mod, wspec, wspec],
        out_specs=(tok,) * 8,
        compiler_params=_cparams("arbitrary", "arbitrary"),
        name="sb_qkv",
    )(x, g.reshape(1, 1, D_MODEL), sc, sh, w_hi, w_lo)


def _pair_masks(rows):
    lane = lax.broadcasted_iota(jnp.int32, (rows, LANES), 1)
    return lane < HEAD_DIM


def _suffix_matrix():
    r = lax.broadcasted_iota(jnp.int32, (LANES, LANES), 0)
    c = lax.broadcasted_iota(jnp.int32, (LANES, LANES), 1)
    u = jnp.where(r > c, 1.0, 0.0).astype(BF16)
    return jnp.concatenate([u, u], axis=0)


def _query_pieces(q_hi, q_lo):
    hi, lo = _stack_heads(q_hi), _stack_heads(q_lo)
    return jnp.concatenate([hi, lo, hi], axis=1)


def _key_pieces(k_hi, k_lo):
    return jnp.concatenate([k_hi, k_hi, k_lo], axis=1)


def _sb_sweep(q3, keys, values, state, u2, visible):
    n = len(q3)
    rows = q3[0].shape[0]
    tq = rows // 2
    wide = lambda x: jnp.concatenate([x[:tq], x[tq:]], axis=1)
    z_all = lax.dot_general(jnp.stack(q3), jnp.stack([_key_pieces(*keys[p]) for p in range(n)]),
                            _BATCHED_NT, preferred_element_type=F32)
    zs, sps = [], []
    for p in range(n):
        z = z_all[p]
        sp = jnp.maximum(z, 0.0) + jnp.log(1.0 + jnp.exp(-jnp.abs(z)))
        if visible is not None:
            sp = jnp.where(visible, sp, 0.0)
        zs.append(z)
        sps.append(sp)
    later_all = _dot2(jnp.concatenate(sps, axis=0), u2)
    weights, vals = [], []
    for p in range(n):
        later = later_all[p * rows:(p + 1) * rows]
        w = jnp.exp(zs[p] - sps[p] - later - state[p][0])
        if visible is not None:
            w = jnp.where(visible, w, 0.0)
        w_hi, w_lo = _split2(w)
        v_hi, v_lo = _stack_heads(values[p][0]), _stack_heads(values[p][1])
        weights.append(jnp.concatenate([wide(w_hi), wide(w_lo), wide(w_hi)], axis=1))
        vals.append(jnp.concatenate([v_hi, v_hi, v_lo], axis=0))
    gained = lax.dot_general(jnp.stack(weights), jnp.stack(vals), _BATCHED,
                             preferred_element_type=F32)
    return tuple((state[p][0] + jnp.sum(sps[p], axis=-1, keepdims=True), state[p][1] + gained[p])
                 for p in range(n))


def _stack_heads(x, axis=0):
    mask_a = _pair_masks(x.shape[0])
    return jnp.concatenate([jnp.where(mask_a, x, 0), jnp.where(mask_a, 0, x)], axis=axis)


def _diag_visible(tq):
    row = lax.broadcasted_iota(jnp.int32, (2 * tq, LANES), 0)
    col = lax.broadcasted_iota(jnp.int32, (2 * tq, LANES), 1)
    qpos = jnp.where(row >= tq, row - tq, row)
    return col < qpos


def _pair_lanes(p):
    return slice(p * LANES, (p + 1) * LANES)


def _sweep_earlier_tiles(n_tiles, sweep, state):
    def cond(c):
        return jnp.logical_and(c[0] < n_tiles, jnp.logical_not(c[1]))

    def body(c):
        st = sweep(n_tiles - 1 - c[0], c[2])
        low = st[0][0]
        for p in range(1, len(st)):
            low = jnp.minimum(low, st[p][0])
        return c[0] + 1, jnp.min(low) > ZERO_WEIGHT_CARRY, st

    return lax.while_loop(cond, body, (jnp.int32(0), False, state))[2]


def _attn_prompt_kernel(qh_ref, ql_ref, kh_ref, kl_ref, vh_ref, vl_ref, o_ref):
    tq = qh_ref.shape[1]
    qi = pl.program_id(1)
    u2 = _suffix_matrix()
    q3 = [_query_pieces(qh_ref[0, :, _pair_lanes(p)], ql_ref[0, :, _pair_lanes(p)])
          for p in range(PAIRS)]

    def sweep(start, state, visible):
        rows = pl.ds(pl.multiple_of(start, tq), tq)
        tile = lambda ref, p: ref[0, rows, _pair_lanes(p)]
        return _sb_sweep(q3, [(tile(kh_ref, p), tile(kl_ref, p)) for p in range(PAIRS)],
                         [(tile(vh_ref, p), tile(vl_ref, p)) for p in range(PAIRS)],
                         state, u2, visible)

    zero = (jnp.zeros((2 * tq, LANES), F32), jnp.zeros((tq, LANES), F32))
    state = sweep(qi * tq, (zero,) * PAIRS, _diag_visible(tq))
    state = _sweep_earlier_tiles(qi, lambda j, st: sweep(j * tq, st, None), state)
    for p in range(PAIRS):
        o_ref[0, :, _pair_lanes(p)] = state[p][1]


def _attn_prompt(q, k, v):
    batch, length, _ = q[0].shape
    tq = LANES
    qspec = pl.BlockSpec((1, tq, D_MODEL), lambda b, i: (b, i, 0))
    kspec = pl.BlockSpec((1, length, D_MODEL), lambda b, i: (b, 0, 0), pipeline_mode=pl.Buffered(1))
    return pl.pallas_call(
        _attn_prompt_kernel,
        out_shape=jax.ShapeDtypeStruct((batch, length, D_MODEL), F32),
        grid=(batch, length // tq),
        in_specs=[qspec, qspec, kspec, kspec, kspec, kspec],
        out_specs=qspec,
        compiler_params=_cparams("arbitrary", "arbitrary"),
        name="sb_attn_prompt",
    )(*q, *k, *v)


def _attn_sample_kernel(qh_ref, ql_ref, kh_ref, kl_ref, vh_ref, vl_ref, kc_ref, vc_ref, o_ref):
    tq = qh_ref.shape[1]
    past = kc_ref.shape[1]
    u2 = _suffix_matrix()
    pad = jnp.zeros((LANES - tq, LANES), BF16)
    padded = lambda ref, p: jnp.concatenate([ref[0, :, _pair_lanes(p)], pad], axis=0)
    q3 = [_query_pieces(qh_ref[0, :, _pair_lanes(p)], ql_ref[0, :, _pair_lanes(p)])
          for p in range(PAIRS)]
    zero = (jnp.zeros((2 * tq, LANES), F32), jnp.zeros((tq, LANES), F32))
    state = _sb_sweep(q3, [(padded(kh_ref, p), padded(kl_ref, p)) for p in range(PAIRS)],
                      [(padded(vh_ref, p), padded(vl_ref, p)) for p in range(PAIRS)],
                      (zero,) * PAIRS, u2, _diag_visible(tq))

    def sweep(j, st):
        rows = pl.ds(pl.multiple_of(j * LANES, LANES), LANES)
        return _sb_sweep(q3, [_split2(kc_ref[0, rows, _pair_lanes(p)]) for p in range(PAIRS)],
                         [_split2(vc_ref[0, rows, _pair_lanes(p)]) for p in range(PAIRS)],
                         st, u2, None)

    state = _sweep_earlier_tiles(past // LANES, sweep, state)
    for p in range(PAIRS):
        o_ref[0, :, _pair_lanes(p)] = state[p][1]


def _attn_sample(q, k, v, k_cache, v_cache):
    batch, length, _ = q[0].shape
    past = k_cache.shape[1]
    assert length <= LANES and past % LANES == 0
    qspec = pl.BlockSpec((1, length, D_MODEL), lambda b: (b, 0, 0))
    cspec = pl.BlockSpec((1, past, D_MODEL), lambda b: (b, 0, 0))
    return pl.pallas_call(
        _attn_sample_kernel,
        out_shape=jax.ShapeDtypeStruct((batch, length, D_MODEL), F32),
        grid=(batch,),
        in_specs=[qspec] * 6 + [cspec, cspec],
        out_specs=qspec,
        compiler_params=_cparams("arbitrary"),
        name="sb_attn_sample",
    )(*q, *k, *v, k_cache, v_cache)


def _row_max(rows):
    m = rows[0]
    for r in rows[1:]:
        m = jnp.maximum(m, r)
    return m


def _first_hit(rows, m):
    hits = []
    taken = None
    for r in rows:
        eq = r == m
        hit = eq if taken is None else jnp.logical_and(eq, jnp.logical_not(taken))
        taken = eq if taken is None else jnp.logical_or(taken, eq)
        hits.append(hit)
    return hits


def _route(s_rows, sel_rows):
    neg = jnp.float32(-jnp.inf)
    top1, top2, score = [], [], []
    for g in range(N_GROUPS):
        rows = sel_rows[g * EXPERTS_PER_GROUP:(g + 1) * EXPERTS_PER_GROUP]
        m1 = _row_max(rows)
        h1 = _first_hit(rows, m1)
        rest = [jnp.where(h, neg, r) for h, r in zip(h1, rows)]
        m2 = _row_max(rest)
        h2 = _first_hit(rest, m2)
        top1.append(h1)
        top2.append(h2)
        score.append(m1 + m2)
    best = _row_max(score)
    ghit = _first_hit(score, best)
    picked = []
    for g in range(N_GROUPS):
        for e in range(EXPERTS_PER_GROUP):
            picked.append(jnp.logical_and(ghit[g], jnp.logical_or(top1[g][e], top2[g][e])))
    total = None
    for p, s in zip(picked, s_rows):
        term = jnp.where(p, s, 0.0)
        total = term if total is None else total + term
    gates = [jnp.where(p, s / total, 0.0) for p, s in zip(picked, s_rows)]
    in_group = []
    for e in range(EXPERTS_PER_GROUP):
        row = gates[e]
        for g in range(1, N_GROUPS):
            row = row + gates[g * EXPERTS_PER_GROUP + e]
        in_group.append(row)
    group = jnp.zeros_like(best)
    for g in range(1, N_GROUPS):
        group = jnp.where(ghit[g], float(g), group)
    return in_group, group


def _oproj_kernel(m_ref, x_ref, gate_ref, wh_ref, wl_ref, g_ref, sc_ref, sh_ref, rw_ref, rb_ref,
                  x1_ref, h_ref, route_ref):
    bs, tl, _ = x_ref.shape
    m = m_ref[...].reshape(bs * tl, D_MODEL)
    mix = _dot3(m, wh_ref[...], wl_ref[...]).reshape(bs, tl, D_MODEL)
    x1 = x_ref[...] + gate_ref[...] * mix
    x1_ref[...] = x1
    h = _norm_mod(x1, g_ref[...], sc_ref[...], sh_ref[...])
    h_ref[...] = h
    logits = lax.dot_general(rw_ref[...], h.reshape(bs * tl, D_MODEL), _NT,
                             precision=lax.Precision.HIGHEST,
                             preferred_element_type=F32)
    s = _sigmoid(logits)
    sel = s + rb_ref[...]
    s_rows = [s[e:e + 1, :] for e in range(N_EXPERTS)]
    sel_rows = [sel[e:e + 1, :] for e in range(N_EXPERTS)]
    in_group, group = _route(s_rows, sel_rows)
    pad = [jnp.zeros_like(group)] * (SUBLANES - EXPERTS_PER_GROUP - 1)
    route_ref[0] = jnp.concatenate(in_group + [group] + pad, axis=0)


def _oproj_router(m, x, gate, w_pieces, g, sc, sh, router_wt, router_b):
    seqs, length, _ = x.shape
    bs, tl = _seq_blocks(seqs, length)
    n_tiles = (seqs // bs) * (length // tl)
    per_seq = length // tl
    tok = pl.BlockSpec((bs, tl, D_MODEL), lambda s, i: (s, i, 0))
    mod = pl.BlockSpec((bs, 1, D_MODEL), lambda s, i: (s, 0, 0))
    one = pl.BlockSpec((1, 1, D_MODEL), lambda s, i: (0, 0, 0))
    wspec = pl.BlockSpec((D_MODEL, D_MODEL), lambda s, i: (0, 0))
    f32o = jax.ShapeDtypeStruct((seqs, length, D_MODEL), F32)
    x1, h, route = pl.pallas_call(
        _oproj_kernel,
        out_shape=(f32o, f32o, jax.ShapeDtypeStruct((n_tiles, SUBLANES, bs * tl), F32)),
        grid=(seqs // bs, per_seq),
        in_specs=[tok, tok, mod, wspec, wspec, one, mod, mod,
                  pl.BlockSpec((N_EXPERTS, D_MODEL), lambda s, i: (0, 0)),
                  pl.BlockSpec((N_EXPERTS, 1), lambda s, i: (0, 0))],
        out_specs=(tok, tok,
                   pl.BlockSpec((1, SUBLANES, bs * tl), lambda s, i: (s * per_seq + i, 0, 0))),
        compiler_params=_cparams("arbitrary", "arbitrary"),
        name="oproj_router",
    )(m, x, gate, *w_pieces, g.reshape(1, 1, D_MODEL), sc, sh, router_wt,
      router_b.reshape(N_EXPERTS, 1))
    route = jnp.swapaxes(route, 1, 2).reshape(seqs * length, SUBLANES)
    return x1, h, route[:, :EXPERTS_PER_GROUP], route[:, EXPERTS_PER_GROUP].astype(jnp.int32)


def _dispatch_plan(group, gates, tile):
    n = group.shape[0]
    n_slots = n + N_GROUPS * tile
    onehot = (group[:, None] == jnp.arange(N_GROUPS)[None, :]).astype(jnp.int32)
    counts = jnp.sum(onehot, axis=0)
    padded = (counts + tile - 1) // tile * tile
    ends = jnp.cumsum(padded)
    starts = ends - padded
    rank = jnp.take_along_axis(jnp.cumsum(onehot, axis=0) - onehot, group[:, None], axis=1)[:, 0]
    slot = starts[group] + rank
    src = jnp.zeros((n_slots,), jnp.int32).at[slot].set(jnp.arange(n, dtype=jnp.int32))
    slot_gates = jnp.zeros((n_slots, EXPERTS_PER_GROUP), F32).at[slot].set(gates)
    tile_start = jnp.arange(n_slots // tile, dtype=jnp.int32) * tile
    tile_group = jnp.minimum(jnp.sum(tile_start[:, None] >= ends[None, :], axis=1), N_GROUPS - 1)
    return src, slot.astype(jnp.int32), slot_gates, tile_group.astype(jnp.int32)


def _start_row_gather(idx_ref, base, src_hbm, dst_ref, sem):
    def body(r, _):
        row = idx_ref[base + r]
        pltpu.make_async_copy(src_hbm.at[pl.ds(row, 1), :], dst_ref.at[pl.ds(r, 1), :], sem).start()
        return 0

    lax.fori_loop(0, dst_ref.shape[0], body, 0, unroll=8)


def _wait_row_gather(src_hbm, dst_ref, sem):
    pltpu.make_async_copy(src_hbm.at[pl.ds(0, dst_ref.shape[0]), :], dst_ref, sem).wait()


def _moe_kernel(src_ref, tg_ref, h_hbm, gates_ref, wg_ref, wu_ref, wd_ref, o_ref, hbuf, sem):
    t = pl.program_id(0)
    tile = o_ref.shape[0]
    slot = t % 2

    @pl.when(t == 0)
    def _():
        _start_row_gather(src_ref, 0, h_hbm, hbuf.at[0], sem.at[0])

    @pl.when(t + 1 < pl.num_programs(0))
    def _():
        _start_row_gather(src_ref, (t + 1) * tile, h_hbm, hbuf.at[1 - slot], sem.at[1 - slot])

    _wait_row_gather(h_hbm, hbuf.at[slot], sem.at[slot])
    h = hbuf[slot].astype(BF16)
    gates = gates_ref[...]
    acc = jnp.zeros((tile, D_MODEL), F32)
    for e in range(EXPERTS_PER_GROUP):
        a = _dot(h, wg_ref[e])
        u = _dot(h, wu_ref[e])
        he = (a * _sigmoid(a) * u).astype(BF16)
        acc = acc + gates[:, e:e + 1] * _dot(he, wd_ref[e])
    o_ref[...] = acc


def _moe_experts(h, src, slot_gates, tile_group, wg_bf, wu_bf, wd_bf, tile):
    n_slots = src.shape[0]
    group_w = lambda shape: pl.BlockSpec((EXPERTS_PER_GROUP,) + shape, lambda t, src, tg: (tg[t], 0, 0))
    return pl.pallas_call(
        _moe_kernel,
        out_shape=jax.ShapeDtypeStruct((n_slots, D_MODEL), F32),
        grid_spec=pltpu.PrefetchScalarGridSpec(
            num_scalar_prefetch=2,
            grid=(n_slots // tile,),
            in_specs=[pl.BlockSpec(memory_space=pl.ANY),
                      pl.BlockSpec((tile, EXPERTS_PER_GROUP), lambda t, src, tg: (t, 0)),
                      group_w((D_MODEL, D_EXPERT)), group_w((D_MODEL, D_EXPERT)),
                      group_w((D_EXPERT, D_MODEL))],
            out_specs=pl.BlockSpec((tile, D_MODEL), lambda t, src, tg: (t, 0)),
            scratch_shapes=[pltpu.VMEM((2, tile, D_MODEL), F32), pltpu.SemaphoreType.DMA((2,))]),
        compiler_params=_cparams("arbitrary"),
        name="moe_experts",
    )(src, tile_group, h, slot_gates, wg_bf, wu_bf, wd_bf)


def _moe_combine_kernel(slot_ref, f_hbm, x_ref, gate_ref, fg_ref, o_ref, fbuf, sem, *, final_norm):
    bs, tl, _ = x_ref.shape
    tile = bs * tl
    t = pl.program_id(0) * pl.num_programs(1) + pl.program_id(1)
    n_tiles = pl.num_programs(0) * pl.num_programs(1)
    slot = t % 2

    @pl.when(t == 0)
    def _():
        _start_row_gather(slot_ref, 0, f_hbm, fbuf.at[0], sem.at[0])

    @pl.when(t + 1 < n_tiles)
    def _():
        _start_row_gather(slot_ref, (t + 1) * tile, f_hbm, fbuf.at[1 - slot], sem.at[1 - slot])

    _wait_row_gather(f_hbm, fbuf.at[slot], sem.at[slot])
    x2 = x_ref[...] + gate_ref[...] * fbuf[slot].reshape(bs, tl, D_MODEL)
    if final_norm:
        ms = jnp.mean(x2 * x2, axis=-1, keepdims=True)
        x2 = x2 * lax.rsqrt(ms + NORM_EPS) * fg_ref[...]
    o_ref[...] = x2


def _moe_combine(ffn_sorted, slot, x, gate, final_g, final_norm):
    seqs, length, _ = x.shape
    bs, tl = _seq_blocks(seqs, length)
    tok = pl.BlockSpec((bs, tl, D_MODEL), lambda s, i, slot: (s, i, 0))
    return pl.pallas_call(
        functools.partial(_moe_combine_kernel, final_norm=final_norm),
        out_shape=jax.ShapeDtypeStruct((seqs, length, D_MODEL), F32),
        grid_spec=pltpu.PrefetchScalarGridSpec(
            num_scalar_prefetch=1,
            grid=(seqs // bs, length // tl),
            in_specs=[pl.BlockSpec(memory_space=pl.ANY), tok,
                      pl.BlockSpec((bs, 1, D_MODEL), lambda s, i, slot: (s, 0, 0)),
                      pl.BlockSpec((1, 1, D_MODEL), lambda s, i, slot: (0, 0, 0))],
            out_specs=tok,
            scratch_shapes=[pltpu.VMEM((2, bs * tl, D_MODEL), F32), pltpu.SemaphoreType.DMA((2,))]),
        compiler_params=_cparams("arbitrary", "arbitrary"),
        name="moe_combine",
    )(slot, ffn_sorted, x, gate, final_g.reshape(1, 1, D_MODEL))


def _moe(h, group, gates, x, gate, final_g, wg_bf, wu_bf, wd_bf, final_norm):
    seqs, length, _ = x.shape
    tile = min(MOE_TILE, max(LANES, seqs * length // (2 * N_GROUPS)))
    src, slot, slot_gates, tile_group = _dispatch_plan(group, gates, tile)
    ffn_sorted = _moe_experts(h.reshape(seqs * length, D_MODEL), src, slot_gates, tile_group,
                              wg_bf, wu_bf, wd_bf, tile)
    return _moe_combine(ffn_sorted, slot, x, gate, final_g, final_norm)


def _head_selectors():
    r = lax.broadcasted_iota(jnp.int32, (D_MODEL, LANES), 0)
    c = lax.broadcasted_iota(jnp.int32, (D_MODEL, LANES), 1)
    sel = jnp.where(r // HEAD_DIM == c, 1.0, 0.0).astype(BF16)
    rt = lax.broadcasted_iota(jnp.int32, (LANES, D_MODEL), 0)
    ct = lax.broadcasted_iota(jnp.int32, (LANES, D_MODEL), 1)
    selt = jnp.where(ct // HEAD_DIM == rt, 1.0, 0.0).astype(BF16)
    return jnp.concatenate([sel, sel], axis=0), jnp.concatenate([selt, selt], axis=0)


def _head_sum(x, sel2, selt2):
    return _dot2(_dot2(x, sel2), selt2)


def _rwkv_pre_kernel(x_ref, prev_ref, shift_ref, g_ref, sc_ref, sh_ref, mu_ref,
                     wr_ref, wk_ref, wv_ref, w0_ref, w1_ref, w2_ref, a0_ref, a1_ref, a2_ref,
                     g1_ref, g2_ref, kk_ref, ka_ref,
                     r_ref, w_ref, k_ref, v_ref, na_ref, kb_ref, go_ref, so_ref):
    bs, tl, _ = x_ref.shape
    t = bs * tl
    i = pl.program_id(1)
    g, sc, sh = g_ref[...], sc_ref[...], sh_ref[...]
    h = _norm_mod(x_ref[...], g, sc, sh)
    so_ref[...] = h[:, tl - 1:tl, :]
    h_before = _norm_mod(prev_ref[:, 7:8, :], g, sc, sh)
    first = jnp.where(i == 0, shift_ref[...], h_before)
    row = lax.broadcasted_iota(jnp.int32, (bs, tl, D_MODEL), 1)
    prev = jnp.where(row == 0, first, pltpu.roll(h, 1, 1))
    h = h.reshape(t, D_MODEL)
    xx = prev.reshape(t, D_MODEL) - h
    mixed = [(h + xx * mu_ref[j:j + 1, :]).astype(BF16) for j in range(6)]
    xr, xw, xk, xv, xa, xg = mixed
    r = _dot(xr, wr_ref[...])
    k = _dot(xk, wk_ref[...])
    v = _dot(xv, wv_ref[...])
    wl = w0_ref[...] + _dot(jnp.tanh(_dot(xw, w1_ref[...])).astype(BF16), w2_ref[...])
    neg = -wl
    softplus = jnp.maximum(neg, 0.0) + jnp.log(1.0 + jnp.exp(-jnp.abs(neg)))
    w = -softplus - 0.5
    log_decay = -jnp.exp(w)
    a = _sigmoid(a0_ref[...] + _dot(_dot(xa, a1_ref[...]).astype(BF16), a2_ref[...]))
    gate = _dot(_sigmoid(_dot(xg, g1_ref[...])).astype(BF16), g2_ref[...])
    sel2, selt2 = _head_selectors()
    kk = k * kk_ref[...]
    norm = jnp.maximum(jnp.sqrt(_head_sum(kk * kk, sel2, selt2)), 1e-12)
    kk = kk / norm
    k = k * (1.0 + (a - 1.0) * ka_ref[...])
    shape = (bs, tl, D_MODEL)
    r_ref[...] = r.reshape(shape)
    w_ref[...] = log_decay.reshape(shape)
    k_ref[...] = k.reshape(shape)
    v_ref[...] = v.reshape(shape)
    na_ref[...] = (-kk).reshape(shape)
    kb_ref[...] = (kk * a).reshape(shape)
    go_ref[...] = gate.reshape(shape)


def _rwkv_pre(x, shift0, g, sc, sh, p):
    seqs, length, _ = x.shape
    bs, tl = _seq_blocks(seqs, length, RWKV_PRE_TILE)
    tok = pl.BlockSpec((bs, tl, D_MODEL), lambda s, i: (s, i, 0))
    prev = pl.BlockSpec((bs, 8, D_MODEL), lambda s, i: (s, jnp.maximum(i * (tl // 8) - 1, 0), 0))
    mod = pl.BlockSpec((bs, 1, D_MODEL), lambda s, i: (s, 0, 0))
    one = pl.BlockSpec((1, 1, D_MODEL), lambda s, i: (0, 0, 0))

    def full(a):
        return pl.BlockSpec(a.shape, lambda s, i: (0,) * a.ndim)

    vec = lambda a: a.reshape(1, D_MODEL)
    weights = [p['rw_mu'], p['rw_w_r'], p['rw_w_k'], p['rw_w_v'], vec(p['rw_w0']), p['rw_w1'],
               p['rw_w2'], vec(p['rw_a0']), p['rw_a1'], p['rw_a2'], p['rw_g1'], p['rw_g2'],
               vec(p['rw_k_k']), vec(p['rw_k_a'])]
    f32o = jax.ShapeDtypeStruct((seqs, length, D_MODEL), F32)
    outs = pl.pallas_call(
        _rwkv_pre_kernel,
        out_shape=(f32o,) * 7 + (jax.ShapeDtypeStruct((seqs, 1, D_MODEL), F32),),
        grid=(seqs // bs, length // tl),
        in_specs=[tok, prev, mod, one, mod, mod] + [full(a) for a in weights],
        out_specs=(tok,) * 7 + (mod,),
        compiler_params=_cparams("arbitrary", "arbitrary"),
        name="rwkv_pre",
    )(x, x, shift0, g.reshape(1, 1, D_MODEL), sc, sh, *weights)
    return outs


def _bdot(a, b, dims=None):
    a, b = a.astype(BF16), b.astype(BF16)
    if dims is None:
        return jnp.dot(a, b, preferred_element_type=F32)
    return lax.dot_general(a, b, dims, preferred_element_type=F32)


def _scan_kernel(r_ref, lw_ref, k_ref, v_ref, a_ref, b_ref, s0_ref, y_ref, st_ref, z_ref, *, chunk):
    tc = r_ref.shape[1]
    t2 = 2 * chunk
    c = pl.program_id(1)
    lane_a = _pair_masks(chunk)
    fold = lambda x: jnp.where(lane_a, x[:chunk], x[chunk:])
    stack = lambda x: jnp.concatenate([x, x], axis=0)
    split_heads = lambda x: _stack_heads(x, axis=0)

    def block_diag(mats):
        n = mats[0].shape[0]
        zero = jnp.zeros((n, n), F32)
        return jnp.concatenate(
            [jnp.concatenate([m if j == i else zero for j in range(len(mats))], axis=1)
             for i, m in enumerate(mats)], axis=0)

    @pl.when(c == 0)
    def _():
        for q in range(PAIRS // SCAN_GROUP):
            z_ref[q] = block_diag([s0_ref[0, 2 * SCAN_GROUP * q + h] for h in range(2 * SCAN_GROUP)])

    row = lax.broadcasted_iota(jnp.int32, (t2, t2), 0)
    col = lax.broadcasted_iota(jnp.int32, (t2, t2), 1)
    same_head = (row // chunk) == (col // chunk)
    strict = jnp.logical_and(same_head, (row % chunk) > (col % chunk))
    incl = jnp.logical_and(same_head, (row % chunk) >= (col % chunk))
    tr = lax.broadcasted_iota(jnp.int32, (chunk, chunk), 0)
    tcol = lax.broadcasted_iota(jnp.int32, (chunk, chunk), 1)
    tri = jnp.where(tr >= tcol, 1.0, 0.0).astype(BF16)
    tri2 = jnp.concatenate([tri, tri], axis=1)
    gw = SCAN_GROUP * LANES
    zr = lax.broadcasted_iota(jnp.int32, (gw, gw), 0)
    zc = lax.broadcasted_iota(jnp.int32, (gw, gw), 1)
    z_diag = (zr // HEAD_DIM) == (zc // HEAD_DIM)
    n_doublings = chunk.bit_length() - 1

    def one_chunk(ci, _):
        rows = pl.ds(pl.multiple_of(ci * chunk, chunk), chunk)
        lw, r, k, v, a, b = [ref[0, rows, :] for ref in (lw_ref, r_ref, k_ref, v_ref, a_ref, b_ref)]
        hi, lo = _split2(lw)
        cum = _dot(tri2, jnp.concatenate([hi, lo], axis=0))
        grow = jnp.exp(-cum)
        decay = jnp.exp(cum)
        total = decay[chunk - 1:chunk, :]
        a_t = a * jnp.exp(cum - lw)
        r_t = r * decay
        b_t = b * grow
        k_t = k * grow
        b_end = b_t * total
        k_end = k_t * total
        n_groups = PAIRS // SCAN_GROUP
        pair = lambda x, p: x[:, _pair_lanes(p)]
        grp = lambda x, q: x[:, q * gw:(q + 1) * gw]
        members = lambda q: range(SCAN_GROUP * q, SCAN_GROUP * (q + 1))
        per_group = lambda f: jnp.stack([f(q) for q in range(n_groups)])
        g = _bdot(jnp.stack([jnp.concatenate([split_heads(pair(a_t, p)), split_heads(pair(r_t, p))], axis=0)
                             for p in range(PAIRS)]),
                  jnp.stack([jnp.concatenate([stack(pair(b_t, p)), stack(pair(k_t, p))], axis=0)
                             for p in range(PAIRS)]), _BATCHED_NT)
        l_ab = jnp.where(strict, g[:, :t2, :t2], 0.0)
        l_ak = jnp.where(strict, g[:, :t2, t2:], 0.0)
        l_rb = jnp.where(incl, g[:, t2:, :t2], 0.0)
        l_rk = jnp.where(incl, g[:, t2:, t2:], 0.0)
        diag = lambda m: per_group(lambda q: block_diag([m[p] for p in members(q)]))
        z = z_ref[...]
        applied = _bdot(per_group(lambda q: jnp.concatenate([grp(a_t, q), grp(r_t, q)], axis=0)),
                        z, _BATCHED_NT)
        vv = per_group(lambda q: jnp.concatenate([stack(pair(v, p)) for p in members(q)], axis=0))
        x = per_group(lambda q: jnp.concatenate(
            [stack(applied[q, :chunk, j * LANES:(j + 1) * LANES]) for j in range(SCAN_GROUP)],
            axis=0)) + _bdot(diag(l_ak), vv, _BATCHED)
        power = diag(l_ab)
        for i in range(n_doublings):
            x = x + _bdot(power, x, _BATCHED)
            if i + 1 < n_doublings:
                power = _bdot(power, power, _BATCHED)
        unstack = lambda m, q: jnp.concatenate(
            [fold(m[q, j * t2:(j + 1) * t2]) for j in range(SCAN_GROUP)], axis=1)
        u = [unstack(x, q) for q in range(n_groups)]
        uu = per_group(lambda q: jnp.concatenate(
            [stack(u[q][:, j * LANES:(j + 1) * LANES]) for j in range(SCAN_GROUP)], axis=0))
        ys = _bdot(jnp.concatenate([diag(l_rb), diag(l_rk)], axis=2),
                   jnp.concatenate([uu, vv], axis=1), _BATCHED)
        y_ref[0, rows, :] = jnp.concatenate(
            [applied[q, chunk:] + unstack(ys, q) for q in range(n_groups)], axis=1)
        grown = _bdot(per_group(lambda q: jnp.concatenate([u[q], grp(v, q)], axis=0)),
                      per_group(lambda q: jnp.concatenate([grp(b_end, q), grp(k_end, q)], axis=0)),
                      _BATCHED_TN)
        z_ref[...] = (z * per_group(lambda q: grp(total, q)) + jnp.where(z_diag, grown, 0.0))
        return 0

    lax.fori_loop(0, tc // chunk, one_chunk, 0)

    @pl.when(c == pl.num_programs(1) - 1)
    def _():
        for q in range(PAIRS // SCAN_GROUP):
            z = z_ref[q]
            for h in range(2 * SCAN_GROUP):
                block = slice(h * HEAD_DIM, (h + 1) * HEAD_DIM)
                st_ref[0, 2 * SCAN_GROUP * q + h] = z[block, block]


def _scan(r, lw, k, v, na, kb, s0):
    seqs, length, _ = r.shape
    chunk = min(length, SCAN_CHUNK)
    tc = min(length, SCAN_BLOCK)
    assert length % tc == 0 and tc % chunk == 0 and chunk & (chunk - 1) == 0
    tok = pl.BlockSpec((1, tc, D_MODEL), lambda s, c: (s, c, 0))
    st = pl.BlockSpec((1, HEADS, HEAD_DIM, HEAD_DIM), lambda s, c: (s, 0, 0, 0))
    return pl.pallas_call(
        functools.partial(_scan_kernel, chunk=chunk),
        out_shape=(jax.ShapeDtypeStruct((seqs, length, D_MODEL), F32),
                   jax.ShapeDtypeStruct((seqs, HEADS, HEAD_DIM, HEAD_DIM), F32)),
        grid=(seqs, length // tc),
        in_specs=[tok] * 6 + [st],
        out_specs=(tok, st),
        scratch_shapes=[pltpu.VMEM((PAIRS // SCAN_GROUP, SCAN_GROUP * LANES, SCAN_GROUP * LANES), F32)],
        compiler_params=_cparams("arbitrary", "arbitrary"),
        name="rwkv_scan",
    )(r, lw, k, v, na, kb, s0)


def _rwkv_post_kernel(y_ref, r_ref, k_ref, v_ref, g_ref, lg_ref, lb_ref, rk_ref, o_ref):
    bs, tl, _ = y_ref.shape
    t = bs * tl
    sel2, selt2 = _head_selectors()
    y = y_ref[...].reshape(t, D_MODEL)
    mu = _head_sum(y, sel2, selt2) * (1.0 / HEAD_DIM)
    d = y - mu
    var = _head_sum(d * d, sel2, selt2) * (1.0 / HEAD_DIM)
    yn = d * lax.rsqrt(var + LNX_EPS) * lg_ref[...] + lb_ref[...]
    rk = r_ref[...].reshape(t, D_MODEL) * k_ref[...].reshape(t, D_MODEL) * rk_ref[...]
    bonus = _head_sum(rk, sel2, selt2) * v_ref[...].reshape(t, D_MODEL)
    o_ref[...] = ((yn + bonus) * g_ref[...].reshape(t, D_MODEL)).reshape(bs, tl, D_MODEL)


def _rwkv_post(y, r, k, v, gate, lnx_g, lnx_b, r_k):
    seqs, length, _ = y.shape
    bs, tl = _seq_blocks(seqs, length)
    tok = pl.BlockSpec((bs, tl, D_MODEL), lambda s, i: (s, i, 0))
    vec = pl.BlockSpec((1, D_MODEL), lambda s, i: (0, 0))
    return pl.pallas_call(
        _rwkv_post_kernel,
        out_shape=jax.ShapeDtypeStruct((seqs, length, D_MODEL), F32),
        grid=(seqs // bs, length // tl),
        in_specs=[tok] * 5 + [vec] * 3,
        out_specs=tok,
        compiler_params=_cparams("arbitrary", "arbitrary"),
        name="rwkv_post",
    )(y, r, k, v, gate, lnx_g.reshape(1, D_MODEL), lnx_b.reshape(1, D_MODEL),
      r_k.reshape(1, D_MODEL))


def _trunk(x, mod, p, wb, k_cache, v_cache, shift0, s0):
    seqs, length, _ = x.shape
    part = lambda layer, j: mod[layer, :, j * D_MODEL:(j + 1) * D_MODEL].reshape(seqs, 1, D_MODEL)
    router_wt = p['router_w'].T
    sh1, sc1, g1, sh2, sc2, g2 = [part(0, j) for j in range(6)]
    k, v, qh, ql, kh, kl, vh, vl = _qkv(x, p['norm_mix_g'][0], sc1, sh1, *wb['sb_w_qkv'])
    if k_cache is None:
        attn = _attn_prompt((qh, ql), (kh, kl), (vh, vl))
    else:
        past = k_cache.shape[1]
        attn = _attn_sample((qh, ql), (kh, kl), (vh, vl), k_cache.reshape(seqs, past, D_MODEL),
                            v_cache.reshape(seqs, past, D_MODEL))
    x, h, gates, group = _oproj_router(attn, x, g1, wb['sb_w_o'], p['norm_ffn_g'][0], sc2, sh2,
                                       router_wt, p['router_bias'])
    x = _moe(h, group, gates, x, g2, p['final_norm_g'], wb['exp_w_gate'][0], wb['exp_w_up'][0],
             wb['exp_w_down'][0], final_norm=False)
    sb_k = k.reshape(seqs, length, HEADS, HEAD_DIM)
    sb_v = v.reshape(seqs, length, HEADS, HEAD_DIM)
    sh1, sc1, g1, sh2, sc2, g2 = [part(1, j) for j in range(6)]
    r, w, kq, vv, na, kb, gate, shift = _rwkv_pre(x, shift0, p['norm_mix_g'][1], sc1, sh1, wb)
    y, state = _scan(r, w, kq, vv, na, kb, s0)
    m = _rwkv_post(y, r, kq, vv, gate, p['rw_lnx_g'], p['rw_lnx_b'], p['rw_r_k'])
    x, h, gates, group = _oproj_router(m, x, g1, wb['rw_w_o'], p['norm_ffn_g'][1], sc2, sh2,
                                       router_wt, p['router_bias'])
    y_out = _moe(h, group, gates, x, g2, p['final_norm_g'], wb['exp_w_gate'][1], wb['exp_w_up'][1],
                 wb['exp_w_down'][1], final_norm=True)
    return y_out, sb_k, sb_v, state, shift


def kernel(x_prompt, x_sample, cache_sb_k, cache_sb_v, state_rw_wkv, state_rw_shift,
           c_prompt, c_sample, ada_w, ada_b, norm_mix_g, norm_ffn_g, final_norm_g,
           sb_w_qkv, sb_w_o, rw_mu, rw_w_r, rw_w_k, rw_w_v, rw_w0, rw_w1, rw_w2,
           rw_a0, rw_a1, rw_a2, rw_g1, rw_g2, rw_k_k, rw_k_a, rw_r_k, rw_lnx_g, rw_lnx_b,
           rw_w_o, router_w, router_bias, exp_w_gate, exp_w_up, exp_w_down):
    p = {
        'norm_mix_g': norm_mix_g, 'norm_ffn_g': norm_ffn_g, 'final_norm_g': final_norm_g,
        'rw_r_k': rw_r_k, 'rw_lnx_g': rw_lnx_g, 'rw_lnx_b': rw_lnx_b,
        'router_w': router_w, 'router_bias': router_bias,
    }
    bf = lambda a: a.astype(BF16)
    wb = {
        'sb_w_qkv': _split_weight(sb_w_qkv), 'sb_w_o': _split_weight(sb_w_o),
        'rw_w_o': _split_weight(rw_w_o),
        'rw_mu': rw_mu, 'rw_w_r': bf(rw_w_r), 'rw_w_k': bf(rw_w_k), 'rw_w_v': bf(rw_w_v),
        'rw_w0': rw_w0, 'rw_w1': bf(rw_w1), 'rw_w2': bf(rw_w2), 'rw_a0': rw_a0,
        'rw_a1': bf(rw_a1), 'rw_a2': bf(rw_a2), 'rw_g1': bf(rw_g1), 'rw_g2': bf(rw_g2),
        'rw_k_k': rw_k_k, 'rw_k_a': rw_k_a,
        'exp_w_gate': bf(exp_w_gate), 'exp_w_up': bf(exp_w_up), 'exp_w_down': bf(exp_w_down),
    }
    n_prompt = x_prompt.shape[0]
    mod = _ada(jnp.concatenate([c_prompt, c_sample], axis=0), ada_w, ada_b)
    shift0 = jnp.zeros((n_prompt, 1, D_MODEL), x_prompt.dtype)
    s0 = jnp.zeros((n_prompt, HEADS, HEAD_DIM, HEAD_DIM), F32)
    y_p, k_p, v_p, st_p, sh_p = _trunk(x_prompt, mod[:, :n_prompt], p, wb, None, None, shift0, s0)
    y_s, k_s, v_s, st_s, sh_s = _trunk(x_sample, mod[:, n_prompt:], p, wb, cache_sb_k, cache_sb_v,
                                       state_rw_shift, state_rw_wkv)
    return (y_p, y_s, k_p, v_p, k_s, v_s, st_p, sh_p, st_s, sh_s)
```

```python
import functools

import jax
import jax.numpy as jnp
from jax import lax
from jax.experimental import pallas as pl
from jax.experimental.pallas import tpu as pltpu

F32 = jnp.float32
BF16 = jnp.bfloat16

D_MODEL = 1024
HEADS = 16
HEAD_DIM = 64
LANES = 128
SUBLANES = 8
PAIRS = D_MODEL // LANES
N_EXPERTS = 16
N_GROUPS = 4
EXPERTS_PER_GROUP = 4
D_EXPERT = 512
NORM_EPS = 1e-6
LNX_EPS = 64e-5
ZERO_WEIGHT_CARRY = 104.0
TOKEN_TILE = 512
RWKV_PRE_TILE = 256
QKV_TILE = 256
SCAN_CHUNK = 64
SCAN_BLOCK = 256
SCAN_GROUP = 2
MOE_TILE = 512
VMEM_LIMIT = 56 * 1024 * 1024

_NT = (((1,), (1,)), ((), ()))
_BATCHED = (((2,), (1,)), ((0,), (0,)))
_BATCHED_NT = (((2,), (2,)), ((0,), (0,)))
_BATCHED_TN = (((1,), (1,)), ((0,), (0,)))


def _cparams(*sem):
    return pltpu.CompilerParams(dimension_semantics=sem, vmem_limit_bytes=VMEM_LIMIT)


def _sigmoid(x):
    return 1.0 / (1.0 + jnp.exp(-x))


def _split2(x):
    hi = x.astype(BF16)
    lo = (x - hi.astype(F32)).astype(BF16)
    return hi, lo


def _dot(a, b):
    return jnp.dot(a, b, preferred_element_type=F32)


def _dot2(x, w2):
    hi, lo = _split2(x)
    return _dot(jnp.concatenate([hi, lo], axis=1), w2)


def _norm_mod(x, g, sc, sh):
    ms = jnp.mean(x * x, axis=-1, keepdims=True)
    y = x * lax.rsqrt(ms + NORM_EPS) * g
    return y * (1.0 + sc) + sh


def _seq_blocks(seqs, length, tile=TOKEN_TILE):
    if length >= tile:
        assert length % tile == 0
        return 1, tile
    assert tile % length == 0 and seqs % (tile // length) == 0 and length % 16 == 0
    return tile // length, length


def _ada_kernel(c_ref, w_ref, b_ref, o_ref):
    c = c_ref[...]
    s = c * _sigmoid(c)
    o_ref[0] = jnp.dot(s, w_ref[0], precision=lax.Precision.HIGHEST,
                       preferred_element_type=F32) + b_ref[0]


def _ada(c, ada_w, ada_b):
    rows = c.shape[0]
    depth = ada_w.shape[0]
    return pl.pallas_call(
        _ada_kernel,
        out_shape=jax.ShapeDtypeStruct((depth, rows, 6 * D_MODEL), F32),
        grid=(depth, 6),
        in_specs=[pl.BlockSpec((rows, D_MODEL), lambda l, j: (0, 0)),
                  pl.BlockSpec((1, D_MODEL, D_MODEL), lambda l, j: (l, 0, j)),
                  pl.BlockSpec((1, 1, D_MODEL), lambda l, j: (l, 0, j))],
        out_specs=pl.BlockSpec((1, rows, D_MODEL), lambda l, j: (l, 0, j)),
        compiler_params=_cparams("arbitrary", "arbitrary"),
        name="ada_mod",
    )(c, ada_w, ada_b.reshape(depth, 1, 6 * D_MODEL))


def _dot3(x, w_hi, w_lo):
    hi, lo = _split2(x)
    rows = x.shape[0]
    both = _dot(jnp.concatenate([hi, lo], axis=0), w_hi)
    return both[:rows] + both[rows:] + _dot(hi, w_lo)


def _split_weight(w):
    hi = w.astype(BF16)
    return hi, (w - hi.astype(F32)).astype(BF16)


def _qkv_kernel(x_ref, g_ref, sc_ref, sh_ref, wh_ref, wl_ref, k_ref, v_ref, *piece_refs):
    bs, tl, _ = x_ref.shape
    h = _norm_mod(x_ref[...], g_ref[...], sc_ref[...], sh_ref[...]).reshape(bs * tl, D_MODEL)
    qkv = _dot3(h, wh_ref[...], wl_ref[...])
    q = qkv[:, :D_MODEL] * (HEAD_DIM ** -0.5)
    k = qkv[:, D_MODEL:2 * D_MODEL]
    v = qkv[:, 2 * D_MODEL:]
    k_ref[...] = k.reshape(bs, tl, D_MODEL)
    v_ref[...] = v.reshape(bs, tl, D_MODEL)
    pieces = _split2(q) + _split2(k) + _split2(v)
    for ref, piece in zip(piece_refs, pieces):
        ref[...] = piece.reshape(bs, tl, D_MODEL)


def _qkv(x, g, sc, sh, w_hi, w_lo):
    seqs, length, _ = x.shape
    bs, tl = _seq_blocks(seqs, length, QKV_TILE)
    tok = pl.BlockSpec((bs, tl, D_MODEL), lambda s, i: (s, i, 0))
    mod = pl.BlockSpec((bs, 1, D_MODEL), lambda s, i: (s, 0, 0))
    wspec = pl.BlockSpec((D_MODEL, 3 * D_MODEL), lambda s, i: (0, 0))
    f32o = jax.ShapeDtypeStruct((seqs, length, D_MODEL), F32)
    bfo = jax.ShapeDtypeStruct((seqs, length, D_MODEL), BF16)
    return pl.pallas_call(
        _qkv_kernel,
        out_shape=(f32o, f32o) + (bfo,) * 6,
        grid=(seqs // bs, length // tl),
        in_specs=[tok, pl.BlockSpec((1, 1, D_MODEL), lambda s, i: (0, 0, 0)), mod, mod, wspec, wspec],
        out_specs=(tok,) * 8,
        compiler_params=_cparams("arbitrary", "arbitrary"),
        name="sb_qkv",
    )(x, g.reshape(1, 1, D_MODEL), sc, sh, w_hi, w_lo)


def _pair_masks(rows):
    lane = lax.broadcasted_iota(jnp.int32, (rows, LANES), 1)
    return lane < HEAD_DIM


def _suffix_matrix():
    r = lax.broadcasted_iota(jnp.int32, (LANES, LANES), 0)
    c = lax.broadcasted_iota(jnp.int32, (LANES, LANES), 1)
    u = jnp.where(r > c, 1.0, 0.0).astype(BF16)
    return jnp.concatenate([u, u], axis=0)


def _query_pieces(q_hi, q_lo):
    hi, lo = _stack_heads(q_hi), _stack_heads(q_lo)
    return jnp.concatenate([hi, lo, hi], axis=1)


def _key_pieces(k_hi, k_lo):
    return jnp.concatenate([k_hi, k_hi, k_lo], axis=1)


def _sb_sweep(q3, keys, values, state, u2, visible):
    n = len(q3)
    rows = q3[0].shape[0]
    tq = rows // 2
    wide = lambda x: jnp.concatenate([x[:tq], x[tq:]], axis=1)
    z_all = lax.dot_general(jnp.stack(q3), jnp.stack([_key_pieces(*keys[p]) for p in range(n)]),
                            _BATCHED_NT, preferred_element_type=F32)
    zs, sps = [], []
    for p in range(n):
        z = z_all[p]
        sp = jnp.maximum(z, 0.0) + jnp.log(1.0 + jnp.exp(-jnp.abs(z)))
        if visible is not None:
            sp = jnp.where(visible, sp, 0.0)
        zs.append(z)
        sps.append(sp)
    later_all = _dot2(jnp.concatenate(sps, axis=0), u2)
    weights, vals = [], []
    for p in range(n):
        later = later_all[p * rows:(p + 1) * rows]
        w = jnp.exp(zs[p] - sps[p] - later - state[p][0])
        if visible is not None:
            w = jnp.where(visible, w, 0.0)
        w_hi, w_lo = _split2(w)
        v_hi, v_lo = _stack_heads(values[p][0]), _stack_heads(values[p][1])
        weights.append(jnp.concatenate([wide(w_hi), wide(w_lo), wide(w_hi)], axis=1))
        vals.append(jnp.concatenate([v_hi, v_hi, v_lo], axis=0))
    gained = lax.dot_general(jnp.stack(weights), jnp.stack(vals), _BATCHED,
                             preferred_element_type=F32)
    return tuple((state[p][0] + jnp.sum(sps[p], axis=-1, keepdims=True), state[p][1] + gained[p])
                 for p in range(n))


def _stack_heads(x, axis=0):
    mask_a = _pair_masks(x.shape[0])
    return jnp.concatenate([jnp.where(mask_a, x, 0), jnp.where(mask_a, 0, x)], axis=axis)


def _diag_visible(tq):
    row = lax.broadcasted_iota(jnp.int32, (2 * tq, LANES), 0)
    col = lax.broadcasted_iota(jnp.int32, (2 * tq, LANES), 1)
    qpos = jnp.where(row >= tq, row - tq, row)
    return col < qpos


def _pair_lanes(p):
    return slice(p * LANES, (p + 1) * LANES)


def _sweep_earlier_tiles(n_tiles, sweep, state):
    def cond(c):
        return jnp.logical_and(c[0] < n_tiles, jnp.logical_not(c[1]))

    def body(c):
        st = sweep(n_tiles - 1 - c[0], c[2])
        low = st[0][0]
        for p in range(1, len(st)):
            low = jnp.minimum(low, st[p][0])
        return c[0] + 1, jnp.min(low) > ZERO_WEIGHT_CARRY, st

    return lax.while_loop(cond, body, (jnp.int32(0), False, state))[2]


def _attn_prompt_kernel(qh_ref, ql_ref, kh_ref, kl_ref, vh_ref, vl_ref, o_ref):
    tq = qh_ref.shape[1]
    qi = pl.program_id(1)
    u2 = _suffix_matrix()
    q3 = [_query_pieces(qh_ref[0, :, _pair_lanes(p)], ql_ref[0, :, _pair_lanes(p)])
          for p in range(PAIRS)]

    def sweep(start, state, visible):
        rows = pl.ds(pl.multiple_of(start, tq), tq)
        tile = lambda ref, p: ref[0, rows, _pair_lanes(p)]
        return _sb_sweep(q3, [(tile(kh_ref, p), tile(kl_ref, p)) for p in range(PAIRS)],
                         [(tile(vh_ref, p), tile(vl_ref, p)) for p in range(PAIRS)],
                         state, u2, visible)

    zero = (jnp.zeros((2 * tq, LANES), F32), jnp.zeros((tq, LANES), F32))
    state = sweep(qi * tq, (zero,) * PAIRS, _diag_visible(tq))
    state = _sweep_earlier_tiles(qi, lambda j, st: sweep(j * tq, st, None), state)
    for p in range(PAIRS):
        o_ref[0, :, _pair_lanes(p)] = state[p][1]


def _attn_prompt(q, k, v):
    batch, length, _ = q[0].shape
    tq = LANES
    qspec = pl.BlockSpec((1, tq, D_MODEL), lambda b, i: (b, i, 0))
    kspec = pl.BlockSpec((1, length, D_MODEL), lambda b, i: (b, 0, 0), pipeline_mode=pl.Buffered(1))
    return pl.pallas_call(
        _attn_prompt_kernel,
        out_shape=jax.ShapeDtypeStruct((batch, length, D_MODEL), F32),
        grid=(batch, length // tq),
        in_specs=[qspec, qspec, kspec, kspec, kspec, kspec],
        out_specs=qspec,
        compiler_params=_cparams("arbitrary", "arbitrary"),
        name="sb_attn_prompt",
    )(*q, *k, *v)


def _attn_sample_kernel(qh_ref, ql_ref, kh_ref, kl_ref, vh_ref, vl_ref, kc_ref, vc_ref, o_ref):
    tq = qh_ref.shape[1]
    past = kc_ref.shape[1]
    u2 = _suffix_matrix()
    pad = jnp.zeros((LANES - tq, LANES), BF16)
    padded = lambda ref, p: jnp.concatenate([ref[0, :, _pair_lanes(p)], pad], axis=0)
    q3 = [_query_pieces(qh_ref[0, :, _pair_lanes(p)], ql_ref[0, :, _pair_lanes(p)])
          for p in range(PAIRS)]
    zero = (jnp.zeros((2 * tq, LANES), F32), jnp.zeros((tq, LANES), F32))
    state = _sb_sweep(q3, [(padded(kh_ref, p), padded(kl_ref, p)) for p in range(PAIRS)],
                      [(padded(vh_ref, p), padded(vl_ref, p)) for p in range(PAIRS)],
                      (zero,) * PAIRS, u2, _diag_visible(tq))

    def sweep(j, st):
        rows = pl.ds(pl.multiple_of(j * LANES, LANES), LANES)
        return _sb_sweep(q3, [_split2(kc_ref[0, rows, _pair_lanes(p)]) for p in range(PAIRS)],
                         [_split2(vc_ref[0, rows, _pair_lanes(p)]) for p in range(PAIRS)],
                         st, u2, None)

    state = _sweep_earlier_tiles(past // LANES, sweep, state)
    for p in range(PAIRS):
        o_ref[0, :, _pair_lanes(p)] = state[p][1]


def _attn_sample(q, k, v, k_cache, v_cache):
    batch, length, _ = q[0].shape
    past = k_cache.shape[1]
    assert length <= LANES and past % LANES == 0
    qspec = pl.BlockSpec((1, length, D_MODEL), lambda b: (b, 0, 0))
    cspec = pl.BlockSpec((1, past, D_MODEL), lambda b: (b, 0, 0))
    return pl.pallas_call(
        _attn_sample_kernel,
        out_shape=jax.ShapeDtypeStruct((batch, length, D_MODEL), F32),
        grid=(batch,),
        in_specs=[qspec] * 6 + [cspec, cspec],
        out_specs=qspec,
        compiler_params=_cparams("arbitrary"),
        name="sb_attn_sample",
    )(*q, *k, *v, k_cache, v_cache)


def _row_max(rows):
    m = rows[0]
    for r in rows[1:]:
        m = jnp.maximum(m, r)
    return m


def _first_hit(rows, m):
    hits = []
    taken = None
    for r in rows:
        eq = r == m
        hit = eq if taken is None else jnp.logical_and(eq, jnp.logical_not(taken))
        taken = eq if taken is None else jnp.logical_or(taken, eq)
        hits.append(hit)
    return hits


def _route(s_rows, sel_rows):
    neg = jnp.float32(-jnp.inf)
    top1, top2, score = [], [], []
    for g in range(N_GROUPS):
        rows = sel_rows[g * EXPERTS_PER_GROUP:(g + 1) * EXPERTS_PER_GROUP]
        m1 = _row_max(rows)
        h1 = _first_hit(rows, m1)
        rest = [jnp.where(h, neg, r) for h, r in zip(h1, rows)]
        m2 = _row_max(rest)
        h2 = _first_hit(rest, m2)
        top1.append(h1)
        top2.append(h2)
        score.append(m1 + m2)
    best = _row_max(score)
    ghit = _first_hit(score, best)
    picked = []
    for g in range(N_GROUPS):
        for e in range(EXPERTS_PER_GROUP):
            picked.append(jnp.logical_and(ghit[g], jnp.logical_or(top1[g][e], top2[g][e])))
    total = None
    for p, s in zip(picked, s_rows):
        term = jnp.where(p, s, 0.0)
        total = term if total is None else total + term
    gates = [jnp.where(p, s / total, 0.0) for p, s in zip(picked, s_rows)]
    in_group = []
    for e in range(EXPERTS_PER_GROUP):
        row = gates[e]
        for g in range(1, N_GROUPS):
            row = row + gates[g * EXPERTS_PER_GROUP + e]
        in_group.append(row)
    group = jnp.zeros_like(best)
    for g in range(1, N_GROUPS):
        group = jnp.where(ghit[g], float(g), group)
    return in_group, group


def _oproj_kernel(m_ref, x_ref, gate_ref, wh_ref, wl_ref, g_ref, sc_ref, sh_ref, rw_ref, rb_ref,
                  x1_ref, h_ref, route_ref, *, precise):
    bs, tl, _ = x_ref.shape
    m = m_ref[...].reshape(bs * tl, D_MODEL)
    if precise:
        mix = _dot3(m, wh_ref[...], wl_ref[...])
    else:
        mix = _dot(m.astype(BF16), wh_ref[...])
    mix = mix.reshape(bs, tl, D_MODEL)
    x1 = x_ref[...] + gate_ref[...] * mix
    x1_ref[...] = x1
    h = _norm_mod(x1, g_ref[...], sc_ref[...], sh_ref[...])
    h_ref[...] = h
    logits = lax.dot_general(rw_ref[...], h.reshape(bs * tl, D_MODEL), _NT,
                             precision=lax.Precision.HIGHEST,
                             preferred_element_type=F32)
    s = _sigmoid(logits)
    sel = s + rb_ref[...]
    s_rows = [s[e:e + 1, :] for e in range(N_EXPERTS)]
    sel_rows = [sel[e:e + 1, :] for e in range(N_EXPERTS)]
    in_group, group = _route(s_rows, sel_rows)
    pad = [jnp.zeros_like(group)] * (SUBLANES - EXPERTS_PER_GROUP - 1)
    route_ref[0] = jnp.concatenate(in_group + [group] + pad, axis=0)


def _oproj_router(m, x, gate, w_pieces, g, sc, sh, router_wt, router_b, precise):
    seqs, length, _ = x.shape
    bs, tl = _seq_blocks(seqs, length)
    n_tiles = (seqs // bs) * (length // tl)
    per_seq = length // tl
    tok = pl.BlockSpec((bs, tl, D_MODEL), lambda s, i: (s, i, 0))
    mod = pl.BlockSpec((bs, 1, D_MODEL), lambda s, i: (s, 0, 0))
    one = pl.BlockSpec((1, 1, D_MODEL), lambda s, i: (0, 0, 0))
    wspec = pl.BlockSpec((D_MODEL, D_MODEL), lambda s, i: (0, 0))
    f32o = jax.ShapeDtypeStruct((seqs, length, D_MODEL), F32)
    x1, h, route = pl.pallas_call(
        functools.partial(_oproj_kernel, precise=precise),
        out_shape=(f32o, f32o, jax.ShapeDtypeStruct((n_tiles, SUBLANES, bs * tl), F32)),
        grid=(seqs // bs, per_seq),
        in_specs=[tok, tok, mod, wspec, wspec, one, mod, mod,
                  pl.BlockSpec((N_EXPERTS, D_MODEL), lambda s, i: (0, 0)),
                  pl.BlockSpec((N_EXPERTS, 1), lambda s, i: (0, 0))],
        out_specs=(tok, tok,
                   pl.BlockSpec((1, SUBLANES, bs * tl), lambda s, i: (s * per_seq + i, 0, 0))),
        compiler_params=_cparams("arbitrary", "arbitrary"),
        name="oproj_router",
    )(m, x, gate, *w_pieces, g.reshape(1, 1, D_MODEL), sc, sh, router_wt,
      router_b.reshape(N_EXPERTS, 1))
    route = jnp.swapaxes(route, 1, 2).reshape(seqs * length, SUBLANES)
    return x1, h, route[:, :EXPERTS_PER_GROUP], route[:, EXPERTS_PER_GROUP].astype(jnp.int32)


def _dispatch_plan(group, gates, tile):
    n = group.shape[0]
    n_slots = n + N_GROUPS * tile
    onehot = (group[:, None] == jnp.arange(N_GROUPS)[None, :]).astype(jnp.int32)
    counts = jnp.sum(onehot, axis=0)
    padded = (counts + tile - 1) // tile * tile
    ends = jnp.cumsum(padded)
    starts = ends - padded
    rank = jnp.take_along_axis(jnp.cumsum(onehot, axis=0) - onehot, group[:, None], axis=1)[:, 0]
    slot = starts[group] + rank
    src = jnp.zeros((n_slots,), jnp.int32).at[slot].set(jnp.arange(n, dtype=jnp.int32))
    slot_gates = jnp.zeros((n_slots, EXPERTS_PER_GROUP), F32).at[slot].set(gates)
    tile_start = jnp.arange(n_slots // tile, dtype=jnp.int32) * tile
    tile_group = jnp.minimum(jnp.sum(tile_start[:, None] >= ends[None, :], axis=1), N_GROUPS - 1)
    return src, slot.astype(jnp.int32), slot_gates, tile_group.astype(jnp.int32)


def _start_row_gather(idx_ref, base, src_hbm, dst_ref, sem):
    def body(r, _):
        row = idx_ref[base + r]
        pltpu.make_async_copy(src_hbm.at[pl.ds(row, 1), :], dst_ref.at[pl.ds(r, 1), :], sem).start()
        return 0

    lax.fori_loop(0, dst_ref.shape[0], body, 0, unroll=8)


def _wait_row_gather(src_hbm, dst_ref, sem):
    pltpu.make_async_copy(src_hbm.at[pl.ds(0, dst_ref.shape[0]), :], dst_ref, sem).wait()


def _moe_kernel(src_ref, tg_ref, h_hbm, gates_ref, wg_ref, wu_ref, wd_ref, o_ref, hbuf, sem):
    t = pl.program_id(0)
    tile = o_ref.shape[0]
    slot = t % 2

    @pl.when(t == 0)
    def _():
        _start_row_gather(src_ref, 0, h_hbm, hbuf.at[0], sem.at[0])

    @pl.when(t + 1 < pl.num_programs(0))
    def _():
        _start_row_gather(src_ref, (t + 1) * tile, h_hbm, hbuf.at[1 - slot], sem.at[1 - slot])

    _wait_row_gather(h_hbm, hbuf.at[slot], sem.at[slot])
    h = hbuf[slot].astype(BF16)
    gates = gates_ref[...]
    acc = jnp.zeros((tile, D_MODEL), F32)
    for e in range(EXPERTS_PER_GROUP):
        a = _dot(h, wg_ref[e])
        u = _dot(h, wu_ref[e])
        he = (a * _sigmoid(a) * u).astype(BF16)
        acc = acc + gates[:, e:e + 1] * _dot(he, wd_ref[e])
    o_ref[...] = acc


def _moe_experts(h, src, slot_gates, tile_group, wg_bf, wu_bf, wd_bf, tile):
    n_slots = src.shape[0]
    group_w = lambda shape: pl.BlockSpec((EXPERTS_PER_GROUP,) + shape, lambda t, src, tg: (tg[t], 0, 0))
    return pl.pallas_call(
        _moe_kernel,
        out_shape=jax.ShapeDtypeStruct((n_slots, D_MODEL), F32),
        grid_spec=pltpu.PrefetchScalarGridSpec(
            num_scalar_prefetch=2,
            grid=(n_slots // tile,),
            in_specs=[pl.BlockSpec(memory_space=pl.ANY),
                      pl.BlockSpec((tile, EXPERTS_PER_GROUP), lambda t, src, tg: (t, 0)),
                      group_w((D_MODEL, D_EXPERT)), group_w((D_MODEL, D_EXPERT)),
                      group_w((D_EXPERT, D_MODEL))],
            out_specs=pl.BlockSpec((tile, D_MODEL), lambda t, src, tg: (t, 0)),
            scratch_shapes=[pltpu.VMEM((2, tile, D_MODEL), F32), pltpu.SemaphoreType.DMA((2,))]),
        compiler_params=_cparams("arbitrary"),
        name="moe_experts",
    )(src, tile_group, h, slot_gates, wg_bf, wu_bf, wd_bf)


def _moe_combine_kernel(slot_ref, f_hbm, x_ref, gate_ref, fg_ref, o_ref, fbuf, sem, *, final_norm):
    bs, tl, _ = x_ref.shape
    tile = bs * tl
    t = pl.program_id(0) * pl.num_programs(1) + pl.program_id(1)
    n_tiles = pl.num_programs(0) * pl.num_programs(1)
    slot = t % 2

    @pl.when(t == 0)
    def _():
        _start_row_gather(slot_ref, 0, f_hbm, fbuf.at[0], sem.at[0])

    @pl.when(t + 1 < n_tiles)
    def _():
        _start_row_gather(slot_ref, (t + 1) * tile, f_hbm, fbuf.at[1 - slot], sem.at[1 - slot])

    _wait_row_gather(f_hbm, fbuf.at[slot], sem.at[slot])
    x2 = x_ref[...] + gate_ref[...] * fbuf[slot].reshape(bs, tl, D_MODEL)
    if final_norm:
        ms = jnp.mean(x2 * x2, axis=-1, keepdims=True)
        x2 = x2 * lax.rsqrt(ms + NORM_EPS) * fg_ref[...]
    o_ref[...] = x2


def _moe_combine(ffn_sorted, slot, x, gate, final_g, final_norm):
    seqs, length, _ = x.shape
    bs, tl = _seq_blocks(seqs, length)
    tok = pl.BlockSpec((bs, tl, D_MODEL), lambda s, i, slot: (s, i, 0))
    return pl.pallas_call(
        functools.partial(_moe_combine_kernel, final_norm=final_norm),
        out_shape=jax.ShapeDtypeStruct((seqs, length, D_MODEL), F32),
        grid_spec=pltpu.PrefetchScalarGridSpec(
            num_scalar_prefetch=1,
            grid=(seqs // bs, length // tl),
            in_specs=[pl.BlockSpec(memory_space=pl.ANY), tok,
                      pl.BlockSpec((bs, 1, D_MODEL), lambda s, i, slot: (s, 0, 0)),
                      pl.BlockSpec((1, 1, D_MODEL), lambda s, i, slot: (0, 0, 0))],
            out_specs=tok,
            scratch_shapes=[pltpu.VMEM((2, bs * tl, D_MODEL), F32), pltpu.SemaphoreType.DMA((2,))]),
        compiler_params=_cparams("arbitrary", "arbitrary"),
        name="moe_combine",
    )(slot, ffn_sorted, x, gate, final_g.reshape(1, 1, D_MODEL))


def _moe(h, group, gates, x, gate, final_g, wg_bf, wu_bf, wd_bf, final_norm):
    seqs, length, _ = x.shape
    tile = min(MOE_TILE, max(LANES, seqs * length // (2 * N_GROUPS)))
    src, slot, slot_gates, tile_group = _dispatch_plan(group, gates, tile)
    ffn_sorted = _moe_experts(h.reshape(seqs * length, D_MODEL), src, slot_gates, tile_group,
                              wg_bf, wu_bf, wd_bf, tile)
    return _moe_combine(ffn_sorted, slot, x, gate, final_g, final_norm)


def _head_selectors():
    r = lax.broadcasted_iota(jnp.int32, (D_MODEL, LANES), 0)
    c = lax.broadcasted_iota(jnp.int32, (D_MODEL, LANES), 1)
    sel = jnp.where(r // HEAD_DIM == c, 1.0, 0.0).astype(BF16)
    rt = lax.broadcasted_iota(jnp.int32, (LANES, D_MODEL), 0)
    ct = lax.broadcasted_iota(jnp.int32, (LANES, D_MODEL), 1)
    selt = jnp.where(ct // HEAD_DIM == rt, 1.0, 0.0).astype(BF16)
    return jnp.concatenate([sel, sel], axis=0), jnp.concatenate([selt, selt], axis=0)


def _head_sum(x, sel2, selt2):
    return _dot2(_dot2(x, sel2), selt2)


def _rwkv_pre_kernel(x_ref, prev_ref, shift_ref, g_ref, sc_ref, sh_ref, mu_ref,
                     wr_ref, wk_ref, wv_ref, w0_ref, w1_ref, w2_ref, a0_ref, a1_ref, a2_ref,
                     g1_ref, g2_ref, kk_ref, ka_ref,
                     r_ref, w_ref, k_ref, v_ref, na_ref, kb_ref, go_ref, so_ref):
    bs, tl, _ = x_ref.shape
    t = bs * tl
    i = pl.program_id(1)
    g, sc, sh = g_ref[...], sc_ref[...], sh_ref[...]
    h = _norm_mod(x_ref[...], g, sc, sh)
    so_ref[...] = h[:, tl - 1:tl, :]
    h_before = _norm_mod(prev_ref[:, 7:8, :], g, sc, sh)
    first = jnp.where(i == 0, shift_ref[...], h_before)
    row = lax.broadcasted_iota(jnp.int32, (bs, tl, D_MODEL), 1)
    prev = jnp.where(row == 0, first, pltpu.roll(h, 1, 1))
    h = h.reshape(t, D_MODEL)
    xx = prev.reshape(t, D_MODEL) - h
    mixed = [(h + xx * mu_ref[j:j + 1, :]).astype(BF16) for j in range(6)]
    xr, xw, xk, xv, xa, xg = mixed
    r = _dot(xr, wr_ref[...])
    k = _dot(xk, wk_ref[...])
    v = _dot(xv, wv_ref[...])
    wl = w0_ref[...] + _dot(jnp.tanh(_dot(xw, w1_ref[...])).astype(BF16), w2_ref[...])
    neg = -wl
    softplus = jnp.maximum(neg, 0.0) + jnp.log(1.0 + jnp.exp(-jnp.abs(neg)))
    w = -softplus - 0.5
    log_decay = -jnp.exp(w)
    a = _sigmoid(a0_ref[...] + _dot(_dot(xa, a1_ref[...]).astype(BF16), a2_ref[...]))
    gate = _dot(_sigmoid(_dot(xg, g1_ref[...])).astype(BF16), g2_ref[...])
    sel2, selt2 = _head_selectors()
    kk = k * kk_ref[...]
    norm = jnp.maximum(jnp.sqrt(_head_sum(kk * kk, sel2, selt2)), 1e-12)
    kk = kk / norm
    k = k * (1.0 + (a - 1.0) * ka_ref[...])
    shape = (bs, tl, D_MODEL)
    r_ref[...] = r.reshape(shape)
    w_ref[...] = log_decay.reshape(shape)
    k_ref[...] = k.reshape(shape)
    v_ref[...] = v.reshape(shape)
    na_ref[...] = (-kk).reshape(shape)
    kb_ref[...] = (kk * a).reshape(shape)
    go_ref[...] = gate.reshape(shape)


def _rwkv_pre(x, shift0, g, sc, sh, p):
    seqs, length, _ = x.shape
    bs, tl = _seq_blocks(seqs, length, RWKV_PRE_TILE)
    tok = pl.BlockSpec((bs, tl, D_MODEL), lambda s, i: (s, i, 0))
    prev = pl.BlockSpec((bs, 8, D_MODEL), lambda s, i: (s, jnp.maximum(i * (tl // 8) - 1, 0), 0))
    mod = pl.BlockSpec((bs, 1, D_MODEL), lambda s, i: (s, 0, 0))
    one = pl.BlockSpec((1, 1, D_MODEL), lambda s, i: (0, 0, 0))

    def full(a):
        return pl.BlockSpec(a.shape, lambda s, i: (0,) * a.ndim)

    vec = lambda a: a.reshape(1, D_MODEL)
    weights = [p['rw_mu'], p['rw_w_r'], p['rw_w_k'], p['rw_w_v'], vec(p['rw_w0']), p['rw_w1'],
               p['rw_w2'], vec(p['rw_a0']), p['rw_a1'], p['rw_a2'], p['rw_g1'], p['rw_g2'],
               vec(p['rw_k_k']), vec(p['rw_k_a'])]
    f32o = jax.ShapeDtypeStruct((seqs, length, D_MODEL), F32)
    outs = pl.pallas_call(
        _rwkv_pre_kernel,
        out_shape=(f32o,) * 7 + (jax.ShapeDtypeStruct((seqs, 1, D_MODEL), F32),),
        grid=(seqs // bs, length // tl),
        in_specs=[tok, prev, mod, one, mod, mod] + [full(a) for a in weights],
        out_specs=(tok,) * 7 + (mod,),
        compiler_params=_cparams("arbitrary", "arbitrary"),
        name="rwkv_pre",
    )(x, x, shift0, g.reshape(1, 1, D_MODEL), sc, sh, *weights)
    return outs


def _bdot(a, b, dims=None):
    a, b = a.astype(BF16), b.astype(BF16)
    if dims is None:
        return jnp.dot(a, b, preferred_element_type=F32)
    return lax.dot_general(a, b, dims, preferred_element_type=F32)


def _scan_kernel(r_ref, lw_ref, k_ref, v_ref, a_ref, b_ref, s0_ref, y_ref, st_ref, z_ref, *, chunk):
    tc = r_ref.shape[1]
    t2 = 2 * chunk
    c = pl.program_id(1)
    lane_a = _pair_masks(chunk)
    fold = lambda x: jnp.where(lane_a, x[:chunk], x[chunk:])
    stack = lambda x: jnp.concatenate([x, x], axis=0)
    split_heads = lambda x: _stack_heads(x, axis=0)

    def block_diag(mats):
        n = mats[0].shape[0]
        zero = jnp.zeros((n, n), F32)
        return jnp.concatenate(
            [jnp.concatenate([m if j == i else zero for j in range(len(mats))], axis=1)
             for i, m in enumerate(mats)], axis=0)

    @pl.when(c == 0)
    def _():
        for q in range(PAIRS // SCAN_GROUP):
            z_ref[q] = block_diag([s0_ref[0, 2 * SCAN_GROUP * q + h] for h in range(2 * SCAN_GROUP)])

    row = lax.broadcasted_iota(jnp.int32, (t2, t2), 0)
    col = lax.broadcasted_iota(jnp.int32, (t2, t2), 1)
    same_head = (row // chunk) == (col // chunk)
    strict = jnp.logical_and(same_head, (row % chunk) > (col % chunk))
    incl = jnp.logical_and(same_head, (row % chunk) >= (col % chunk))
    tr = lax.broadcasted_iota(jnp.int32, (chunk, chunk), 0)
    tcol = lax.broadcasted_iota(jnp.int32, (chunk, chunk), 1)
    tri = jnp.where(tr >= tcol, 1.0, 0.0).astype(BF16)
    tri2 = jnp.concatenate([tri, tri], axis=1)
    gw = SCAN_GROUP * LANES
    zr = lax.broadcasted_iota(jnp.int32, (gw, gw), 0)
    zc = lax.broadcasted_iota(jnp.int32, (gw, gw), 1)
    z_diag = (zr // HEAD_DIM) == (zc // HEAD_DIM)
    n_doublings = chunk.bit_length() - 1

    def one_chunk(ci, _):
        rows = pl.ds(pl.multiple_of(ci * chunk, chunk), chunk)
        lw, r, k, v, a, b = [ref[0, rows, :] for ref in (lw_ref, r_ref, k_ref, v_ref, a_ref, b_ref)]
        hi, lo = _split2(lw)
        cum = _dot(tri2, jnp.concatenate([hi, lo], axis=0))
        grow = jnp.exp(-cum)
        decay = jnp.exp(cum)
        total = decay[chunk - 1:chunk, :]
        a_t = a * jnp.exp(cum - lw)
        r_t = r * decay
        b_t = b * grow
        k_t = k * grow
        b_end = b_t * total
        k_end = k_t * total
        n_groups = PAIRS // SCAN_GROUP
        pair = lambda x, p: x[:, _pair_lanes(p)]
        grp = lambda x, q: x[:, q * gw:(q + 1) * gw]
        members = lambda q: range(SCAN_GROUP * q, SCAN_GROUP * (q + 1))
        per_group = lambda f: jnp.stack([f(q) for q in range(n_groups)])
        g = _bdot(jnp.stack([jnp.concatenate([split_heads(pair(a_t, p)), split_heads(pair(r_t, p))], axis=0)
                             for p in range(PAIRS)]),
                  jnp.stack([jnp.concatenate([stack(pair(b_t, p)), stack(pair(k_t, p))], axis=0)
                             for p in range(PAIRS)]), _BATCHED_NT)
        l_ab = jnp.where(strict, g[:, :t2, :t2], 0.0)
        l_ak = jnp.where(strict, g[:, :t2, t2:], 0.0)
        l_rb = jnp.where(incl, g[:, t2:, :t2], 0.0)
        l_rk = jnp.where(incl, g[:, t2:, t2:], 0.0)
        diag = lambda m: per_group(lambda q: block_diag([m[p] for p in members(q)]))
        z = z_ref[...]
        applied = _bdot(per_group(lambda q: jnp.concatenate([grp(a_t, q), grp(r_t, q)], axis=0)),
                        z, _BATCHED_NT)
        vv = per_group(lambda q: jnp.concatenate([stack(pair(v, p)) for p in members(q)], axis=0))
        x = per_group(lambda q: jnp.concatenate(
            [stack(applied[q, :chunk, j * LANES:(j + 1) * LANES]) for j in range(SCAN_GROUP)],
            axis=0)) + _bdot(diag(l_ak), vv, _BATCHED)
        power = diag(l_ab)
        for i in range(n_doublings):
            x = x + _bdot(power, x, _BATCHED)
            if i + 1 < n_doublings:
                power = _bdot(power, power, _BATCHED)
        unstack = lambda m, q: jnp.concatenate(
            [fold(m[q, j * t2:(j + 1) * t2]) for j in range(SCAN_GROUP)], axis=1)
        u = [unstack(x, q) for q in range(n_groups)]
        uu = per_group(lambda q: jnp.concatenate(
            [stack(u[q][:, j * LANES:(j + 1) * LANES]) for j in range(SCAN_GROUP)], axis=0))
        ys = _bdot(jnp.concatenate([diag(l_rb), diag(l_rk)], axis=2),
                   jnp.concatenate([uu, vv], axis=1), _BATCHED)
        y_ref[0, rows, :] = jnp.concatenate(
            [applied[q, chunk:] + unstack(ys, q) for q in range(n_groups)], axis=1)
        grown = _bdot(per_group(lambda q: jnp.concatenate([u[q], grp(v, q)], axis=0)),
                      per_group(lambda q: jnp.concatenate([grp(b_end, q), grp(k_end, q)], axis=0)),
                      _BATCHED_TN)
        z_ref[...] = (z * per_group(lambda q: grp(total, q)) + jnp.where(z_diag, grown, 0.0))
        return 0

    lax.fori_loop(0, tc // chunk, one_chunk, 0)

    @pl.when(c == pl.num_programs(1) - 1)
    def _():
        for q in range(PAIRS // SCAN_GROUP):
            z = z_ref[q]
            for h in range(2 * SCAN_GROUP):
                block = slice(h * HEAD_DIM, (h + 1) * HEAD_DIM)
                st_ref[0, 2 * SCAN_GROUP * q + h] = z[block, block]


def _scan(r, lw, k, v, na, kb, s0):
    seqs, length, _ = r.shape
    chunk = min(length, SCAN_CHUNK)
    tc = min(length, SCAN_BLOCK)
    assert length % tc == 0 and tc % chunk == 0 and chunk & (chunk - 1) == 0
    tok = pl.BlockSpec((1, tc, D_MODEL), lambda s, c: (s, c, 0))
    st = pl.BlockSpec((1, HEADS, HEAD_DIM, HEAD_DIM), lambda s, c: (s, 0, 0, 0))
    return pl.pallas_call(
        functools.partial(_scan_kernel, chunk=chunk),
        out_shape=(jax.ShapeDtypeStruct((seqs, length, D_MODEL), F32),
                   jax.ShapeDtypeStruct((seqs, HEADS, HEAD_DIM, HEAD_DIM), F32)),
        grid=(seqs, length // tc),
        in_specs=[tok] * 6 + [st],
        out_specs=(tok, st),
        scratch_shapes=[pltpu.VMEM((PAIRS // SCAN_GROUP, SCAN_GROUP * LANES, SCAN_GROUP * LANES), F32)],
        compiler_params=_cparams("arbitrary", "arbitrary"),
        name="rwkv_scan",
    )(r, lw, k, v, na, kb, s0)


def _rwkv_post_kernel(y_ref, r_ref, k_ref, v_ref, g_ref, lg_ref, lb_ref, rk_ref, o_ref):
    bs, tl, _ = y_ref.shape
    t = bs * tl
    sel2, selt2 = _head_selectors()
    y = y_ref[...].reshape(t, D_MODEL)
    mu = _head_sum(y, sel2, selt2) * (1.0 / HEAD_DIM)
    d = y - mu
    var = _head_sum(d * d, sel2, selt2) * (1.0 / HEAD_DIM)
    yn = d * lax.rsqrt(var + LNX_EPS) * lg_ref[...] + lb_ref[...]
    rk = r_ref[...].reshape(t, D_MODEL) * k_ref[...].reshape(t, D_MODEL) * rk_ref[...]
    bonus = _head_sum(rk, sel2, selt2) * v_ref[...].reshape(t, D_MODEL)
    o_ref[...] = ((yn + bonus) * g_ref[...].reshape(t, D_MODEL)).reshape(bs, tl, D_MODEL)


def _rwkv_post(y, r, k, v, gate, lnx_g, lnx_b, r_k):
    seqs, length, _ = y.shape
    bs, tl = _seq_blocks(seqs, length)
    tok = pl.BlockSpec((bs, tl, D_MODEL), lambda s, i: (s, i, 0))
    vec = pl.BlockSpec((1, D_MODEL), lambda s, i: (0, 0))
    return pl.pallas_call(
        _rwkv_post_kernel,
        out_shape=jax.ShapeDtypeStruct((seqs, length, D_MODEL), F32),
        grid=(seqs // bs, length // tl),
        in_specs=[tok] * 5 + [vec] * 3,
        out_specs=tok,
        compiler_params=_cparams("arbitrary", "arbitrary"),
        name="rwkv_post",
    )(y, r, k, v, gate, lnx_g.reshape(1, D_MODEL), lnx_b.reshape(1, D_MODEL),
      r_k.reshape(1, D_MODEL))


def _trunk(x, mod, p, wb, k_cache, v_cache, shift0, s0):
    seqs, length, _ = x.shape
    part = lambda layer, j: mod[layer, :, j * D_MODEL:(j + 1) * D_MODEL].reshape(seqs, 1, D_MODEL)
    router_wt = p['router_w'].T
    sh1, sc1, g1, sh2, sc2, g2 = [part(0, j) for j in range(6)]
    k, v, qh, ql, kh, kl, vh, vl = _qkv(x, p['norm_mix_g'][0], sc1, sh1, *wb['sb_w_qkv'])
    if k_cache is None:
        attn = _attn_prompt((qh, ql), (kh, kl), (vh, vl))
    else:
        past = k_cache.shape[1]
        attn = _attn_sample((qh, ql), (kh, kl), (vh, vl), k_cache.reshape(seqs, past, D_MODEL),
                            v_cache.reshape(seqs, past, D_MODEL))
    x, h, gates, group = _oproj_router(attn, x, g1, wb['sb_w_o'], p['norm_ffn_g'][0], sc2, sh2,
                                       router_wt, p['router_bias'], precise=True)
    x = _moe(h, group, gates, x, g2, p['final_norm_g'], wb['exp_w_gate'][0], wb['exp_w_up'][0],
             wb['exp_w_down'][0], final_norm=False)
    sb_k = k.reshape(seqs, length, HEADS, HEAD_DIM)
    sb_v = v.reshape(seqs, length, HEADS, HEAD_DIM)
    sh1, sc1, g1, sh2, sc2, g2 = [part(1, j) for j in range(6)]
    r, w, kq, vv, na, kb, gate, shift = _rwkv_pre(x, shift0, p['norm_mix_g'][1], sc1, sh1, wb)
    y, state = _scan(r, w, kq, vv, na, kb, s0)
    m = _rwkv_post(y, r, kq, vv, gate, p['rw_lnx_g'], p['rw_lnx_b'], p['rw_r_k'])
    x, h, gates, group = _oproj_router(m, x, g1, wb['rw_w_o'], p['norm_ffn_g'][1], sc2, sh2,
                                       router_wt, p['router_bias'], precise=False)
    y_out = _moe(h, group, gates, x, g2, p['final_norm_g'], wb['exp_w_gate'][1], wb['exp_w_up'][1],
                 wb['exp_w_down'][1], final_norm=True)
    return y_out, sb_k, sb_v, state, shift


def kernel(x_prompt, x_sample, cache_sb_k, cache_sb_v, state_rw_wkv, state_rw_shift,
           c_prompt, c_sample, ada_w, ada_b, norm_mix_g, norm_ffn_g, final_norm_g,
           sb_w_qkv, sb_w_o, rw_mu, rw_w_r, rw_w_k, rw_w_v, rw_w0, rw_w1, rw_w2,
           rw_a0, rw_a1, rw_a2, rw_g1, rw_g2, rw_k_k, rw_k_a, rw_r_k, rw_lnx_g, rw_lnx_b,
           rw_w_o, router_w, router_bias, exp_w_gate, exp_w_up, exp_w_down):
    p = {
        'norm_mix_g': norm_mix_g, 'norm_ffn_g': norm_ffn_g, 'final_norm_g': final_norm_g,
        'rw_r_k': rw_r_k, 'rw_lnx_g': rw_lnx_g, 'rw_lnx_b': rw_lnx_b,
        'router_w': router_w, 'router_bias': router_bias,
    }
    bf = lambda a: a.astype(BF16)
    wb = {
        'sb_w_qkv': _split_weight(sb_w_qkv), 'sb_w_o': _split_weight(sb_w_o),
        'rw_w_o': _split_weight(rw_w_o),
        'rw_mu': rw_mu, 'rw_w_r': bf(rw_w_r), 'rw_w_k': bf(rw_w_k), 'rw_w_v': bf(rw_w_v),
        'rw_w0': rw_w0, 'rw_w1': bf(rw_w1), 'rw_w2': bf(rw_w2), 'rw_a0': rw_a0,
        'rw_a1': bf(rw_a1), 'rw_a2': bf(rw_a2), 'rw_g1': bf(rw_g1), 'rw_g2': bf(rw_g2),
        'rw_k_k': rw_k_k, 'rw_k_a': rw_k_a,
        'exp_w_gate': bf(exp_w_gate), 'exp_w_up': bf(exp_w_up), 'exp_w_down': bf(exp_w_down),
    }
    n_prompt = x_prompt.shape[0]
    mod = _ada(jnp.concatenate([c_prompt, c_sample], axis=0), ada_w, ada_b)
    shift0 = jnp.zeros((n_prompt, 1, D_MODEL), x_prompt.dtype)
    s0 = jnp.zeros((n_prompt, HEADS, HEAD_DIM, HEAD_DIM), F32)
    y_p, k_p, v_p, st_p, sh_p = _trunk(x_prompt, mod[:, :n_prompt], p, wb, None, None, shift0, s0)
    y_s, k_s, v_s, st_s, sh_s = _trunk(x_sample, mod[:, n_prompt:], p, wb, cache_sb_k, cache_sb_v,
                                       state_rw_shift, state_rw_wkv)
    return (y_p, y_s, k_p, v_p, k_s, v_s, st_p, sh_p, st_s, sh_s)
```

```python
import functools

import jax
import jax.numpy as jnp
from jax import lax
from jax.experimental import pallas as pl
from jax.experimental.pallas import tpu as pltpu

F32 = jnp.float32
BF16 = jnp.bfloat16

D_MODEL = 1024
HEADS = 16
HEAD_DIM = 64
LANES = 128
SUBLANES = 8
PAIRS = D_MODEL // LANES
N_EXPERTS = 16
N_GROUPS = 4
EXPERTS_PER_GROUP = 4
D_EXPERT = 512
NORM_EPS = 1e-6
LNX_EPS = 64e-5
ZERO_WEIGHT_CARRY = 104.0
TOKEN_TILE = 512
RWKV_PRE_TILE = 256
QKV_TILE = 256
SCAN_CHUNK = 64
SCAN_BLOCK = 256
SCAN_GROUP = 2
MOE_TILE = 512
VMEM_LIMIT = 56 * 1024 * 1024

_NT = (((1,), (1,)), ((), ()))
_BATCHED = (((2,), (1,)), ((0,), (0,)))
_BATCHED_NT = (((2,), (2,)), ((0,), (0,)))
_BATCHED_TN = (((1,), (1,)), ((0,), (0,)))


def _cparams(*sem):
    return pltpu.CompilerParams(dimension_semantics=sem, vmem_limit_bytes=VMEM_LIMIT)


def _sigmoid(x):
    return 1.0 / (1.0 + jnp.exp(-x))


def _split2(x):
    hi = x.astype(BF16)
    lo = (x - hi.astype(F32)).astype(BF16)
    return hi, lo


def _dot(a, b):
    return jnp.dot(a, b, preferred_element_type=F32)


def _dot2(x, w2):
    hi, lo = _split2(x)
    return _dot(jnp.concatenate([hi, lo], axis=1), w2)


def _norm_mod(x, g, sc, sh):
    ms = jnp.mean(x * x, axis=-1, keepdims=True)
    y = x * lax.rsqrt(ms + NORM_EPS) * g
    return y * (1.0 + sc) + sh


def _seq_blocks(seqs, length, tile=TOKEN_TILE):
    if length >= tile:
        assert length % tile == 0
        return 1, tile
    assert tile % length == 0 and seqs % (tile // length) == 0 and length % 16 == 0
    return tile // length, length


def _ada_kernel(c_ref, w_ref, b_ref, o_ref):
    c = c_ref[...]
    s = c * _sigmoid(c)
    o_ref[0] = jnp.dot(s, w_ref[0], precision=lax.Precision.HIGHEST,
                       preferred_element_type=F32) + b_ref[0]


def _ada(c, ada_w, ada_b):
    rows = c.shape[0]
    depth = ada_w.shape[0]
    return pl.pallas_call(
        _ada_kernel,
        out_shape=jax.ShapeDtypeStruct((depth, rows, 6 * D_MODEL), F32),
        grid=(depth, 6),
        in_specs=[pl.BlockSpec((rows, D_MODEL), lambda l, j: (0, 0)),
                  pl.BlockSpec((1, D_MODEL, D_MODEL), lambda l, j: (l, 0, j)),
                  pl.BlockSpec((1, 1, D_MODEL), lambda l, j: (l, 0, j))],
        out_specs=pl.BlockSpec((1, rows, D_MODEL), lambda l, j: (l, 0, j)),
        compiler_params=_cparams("arbitrary", "arbitrary"),
        name="ada_mod",
    )(c, ada_w, ada_b.reshape(depth, 1, 6 * D_MODEL))


def _dot3(x, w_hi, w_lo):
    hi, lo = _split2(x)
    rows = x.shape[0]
    both = _dot(jnp.concatenate([hi, lo], axis=0), w_hi)
    return both[:rows] + both[rows:] + _dot(hi, w_lo)


def _split_weight(w):
    hi = w.astype(BF16)
    return hi, (w - hi.astype(F32)).astype(BF16)


def _qkv_kernel(x_ref, g_ref, sc_ref, sh_ref, wh_ref, wl_ref, k_ref, v_ref, *piece_refs):
    bs, tl, _ = x_ref.shape
    h = _norm_mod(x_ref[...], g_ref[...], sc_ref[...], sh_ref[...]).reshape(bs * tl, D_MODEL)
    qkv = _dot3(h, wh_ref[...], wl_ref[...])
    q = qkv[:, :D_MODEL] * (HEAD_DIM ** -0.5)
    k = qkv[:, D_MODEL:2 * D_MODEL]
    v = qkv[:, 2 * D_MODEL:]
    k_ref[...] = k.reshape(bs, tl, D_MODEL)
    v_ref[...] = v.reshape(bs, tl, D_MODEL)
    pieces = _split2(q) + _split2(k) + _split2(v)
    for ref, piece in zip(piece_refs, pieces):
        ref[...] = piece.reshape(bs, tl, D_MODEL)


def _qkv(x, g, sc, sh, w_hi, w_lo):
    seqs, length, _ = x.shape
    bs, tl = _seq_blocks(seqs, length, QKV_TILE)
    tok = pl.BlockSpec((bs, tl, D_MODEL), lambda s, i: (s, i, 0))
    mod = pl.BlockSpec((bs, 1, D_MODEL), lambda s, i: (s, 0, 0))
    wspec = pl.BlockSpec((D_MODEL, 3 * D_MODEL), lambda s, i: (0, 0))
    f32o = jax.ShapeDtypeStruct((seqs, length, D_MODEL), F32)
    bfo = jax.ShapeDtypeStruct((seqs, length, D_MODEL), BF16)
    return pl.pallas_call(
        _qkv_kernel,
        out_shape=(f32o, f32o) + (bfo,) * 6,
        grid=(seqs // bs, length // tl),
        in_specs=[tok, pl.BlockSpec((1, 1, D_MODEL), lambda s, i: (0, 0, 0)), mod, mod, wspec, wspec],
        out_specs=(tok,) * 8,
        compiler_params=_cparams("arbitrary", "arbitrary"),
        name="sb_qkv",
    )(x, g.reshape(1, 1, D_MODEL), sc, sh, w_hi, w_lo)


def _pair_masks(rows):
    lane = lax.broadcasted_iota(jnp.int32, (rows, LANES), 1)
    return lane < HEAD_DIM


def _suffix_matrix():
    r = lax.broadcasted_iota(jnp.int32, (LANES, LANES), 0)
    c = lax.broadcasted_iota(jnp.int32, (LANES, LANES), 1)
    u = jnp.where(r > c, 1.0, 0.0).astype(BF16)
    return jnp.concatenate([u, u], axis=0)


def _query_pieces(q_hi, q_lo):
    hi, lo = _stack_heads(q_hi), _stack_heads(q_lo)
    return jnp.concatenate([hi, lo, hi], axis=1)


def _key_pieces(k_hi, k_lo):
    return jnp.concatenate([k_hi, k_hi, k_lo], axis=1)


def _sb_sweep(q3, keys, values, state, u2, visible):
    n = len(q3)
    rows = q3[0].shape[0]
    tq = rows // 2
    wide = lambda x: jnp.concatenate([x[:tq], x[tq:]], axis=1)
    z_all = lax.dot_general(jnp.stack(q3), jnp.stack([_key_pieces(*keys[p]) for p in range(n)]),
                            _BATCHED_NT, preferred_element_type=F32)
    zs, sps = [], []
    for p in range(n):
        z = z_all[p]
        sp = jnp.maximum(z, 0.0) + jnp.log(1.0 + jnp.exp(-jnp.abs(z)))
        if visible is not None:
            sp = jnp.where(visible, sp, 0.0)
        zs.append(z)
        sps.append(sp)
    later_all = _dot2(jnp.concatenate(sps, axis=0), u2)
    weights, vals = [], []
    for p in range(n):
        later = later_all[p * rows:(p + 1) * rows]
        w = jnp.exp(zs[p] - sps[p] - later - state[p][0])
        if visible is not None:
            w = jnp.where(visible, w, 0.0)
        w_hi, w_lo = _split2(w)
        v_hi, v_lo = _stack_heads(values[p][0]), _stack_heads(values[p][1])
        weights.append(jnp.concatenate([wide(w_hi), wide(w_lo), wide(w_hi)], axis=1))
        vals.append(jnp.concatenate([v_hi, v_hi, v_lo], axis=0))
    gained = lax.dot_general(jnp.stack(weights), jnp.stack(vals), _BATCHED,
                             preferred_element_type=F32)
    return tuple((state[p][0] + jnp.sum(sps[p], axis=-1, keepdims=True), state[p][1] + gained[p])
                 for p in range(n))


def _stack_heads(x, axis=0):
    mask_a = _pair_masks(x.shape[0])
    return jnp.concatenate([jnp.where(mask_a, x, 0), jnp.where(mask_a, 0, x)], axis=axis)


def _diag_visible(tq):
    row = lax.broadcasted_iota(jnp.int32, (2 * tq, LANES), 0)
    col = lax.broadcasted_iota(jnp.int32, (2 * tq, LANES), 1)
    qpos = jnp.where(row >= tq, row - tq, row)
    return col < qpos


def _pair_lanes(p):
    return slice(p * LANES, (p + 1) * LANES)


def _sweep_earlier_tiles(n_tiles, sweep, state):
    def cond(c):
        return jnp.logical_and(c[0] < n_tiles, jnp.logical_not(c[1]))

    def body(c):
        st = sweep(n_tiles - 1 - c[0], c[2])
        low = st[0][0]
        for p in range(1, len(st)):
            low = jnp.minimum(low, st[p][0])
        return c[0] + 1, jnp.min(low) > ZERO_WEIGHT_CARRY, st

    return lax.while_loop(cond, body, (jnp.int32(0), False, state))[2]


def _attn_prompt_kernel(qh_ref, ql_ref, kh_ref, kl_ref, vh_ref, vl_ref, o_ref):
    tq = qh_ref.shape[1]
    qi = pl.program_id(1)
    u2 = _suffix_matrix()
    q3 = [_query_pieces(qh_ref[0, :, _pair_lanes(p)], ql_ref[0, :, _pair_lanes(p)])
          for p in range(PAIRS)]

    def sweep(start, state, visible):
        rows = pl.ds(pl.multiple_of(start, tq), tq)
        tile = lambda ref, p: ref[0, rows, _pair_lanes(p)]
        return _sb_sweep(q3, [(tile(kh_ref, p), tile(kl_ref, p)) for p in range(PAIRS)],
                         [(tile(vh_ref, p), tile(vl_ref, p)) for p in range(PAIRS)],
                         state, u2, visible)

    zero = (jnp.zeros((2 * tq, LANES), F32), jnp.zeros((tq, LANES), F32))
    state = sweep(qi * tq, (zero,) * PAIRS, _diag_visible(tq))
    state = _sweep_earlier_tiles(qi, lambda j, st: sweep(j * tq, st, None), state)
    for p in range(PAIRS):
        o_ref[0, :, _pair_lanes(p)] = state[p][1]


def _attn_prompt(q, k, v):
    batch, length, _ = q[0].shape
    tq = LANES
    qspec = pl.BlockSpec((1, tq, D_MODEL), lambda b, i: (b, i, 0))
    kspec = pl.BlockSpec((1, length, D_MODEL), lambda b, i: (b, 0, 0), pipeline_mode=pl.Buffered(1))
    return pl.pallas_call(
        _attn_prompt_kernel,
        out_shape=jax.ShapeDtypeStruct((batch, length, D_MODEL), F32),
        grid=(batch, length // tq),
        in_specs=[qspec, qspec, kspec, kspec, kspec, kspec],
        out_specs=qspec,
        compiler_params=_cparams("arbitrary", "arbitrary"),
        name="sb_attn_prompt",
    )(*q, *k, *v)


def _attn_sample_kernel(qh_ref, ql_ref, kh_ref, kl_ref, vh_ref, vl_ref, kc_ref, vc_ref, o_ref):
    tq = qh_ref.shape[1]
    past = kc_ref.shape[1]
    u2 = _suffix_matrix()
    pad = jnp.zeros((LANES - tq, LANES), BF16)
    padded = lambda ref, p: jnp.concatenate([ref[0, :, _pair_lanes(p)], pad], axis=0)
    q3 = [_query_pieces(qh_ref[0, :, _pair_lanes(p)], ql_ref[0, :, _pair_lanes(p)])
          for p in range(PAIRS)]
    zero = (jnp.zeros((2 * tq, LANES), F32), jnp.zeros((tq, LANES), F32))
    state = _sb_sweep(q3, [(padded(kh_ref, p), padded(kl_ref, p)) for p in range(PAIRS)],
                      [(padded(vh_ref, p), padded(vl_ref, p)) for p in range(PAIRS)],
                      (zero,) * PAIRS, u2, _diag_visible(tq))

    def sweep(j, st):
        rows = pl.ds(pl.multiple_of(j * LANES, LANES), LANES)
        return _sb_sweep(q3, [_split2(kc_ref[0, rows, _pair_lanes(p)]) for p in range(PAIRS)],
                         [_split2(vc_ref[0, rows, _pair_lanes(p)]) for p in range(PAIRS)],
                         st, u2, None)

    state = _sweep_earlier_tiles(past // LANES, sweep, state)
    for p in range(PAIRS):
        o_ref[0, :, _pair_lanes(p)] = state[p][1]


def _attn_sample(q, k, v, k_cache, v_cache):
    batch, length, _ = q[0].shape
    past = k_cache.shape[1]
    assert length <= LANES and past % LANES == 0
    qspec = pl.BlockSpec((1, length, D_MODEL), lambda b: (b, 0, 0))
    cspec = pl.BlockSpec((1, past, D_MODEL), lambda b: (b, 0, 0))
    return pl.pallas_call(
        _attn_sample_kernel,
        out_shape=jax.ShapeDtypeStruct((batch, length, D_MODEL), F32),
        grid=(batch,),
        in_specs=[qspec] * 6 + [cspec, cspec],
        out_specs=qspec,
        compiler_params=_cparams("arbitrary"),
        name="sb_attn_sample",
    )(*q, *k, *v, k_cache, v_cache)


def _row_max(rows):
    m = rows[0]
    for r in rows[1:]:
        m = jnp.maximum(m, r)
    return m


def _first_hit(rows, m):
    hits = []
    taken = None
    for r in rows:
        eq = r == m
        hit = eq if taken is None else jnp.logical_and(eq, jnp.logical_not(taken))
        taken = eq if taken is None else jnp.logical_or(taken, eq)
        hits.append(hit)
    return hits


def _route(s_rows, sel_rows):
    neg = jnp.float32(-jnp.inf)
    top1, top2, score = [], [], []
    for g in range(N_GROUPS):
        rows = sel_rows[g * EXPERTS_PER_GROUP:(g + 1) * EXPERTS_PER_GROUP]
        m1 = _row_max(rows)
        h1 = _first_hit(rows, m1)
        rest = [jnp.where(h, neg, r) for h, r in zip(h1, rows)]
        m2 = _row_max(rest)
        h2 = _first_hit(rest, m2)
        top1.append(h1)
        top2.append(h2)
        score.append(m1 + m2)
    best = _row_max(score)
    ghit = _first_hit(score, best)
    picked = []
    for g in range(N_GROUPS):
        for e in range(EXPERTS_PER_GROUP):
            picked.append(jnp.logical_and(ghit[g], jnp.logical_or(top1[g][e], top2[g][e])))
    total = None
    for p, s in zip(picked, s_rows):
        term = jnp.where(p, s, 0.0)
        total = term if total is None else total + term
    gates = [jnp.where(p, s / total, 0.0) for p, s in zip(picked, s_rows)]
    in_group = []
    for e in range(EXPERTS_PER_GROUP):
        row = gates[e]
        for g in range(1, N_GROUPS):
            row = row + gates[g * EXPERTS_PER_GROUP + e]
        in_group.append(row)
    group = jnp.zeros_like(best)
    for g in range(1, N_GROUPS):
        group = jnp.where(ghit[g], float(g), group)
    return in_group, group


def _oproj_kernel(m_ref, x_ref, gate_ref, wh_ref, wl_ref, g_ref, sc_ref, sh_ref, rw_ref, rb_ref,
                  x1_ref, h_ref, route_ref, *, precise):
    bs, tl, _ = x_ref.shape
    m = m_ref[...].reshape(bs * tl, D_MODEL)
    if precise:
        mix = _dot3(m, wh_ref[...], wl_ref[...])
    else:
        mix = _dot(m.astype(BF16), wh_ref[...])
    mix = mix.reshape(bs, tl, D_MODEL)
    x1 = x_ref[...] + gate_ref[...] * mix
    x1_ref[...] = x1
    h = _norm_mod(x1, g_ref[...], sc_ref[...], sh_ref[...])
    h_ref[...] = h
    logits = lax.dot_general(rw_ref[...], h.reshape(bs * tl, D_MODEL), _NT,
                             precision=lax.Precision.HIGHEST,
                             preferred_element_type=F32)
    s = _sigmoid(logits)
    sel = s + rb_ref[...]
    s_rows = [s[e:e + 1, :] for e in range(N_EXPERTS)]
    sel_rows = [sel[e:e + 1, :] for e in range(N_EXPERTS)]
    in_group, group = _route(s_rows, sel_rows)
    pad = [jnp.zeros_like(group)] * (SUBLANES - EXPERTS_PER_GROUP - 1)
    route_ref[0] = jnp.concatenate(in_group + [group] + pad, axis=0)


def _oproj_router(m, x, gate, w_pieces, g, sc, sh, router_wt, router_b, precise):
    seqs, length, _ = x.shape
    bs, tl = _seq_blocks(seqs, length)
    n_tiles = (seqs // bs) * (length // tl)
    per_seq = length // tl
    tok = pl.BlockSpec((bs, tl, D_MODEL), lambda s, i: (s, i, 0))
    mod = pl.BlockSpec((bs, 1, D_MODEL), lambda s, i: (s, 0, 0))
    one = pl.BlockSpec((1, 1, D_MODEL), lambda s, i: (0, 0, 0))
    wspec = pl.BlockSpec((D_MODEL, D_MODEL), lambda s, i: (0, 0))
    f32o = jax.ShapeDtypeStruct((seqs, length, D_MODEL), F32)
    x1, h, route = pl.pallas_call(
        functools.partial(_oproj_kernel, precise=precise),
        out_shape=(f32o, f32o, jax.ShapeDtypeStruct((n_tiles, SUBLANES, bs * tl), F32)),
        grid=(seqs // bs, per_seq),
        in_specs=[tok, tok, mod, wspec, wspec, one, mod, mod,
                  pl.BlockSpec((N_EXPERTS, D_MODEL), lambda s, i: (0, 0)),
                  pl.BlockSpec((N_EXPERTS, 1), lambda s, i: (0, 0))],
        out_specs=(tok, tok,
                   pl.BlockSpec((1, SUBLANES, bs * tl), lambda s, i: (s * per_seq + i, 0, 0))),
        compiler_params=_cparams("arbitrary", "arbitrary"),
        name="oproj_router",
    )(m, x, gate, *w_pieces, g.reshape(1, 1, D_MODEL), sc, sh, router_wt,
      router_b.reshape(N_EXPERTS, 1))
    route = jnp.swapaxes(route, 1, 2).reshape(seqs * length, SUBLANES)
    return x1, h, route[:, :EXPERTS_PER_GROUP], route[:, EXPERTS_PER_GROUP].astype(jnp.int32)


def _dispatch_plan(group, gates, tile):
    n = group.shape[0]
    n_slots = n + N_GROUPS * tile
    onehot = (group[:, None] == jnp.arange(N_GROUPS)[None, :]).astype(jnp.int32)
    counts = jnp.sum(onehot, axis=0)
    padded = (counts + tile - 1) // tile * tile
    ends = jnp.cumsum(padded)
    starts = ends - padded
    rank = jnp.take_along_axis(jnp.cumsum(onehot, axis=0) - onehot, group[:, None], axis=1)[:, 0]
    slot = starts[group] + rank
    src = jnp.zeros((n_slots,), jnp.int32).at[slot].set(jnp.arange(n, dtype=jnp.int32))
    slot_gates = jnp.zeros((n_slots, EXPERTS_PER_GROUP), F32).at[slot].set(gates)
    tile_start = jnp.arange(n_slots // tile, dtype=jnp.int32) * tile
    tile_group = jnp.minimum(jnp.sum(tile_start[:, None] >= ends[None, :], axis=1), N_GROUPS - 1)
    return src, slot.astype(jnp.int32), slot_gates, tile_group.astype(jnp.int32)


def _start_row_gather(idx_ref, base, src_hbm, dst_ref, sem):
    def body(r, _):
        row = idx_ref[base + r]
        pltpu.make_async_copy(src_hbm.at[pl.ds(row, 1), :], dst_ref.at[pl.ds(r, 1), :], sem).start()
        return 0

    lax.fori_loop(0, dst_ref.shape[0], body, 0, unroll=8)


def _wait_row_gather(src_hbm, dst_ref, sem):
    pltpu.make_async_copy(src_hbm.at[pl.ds(0, dst_ref.shape[0]), :], dst_ref, sem).wait()


def _moe_kernel(src_ref, tg_ref, h_hbm, gates_ref, wg_ref, wu_ref, wd_ref, o_ref, hbuf, sem):
    t = pl.program_id(0)
    tile = o_ref.shape[0]
    slot = t % 2

    @pl.when(t == 0)
    def _():
        _start_row_gather(src_ref, 0, h_hbm, hbuf.at[0], sem.at[0])

    @pl.when(t + 1 < pl.num_programs(0))
    def _():
        _start_row_gather(src_ref, (t + 1) * tile, h_hbm, hbuf.at[1 - slot], sem.at[1 - slot])

    _wait_row_gather(h_hbm, hbuf.at[slot], sem.at[slot])
    h = hbuf[slot].astype(BF16)
    gates = gates_ref[...]
    acc = jnp.zeros((tile, D_MODEL), F32)
    for e in range(EXPERTS_PER_GROUP):
        a = _dot(h, wg_ref[e])
        u = _dot(h, wu_ref[e])
        he = (a * _sigmoid(a) * u).astype(BF16)
        acc = acc + gates[:, e:e + 1] * _dot(he, wd_ref[e])
    o_ref[...] = acc


def _moe_experts(h, src, slot_gates, tile_group, wg_bf, wu_bf, wd_bf, tile):
    n_slots = src.shape[0]
    group_w = lambda shape: pl.BlockSpec((EXPERTS_PER_GROUP,) + shape, lambda t, src, tg: (tg[t], 0, 0))
    return pl.pallas_call(
        _moe_kernel,
        out_shape=jax.ShapeDtypeStruct((n_slots, D_MODEL), F32),
        grid_spec=pltpu.PrefetchScalarGridSpec(
            num_scalar_prefetch=2,
            grid=(n_slots // tile,),
            in_specs=[pl.BlockSpec(memory_space=pl.ANY),
                      pl.BlockSpec((tile, EXPERTS_PER_GROUP), lambda t, src, tg: (t, 0)),
                      group_w((D_MODEL, D_EXPERT)), group_w((D_MODEL, D_EXPERT)),
                      group_w((D_EXPERT, D_MODEL))],
            out_specs=pl.BlockSpec((tile, D_MODEL), lambda t, src, tg: (t, 0)),
            scratch_shapes=[pltpu.VMEM((2, tile, D_MODEL), F32), pltpu.SemaphoreType.DMA((2,))]),
        compiler_params=_cparams("arbitrary"),
        name="moe_experts",
    )(src, tile_group, h, slot_gates, wg_bf, wu_bf, wd_bf)


def _moe_combine_kernel(slot_ref, f_hbm, x_ref, gate_ref, fg_ref, o_ref, fbuf, sem, *, final_norm):
    bs, tl, _ = x_ref.shape
    tile = bs * tl
    t = pl.program_id(0) * pl.num_programs(1) + pl.program_id(1)
    n_tiles = pl.num_programs(0) * pl.num_programs(1)
    slot = t % 2

    @pl.when(t == 0)
    def _():
        _start_row_gather(slot_ref, 0, f_hbm, fbuf.at[0], sem.at[0])

    @pl.when(t + 1 < n_tiles)
    def _():
        _start_row_gather(slot_ref, (t + 1) * tile, f_hbm, fbuf.at[1 - slot], sem.at[1 - slot])

    _wait_row_gather(f_hbm, fbuf.at[slot], sem.at[slot])
    x2 = x_ref[...] + gate_ref[...] * fbuf[slot].reshape(bs, tl, D_MODEL)
    if final_norm:
        ms = jnp.mean(x2 * x2, axis=-1, keepdims=True)
        x2 = x2 * lax.rsqrt(ms + NORM_EPS) * fg_ref[...]
    o_ref[...] = x2


def _moe_combine(ffn_sorted, slot, x, gate, final_g, final_norm):
    seqs, length, _ = x.shape
    bs, tl = _seq_blocks(seqs, length)
    tok = pl.BlockSpec((bs, tl, D_MODEL), lambda s, i, slot: (s, i, 0))
    return pl.pallas_call(
        functools.partial(_moe_combine_kernel, final_norm=final_norm),
        out_shape=jax.ShapeDtypeStruct((seqs, length, D_MODEL), F32),
        grid_spec=pltpu.PrefetchScalarGridSpec(
            num_scalar_prefetch=1,
            grid=(seqs // bs, length // tl),
            in_specs=[pl.BlockSpec(memory_space=pl.ANY), tok,
                      pl.BlockSpec((bs, 1, D_MODEL), lambda s, i, slot: (s, 0, 0)),
                      pl.BlockSpec((1, 1, D_MODEL), lambda s, i, slot: (0, 0, 0))],
            out_specs=tok,
            scratch_shapes=[pltpu.VMEM((2, bs * tl, D_MODEL), F32), pltpu.SemaphoreType.DMA((2,))]),
        compiler_params=_cparams("arbitrary", "arbitrary"),
        name="moe_combine",
    )(slot, ffn_sorted, x, gate, final_g.reshape(1, 1, D_MODEL))


def _moe(h, group, gates, x, gate, final_g, wg_bf, wu_bf, wd_bf, final_norm):
    seqs, length, _ = x.shape
    tile = min(MOE_TILE, max(LANES, seqs * length // (2 * N_GROUPS)))
    src, slot, slot_gates, tile_group = _dispatch_plan(group, gates, tile)
    ffn_sorted = _moe_experts(h.reshape(seqs * length, D_MODEL), src, slot_gates, tile_group,
                              wg_bf, wu_bf, wd_bf, tile)
    return _moe_combine(ffn_sorted, slot, x, gate, final_g, final_norm)


def _head_selectors():
    r = lax.broadcasted_iota(jnp.int32, (D_MODEL, LANES), 0)
    c = lax.broadcasted_iota(jnp.int32, (D_MODEL, LANES), 1)
    sel = jnp.where(r // HEAD_DIM == c, 1.0, 0.0).astype(BF16)
    rt = lax.broadcasted_iota(jnp.int32, (LANES, D_MODEL), 0)
    ct = lax.broadcasted_iota(jnp.int32, (LANES, D_MODEL), 1)
    selt = jnp.where(ct // HEAD_DIM == rt, 1.0, 0.0).astype(BF16)
    return jnp.concatenate([sel, sel], axis=0), jnp.concatenate([selt, selt], axis=0)


def _head_sum(x, sel2, selt2):
    return _dot2(_dot2(x, sel2), selt2)


def _rwkv_pre_kernel(x_ref, prev_ref, shift_ref, g_ref, sc_ref, sh_ref, mu_ref,
                     wr_ref, wk_ref, wv_ref, w0_ref, w1_ref, w2_ref, a0_ref, a1_ref, a2_ref,
                     g1_ref, g2_ref, kk_ref, ka_ref,
                     r_ref, w_ref, k_ref, v_ref, na_ref, kb_ref, go_ref, so_ref):
    bs, tl, _ = x_ref.shape
    t = bs * tl
    i = pl.program_id(1)
    g, sc, sh = g_ref[...], sc_ref[...], sh_ref[...]
    h = _norm_mod(x_ref[...], g, sc, sh)
    so_ref[...] = h[:, tl - 1:tl, :]
    h_before = _norm_mod(prev_ref[:, 7:8, :], g, sc, sh)
    first = jnp.where(i == 0, shift_ref[...], h_before)
    row = lax.broadcasted_iota(jnp.int32, (bs, tl, D_MODEL), 1)
    prev = jnp.where(row == 0, first, pltpu.roll(h, 1, 1))
    h = h.reshape(t, D_MODEL)
    xx = prev.reshape(t, D_MODEL) - h
    mixed = [(h + xx * mu_ref[j:j + 1, :]).astype(BF16) for j in range(6)]
    xr, xw, xk, xv, xa, xg = mixed
    r = _dot(xr, wr_ref[...])
    k = _dot(xk, wk_ref[...])
    v = _dot(xv, wv_ref[...])
    wl = w0_ref[...] + _dot(jnp.tanh(_dot(xw, w1_ref[...])).astype(BF16), w2_ref[...])
    neg = -wl
    softplus = jnp.maximum(neg, 0.0) + jnp.log(1.0 + jnp.exp(-jnp.abs(neg)))
    w = -softplus - 0.5
    log_decay = -jnp.exp(w)
    a = _sigmoid(a0_ref[...] + _dot(_dot(xa, a1_ref[...]).astype(BF16), a2_ref[...]))
    gate = _dot(_sigmoid(_dot(xg, g1_ref[...])).astype(BF16), g2_ref[...])
    sel2, selt2 = _head_selectors()
    kk = k * kk_ref[...]
    norm = jnp.maximum(jnp.sqrt(_head_sum(kk * kk, sel2, selt2)), 1e-12)
    kk = kk / norm
    k = k * (1.0 + (a - 1.0) * ka_ref[...])
    shape = (bs, tl, D_MODEL)
    r_ref[...] = r.reshape(shape)
    w_ref[...] = log_decay.reshape(shape)
    k_ref[...] = k.reshape(shape)
    v_ref[...] = v.reshape(shape)
    na_ref[...] = (-kk).reshape(shape)
    kb_ref[...] = (kk * a).reshape(shape)
    go_ref[...] = gate.reshape(shape)


def _rwkv_pre(x, shift0, g, sc, sh, p):
    seqs, length, _ = x.shape
    bs, tl = _seq_blocks(seqs, length, RWKV_PRE_TILE)
    tok = pl.BlockSpec((bs, tl, D_MODEL), lambda s, i: (s, i, 0))
    prev = pl.BlockSpec((bs, 8, D_MODEL), lambda s, i: (s, jnp.maximum(i * (tl // 8) - 1, 0), 0))
    mod = pl.BlockSpec((bs, 1, D_MODEL), lambda s, i: (s, 0, 0))
    one = pl.BlockSpec((1, 1, D_MODEL), lambda s, i: (0, 0, 0))

    def full(a):
        return pl.BlockSpec(a.shape, lambda s, i: (0,) * a.ndim)

    vec = lambda a: a.reshape(1, D_MODEL)
    weights = [p['rw_mu'], p['rw_w_r'], p['rw_w_k'], p['rw_w_v'], vec(p['rw_w0']), p['rw_w1'],
               p['rw_w2'], vec(p['rw_a0']), p['rw_a1'], p['rw_a2'], p['rw_g1'], p['rw_g2'],
               vec(p['rw_k_k']), vec(p['rw_k_a'])]
    f32o = jax.ShapeDtypeStruct((seqs, length, D_MODEL), F32)
    outs = pl.pallas_call(
        _rwkv_pre_kernel,
        out_shape=(f32o,) * 7 + (jax.ShapeDtypeStruct((seqs, 1, D_MODEL), F32),),
        grid=(seqs // bs, length // tl),
        in_specs=[tok, prev, mod, one, mod, mod] + [full(a) for a in weights],
        out_specs=(tok,) * 7 + (mod,),
        compiler_params=_cparams("arbitrary", "arbitrary"),
        name="rwkv_pre",
    )(x, x, shift0, g.reshape(1, 1, D_MODEL), sc, sh, *weights)
    return outs


def _bdot(a, b, dims=None):
    a, b = a.astype(BF16), b.astype(BF16)
    if dims is None:
        return jnp.dot(a, b, preferred_element_type=F32)
    return lax.dot_general(a, b, dims, preferred_element_type=F32)


def _scan_kernel(r_ref, lw_ref, k_ref, v_ref, a_ref, b_ref, s0_ref, y_ref, st_ref, z_ref, *, chunk):
    tc = r_ref.shape[1]
    t2 = 2 * chunk
    c = pl.program_id(1)
    lane_a = _pair_masks(chunk)
    fold = lambda x: jnp.where(lane_a, x[:chunk], x[chunk:])
    stack = lambda x: jnp.concatenate([x, x], axis=0)
    split_heads = lambda x: _stack_heads(x, axis=0)

    def block_diag(mats):
        n = mats[0].shape[0]
        zero = jnp.zeros((n, n), F32)
        return jnp.concatenate(
            [jnp.concatenate([m if j == i else zero for j in range(len(mats))], axis=1)
             for i, m in enumerate(mats)], axis=0)

    @pl.when(c == 0)
    def _():
        for q in range(PAIRS // SCAN_GROUP):
            z_ref[q] = block_diag([s0_ref[0, 2 * SCAN_GROUP * q + h] for h in range(2 * SCAN_GROUP)])

    row = lax.broadcasted_iota(jnp.int32, (t2, t2), 0)
    col = lax.broadcasted_iota(jnp.int32, (t2, t2), 1)
    same_head = (row // chunk) == (col // chunk)
    strict = jnp.logical_and(same_head, (row % chunk) > (col % chunk))
    incl = jnp.logical_and(same_head, (row % chunk) >= (col % chunk))
    tr = lax.broadcasted_iota(jnp.int32, (chunk, chunk), 0)
    tcol = lax.broadcasted_iota(jnp.int32, (chunk, chunk), 1)
    tri = jnp.where(tr >= tcol, 1.0, 0.0).astype(BF16)
    tri2 = jnp.concatenate([tri, tri], axis=1)
    gw = SCAN_GROUP * LANES
    zr = lax.broadcasted_iota(jnp.int32, (gw, gw), 0)
    zc = lax.broadcasted_iota(jnp.int32, (gw, gw), 1)
    z_diag = (zr // HEAD_DIM) == (zc // HEAD_DIM)
    n_doublings = chunk.bit_length() - 1

    def one_chunk(ci, _):
        rows = pl.ds(pl.multiple_of(ci * chunk, chunk), chunk)
        lw, r, k, v, a, b = [ref[0, rows, :] for ref in (lw_ref, r_ref, k_ref, v_ref, a_ref, b_ref)]
        hi, lo = _split2(lw)
        cum = _dot(tri2, jnp.concatenate([hi, lo], axis=0))
        grow = jnp.exp(-cum)
        decay = jnp.exp(cum)
        total = decay[chunk - 1:chunk, :]
        a_t = a * jnp.exp(cum - lw)
        r_t = r * decay
        b_t = b * grow
        k_t = k * grow
        b_end = b_t * total
        k_end = k_t * total
        n_groups = PAIRS // SCAN_GROUP
        pair = lambda x, p: x[:, _pair_lanes(p)]
        grp = lambda x, q: x[:, q * gw:(q + 1) * gw]
        members = lambda q: range(SCAN_GROUP * q, SCAN_GROUP * (q + 1))
        per_group = lambda f: jnp.stack([f(q) for q in range(n_groups)])
        g = _bdot(jnp.stack([jnp.concatenate([split_heads(pair(a_t, p)), split_heads(pair(r_t, p))], axis=0)
                             for p in range(PAIRS)]),
                  jnp.stack([jnp.concatenate([stack(pair(b_t, p)), stack(pair(k_t, p))], axis=0)
                             for p in range(PAIRS)]), _BATCHED_NT)
        l_ab = jnp.where(strict, g[:, :t2, :t2], 0.0)
        l_ak = jnp.where(strict, g[:, :t2, t2:], 0.0)
        l_rb = jnp.where(incl, g[:, t2:, :t2], 0.0)
        l_rk = jnp.where(incl, g[:, t2:, t2:], 0.0)
        diag = lambda m: per_group(lambda q: block_diag([m[p] for p in members(q)]))
        z = z_ref[...]
        applied = _bdot(per_group(lambda q: jnp.concatenate([grp(a_t, q), grp(r_t, q)], axis=0)),
                        z, _BATCHED_NT)
        vv = per_group(lambda q: jnp.concatenate([stack(pair(v, p)) for p in members(q)], axis=0))
        x = per_group(lambda q: jnp.concatenate(
            [stack(applied[q, :chunk, j * LANES:(j + 1) * LANES]) for j in range(SCAN_GROUP)],
            axis=0)) + _bdot(diag(l_ak), vv, _BATCHED)
        power = diag(l_ab)
        for i in range(n_doublings):
            x = x + _bdot(power, x, _BATCHED)
            if i + 1 < n_doublings:
                power = _bdot(power, power, _BATCHED)
        unstack = lambda m, q: jnp.concatenate(
            [fold(m[q, j * t2:(j + 1) * t2]) for j in range(SCAN_GROUP)], axis=1)
        u = [unstack(x, q) for q in range(n_groups)]
        uu = per_group(lambda q: jnp.concatenate(
            [stack(u[q][:, j * LANES:(j + 1) * LANES]) for j in range(SCAN_GROUP)], axis=0))
        ys = _bdot(jnp.concatenate([diag(l_rb), diag(l_rk)], axis=2),
                   jnp.concatenate([uu, vv], axis=1), _BATCHED)
        y_ref[0, rows, :] = jnp.concatenate(
            [applied[q, chunk:] + unstack(ys, q) for q in range(n_groups)], axis=1)
        grown = _bdot(per_group(lambda q: jnp.concatenate([u[q], grp(v, q)], axis=0)),
                      per_group(lambda q: jnp.concatenate([grp(b_end, q), grp(k_end, q)], axis=0)),
                      _BATCHED_TN)
        z_ref[...] = (z * per_group(lambda q: grp(total, q)) + jnp.where(z_diag, grown, 0.0))
        return 0

    lax.fori_loop(0, tc // chunk, one_chunk, 0)

    @pl.when(c == pl.num_programs(1) - 1)
    def _():
        for q in range(PAIRS // SCAN_GROUP):
            z = z_ref[q]
            for h in range(2 * SCAN_GROUP):
                block = slice(h * HEAD_DIM, (h + 1) * HEAD_DIM)
                st_ref[0, 2 * SCAN_GROUP * q + h] = z[block, block]


def _scan(r, lw, k, v, na, kb, s0):
    seqs, length, _ = r.shape
    chunk = min(length, SCAN_CHUNK)
    tc = min(length, SCAN_BLOCK)
    assert length % tc == 0 and tc % chunk == 0 and chunk & (chunk - 1) == 0
    tok = pl.BlockSpec((1, tc, D_MODEL), lambda s, c: (s, c, 0))
    st = pl.BlockSpec((1, HEADS, HEAD_DIM, HEAD_DIM), lambda s, c: (s, 0, 0, 0))
    return pl.pallas_call(
        functools.partial(_scan_kernel, chunk=chunk),
        out_shape=(jax.ShapeDtypeStruct((seqs, length, D_MODEL), F32),
                   jax.ShapeDtypeStruct((seqs, HEADS, HEAD_DIM, HEAD_DIM), F32)),
        grid=(seqs, length // tc),
        in_specs=[tok] * 6 + [st],
        out_specs=(tok, st),
        scratch_shapes=[pltpu.VMEM((PAIRS // SCAN_GROUP, SCAN_GROUP * LANES, SCAN_GROUP * LANES), F32)],
        compiler_params=_cparams("arbitrary", "arbitrary"),
        name="rwkv_scan",
    )(r, lw, k, v, na, kb, s0)


def _rwkv_post_kernel(y_ref, r_ref, k_ref, v_ref, g_ref, lg_ref, lb_ref, rk_ref, o_ref):
    bs, tl, _ = y_ref.shape
    t = bs * tl
    sel2, selt2 = _head_selectors()
    y = y_ref[...].reshape(t, D_MODEL)
    mu = _head_sum(y, sel2, selt2) * (1.0 / HEAD_DIM)
    d = y - mu
    var = _head_sum(d * d, sel2, selt2) * (1.0 / HEAD_DIM)
    yn = d * lax.rsqrt(var + LNX_EPS) * lg_ref[...] + lb_ref[...]
    rk = r_ref[...].reshape(t, D_MODEL) * k_ref[...].reshape(t, D_MODEL) * rk_ref[...]
    bonus = _head_sum(rk, sel2, selt2) * v_ref[...].reshape(t, D_MODEL)
    out = (yn + bonus) * g_ref[...].reshape(t, D_MODEL)
    o_ref[...] = out.astype(BF16).reshape(bs, tl, D_MODEL)


def _rwkv_post(y, r, k, v, gate, lnx_g, lnx_b, r_k):
    seqs, length, _ = y.shape
    bs, tl = _seq_blocks(seqs, length)
    tok = pl.BlockSpec((bs, tl, D_MODEL), lambda s, i: (s, i, 0))
    vec = pl.BlockSpec((1, D_MODEL), lambda s, i: (0, 0))
    return pl.pallas_call(
        _rwkv_post_kernel,
        out_shape=jax.ShapeDtypeStruct((seqs, length, D_MODEL), BF16),
        grid=(seqs // bs, length // tl),
        in_specs=[tok] * 5 + [vec] * 3,
        out_specs=tok,
        compiler_params=_cparams("arbitrary", "arbitrary"),
        name="rwkv_post",
    )(y, r, k, v, gate, lnx_g.reshape(1, D_MODEL), lnx_b.reshape(1, D_MODEL),
      r_k.reshape(1, D_MODEL))


def _trunk(x, mod, p, wb, k_cache, v_cache, shift0, s0):
    seqs, length, _ = x.shape
    part = lambda layer, j: mod[layer, :, j * D_MODEL:(j + 1) * D_MODEL].reshape(seqs, 1, D_MODEL)
    router_wt = p['router_w'].T
    sh1, sc1, g1, sh2, sc2, g2 = [part(0, j) for j in range(6)]
    k, v, qh, ql, kh, kl, vh, vl = _qkv(x, p['norm_mix_g'][0], sc1, sh1, *wb['sb_w_qkv'])
    if k_cache is None:
        attn = _attn_prompt((qh, ql), (kh, kl), (vh, vl))
    else:
        past = k_cache.shape[1]
        attn = _attn_sample((qh, ql), (kh, kl), (vh, vl), k_cache.reshape(seqs, past, D_MODEL),
                            v_cache.reshape(seqs, past, D_MODEL))
    x, h, gates, group = _oproj_router(attn, x, g1, wb['sb_w_o'], p['norm_ffn_g'][0], sc2, sh2,
                                       router_wt, p['router_bias'], precise=True)
    x = _moe(h, group, gates, x, g2, p['final_norm_g'], wb['exp_w_gate'][0], wb['exp_w_up'][0],
             wb['exp_w_down'][0], final_norm=False)
    sb_k = k.reshape(seqs, length, HEADS, HEAD_DIM)
    sb_v = v.reshape(seqs, length, HEADS, HEAD_DIM)
    sh1, sc1, g1, sh2, sc2, g2 = [part(1, j) for j in range(6)]
    r, w, kq, vv, na, kb, gate, shift = _rwkv_pre(x, shift0, p['norm_mix_g'][1], sc1, sh1, wb)
    y, state = _scan(r, w, kq, vv, na, kb, s0)
    m = _rwkv_post(y, r, kq, vv, gate, p['rw_lnx_g'], p['rw_lnx_b'], p['rw_r_k'])
    x, h, gates, group = _oproj_router(m, x, g1, wb['rw_w_o'], p['norm_ffn_g'][1], sc2, sh2,
                                       router_wt, p['router_bias'], precise=False)
    y_out = _moe(h, group, gates, x, g2, p['final_norm_g'], wb['exp_w_gate'][1], wb['exp_w_up'][1],
                 wb['exp_w_down'][1], final_norm=True)
    return y_out, sb_k, sb_v, state, shift


def kernel(x_prompt, x_sample, cache_sb_k, cache_sb_v, state_rw_wkv, state_rw_shift,
           c_prompt, c_sample, ada_w, ada_b, norm_mix_g, norm_ffn_g, final_norm_g,
           sb_w_qkv, sb_w_o, rw_mu, rw_w_r, rw_w_k, rw_w_v, rw_w0, rw_w1, rw_w2,
           rw_a0, rw_a1, rw_a2, rw_g1, rw_g2, rw_k_k, rw_k_a, rw_r_k, rw_lnx_g, rw_lnx_b,
           rw_w_o, router_w, router_bias, exp_w_gate, exp_w_up, exp_w_down):
    p = {
        'norm_mix_g': norm_mix_g, 'norm_ffn_g': norm_ffn_g, 'final_norm_g': final_norm_g,
        'rw_r_k': rw_r_k, 'rw_lnx_g': rw_lnx_g, 'rw_lnx_b': rw_lnx_b,
        'router_w': router_w, 'router_bias': router_bias,
    }
    bf = lambda a: a.astype(BF16)
    wb = {
        'sb_w_qkv': _split_weight(sb_w_qkv), 'sb_w_o': _split_weight(sb_w_o),
        'rw_w_o': _split_weight(rw_w_o),
        'rw_mu': rw_mu, 'rw_w_r': bf(rw_w_r), 'rw_w_k': bf(rw_w_k), 'rw_w_v': bf(rw_w_v),
        'rw_w0': rw_w0, 'rw_w1': bf(rw_w1), 'rw_w2': bf(rw_w2), 'rw_a0': rw_a0,
        'rw_a1': bf(rw_a1), 'rw_a2': bf(rw_a2), 'rw_g1': bf(rw_g1), 'rw_g2': bf(rw_g2),
        'rw_k_k': rw_k_k, 'rw_k_a': rw_k_a,
        'exp_w_gate': bf(exp_w_gate), 'exp_w_up': bf(exp_w_up), 'exp_w_down': bf(exp_w_down),
    }
    n_prompt = x_prompt.shape[0]
    mod = _ada(jnp.concatenate([c_prompt, c_sample], axis=0), ada_w, ada_b)
    shift0 = jnp.zeros((n_prompt, 1, D_MODEL), x_prompt.dtype)
    s0 = jnp.zeros((n_prompt, HEADS, HEAD_DIM, HEAD_DIM), F32)
    y_p, k_p, v_p, st_p, sh_p = _trunk(x_prompt, mod[:, :n_prompt], p, wb, None, None, shift0, s0)
    y_s, k_s, v_s, st_s, sh_s = _trunk(x_sample, mod[:, n_prompt:], p, wb, cache_sb_k, cache_sb_v,
                                       state_rw_shift, state_rw_wkv)
    return (y_p, y_s, k_p, v_p, k_s, v_s, st_p, sh_p, st_s, sh_s)
```

```python
import functools

import jax
import jax.numpy as jnp
from jax import lax
from jax.experimental import pallas as pl
from jax.experimental.pallas import tpu as pltpu

F32 = jnp.float32
BF16 = jnp.bfloat16

D_MODEL = 1024
HEADS = 16
HEAD_DIM = 64
LANES = 128
SUBLANES = 8
PAIRS = D_MODEL // LANES
N_EXPERTS = 16
N_GROUPS = 4
EXPERTS_PER_GROUP = 4
D_EXPERT = 512
NORM_EPS = 1e-6
LNX_EPS = 64e-5
ZERO_WEIGHT_CARRY = 104.0
TOKEN_TILE = 512
RWKV_PRE_TILE = 256
QKV_TILE = 256
SCAN_CHUNK = 64
SCAN_BLOCK = 256
SCAN_GROUP = 2
MOE_TILE = 512
VMEM_LIMIT = 56 * 1024 * 1024

_NT = (((1,), (1,)), ((), ()))
_BATCHED = (((2,), (1,)), ((0,), (0,)))
_BATCHED_NT = (((2,), (2,)), ((0,), (0,)))
_BATCHED_TN = (((1,), (1,)), ((0,), (0,)))


def _cparams(*sem):
    return pltpu.CompilerParams(dimension_semantics=sem, vmem_limit_bytes=VMEM_LIMIT)


def _sigmoid(x):
    return 1.0 / (1.0 + jnp.exp(-x))


def _split2(x):
    hi = x.astype(BF16)
    lo = (x - hi.astype(F32)).astype(BF16)
    return hi, lo


def _dot(a, b):
    return jnp.dot(a, b, preferred_element_type=F32)


def _dot2(x, w2):
    hi, lo = _split2(x)
    return _dot(jnp.concatenate([hi, lo], axis=1), w2)


def _norm_mod(x, g, sc, sh):
    ms = jnp.mean(x * x, axis=-1, keepdims=True)
    y = x * lax.rsqrt(ms + NORM_EPS) * g
    return y * (1.0 + sc) + sh


def _seq_blocks(seqs, length, tile=TOKEN_TILE):
    if length >= tile:
        assert length % tile == 0
        return 1, tile
    assert tile % length == 0 and seqs % (tile // length) == 0 and length % 16 == 0
    return tile // length, length


def _ada_kernel(c_ref, w_ref, b_ref, o_ref):
    c = c_ref[...]
    s = c * _sigmoid(c)
    o_ref[0] = jnp.dot(s, w_ref[0], precision=lax.Precision.HIGHEST,
                       preferred_element_type=F32) + b_ref[0]


def _ada(c, ada_w, ada_b):
    rows = c.shape[0]
    depth = ada_w.shape[0]
    return pl.pallas_call(
        _ada_kernel,
        out_shape=jax.ShapeDtypeStruct((depth, rows, 6 * D_MODEL), F32),
        grid=(depth, 6),
        in_specs=[pl.BlockSpec((rows, D_MODEL), lambda l, j: (0, 0)),
                  pl.BlockSpec((1, D_MODEL, D_MODEL), lambda l, j: (l, 0, j)),
                  pl.BlockSpec((1, 1, D_MODEL), lambda l, j: (l, 0, j))],
        out_specs=pl.BlockSpec((1, rows, D_MODEL), lambda l, j: (l, 0, j)),
        compiler_params=_cparams("arbitrary", "arbitrary"),
        name="ada_mod",
    )(c, ada_w, ada_b.reshape(depth, 1, 6 * D_MODEL))


def _dot3(x, w_hi, w_lo):
    hi, lo = _split2(x)
    rows = x.shape[0]
    both = _dot(jnp.concatenate([hi, lo], axis=0), w_hi)
    return both[:rows] + both[rows:] + _dot(hi, w_lo)


def _split_weight(w):
    hi = w.astype(BF16)
    return hi, (w - hi.astype(F32)).astype(BF16)


def _qkv_kernel(x_ref, g_ref, sc_ref, sh_ref, wh_ref, wl_ref, k_ref, v_ref, *piece_refs):
    bs, tl, _ = x_ref.shape
    h = _norm_mod(x_ref[...], g_ref[...], sc_ref[...], sh_ref[...]).reshape(bs * tl, D_MODEL)
    qkv = _dot3(h, wh_ref[...], wl_ref[...])
    q = qkv[:, :D_MODEL] * (HEAD_DIM ** -0.5)
    k = qkv[:, D_MODEL:2 * D_MODEL]
    v = qkv[:, 2 * D_MODEL:]
    k_ref[...] = k.reshape(bs, tl, D_MODEL)
    v_ref[...] = v.reshape(bs, tl, D_MODEL)
    pieces = _split2(q) + _split2(k) + _split2(v)
    for ref, piece in zip(piece_refs, pieces):
        ref[...] = piece.reshape(bs, tl, D_MODEL)


def _qkv(x, g, sc, sh, w_hi, w_lo):
    seqs, length, _ = x.shape
    bs, tl = _seq_blocks(seqs, length, QKV_TILE)
    tok = pl.BlockSpec((bs, tl, D_MODEL), lambda s, i: (s, i, 0))
    mod = pl.BlockSpec((bs, 1, D_MODEL), lambda s, i: (s, 0, 0))
    wspec = pl.BlockSpec((D_MODEL, 3 * D_MODEL), lambda s, i: (0, 0))
    f32o = jax.ShapeDtypeStruct((seqs, length, D_MODEL), F32)
    bfo = jax.ShapeDtypeStruct((seqs, length, D_MODEL), BF16)
    return pl.pallas_call(
        _qkv_kernel,
        out_shape=(f32o, f32o) + (bfo,) * 6,
        grid=(seqs // bs, length // tl),
        in_specs=[tok, pl.BlockSpec((1, 1, D_MODEL), lambda s, i: (0, 0, 0)), mod, mod, wspec, wspec],
        out_specs=(tok,) * 8,
        compiler_params=_cparams("arbitrary", "arbitrary"),
        name="sb_qkv",
    )(x, g.reshape(1, 1, D_MODEL), sc, sh, w_hi, w_lo)


def _pair_masks(rows):
    lane = lax.broadcasted_iota(jnp.int32, (rows, LANES), 1)
    return lane < HEAD_DIM


def _suffix_matrix():
    r = lax.broadcasted_iota(jnp.int32, (LANES, LANES), 0)
    c = lax.broadcasted_iota(jnp.int32, (LANES, LANES), 1)
    u = jnp.where(r > c, 1.0, 0.0).astype(BF16)
    return jnp.concatenate([u, u], axis=0)


def _query_pieces(q_hi, q_lo):
    hi, lo = _stack_heads(q_hi), _stack_heads(q_lo)
    return jnp.concatenate([hi, lo, hi], axis=1)


def _key_pieces(k_hi, k_lo):
    return jnp.concatenate([k_hi, k_hi, k_lo], axis=1)


def _sb_sweep(q3, keys, values, state, u2, visible):
    n = len(q3)
    rows = q3[0].shape[0]
    tq = rows // 2
    wide = lambda x: jnp.concatenate([x[:tq], x[tq:]], axis=1)
    z_all = lax.dot_general(jnp.stack(q3), jnp.stack([_key_pieces(*keys[p]) for p in range(n)]),
                            _BATCHED_NT, preferred_element_type=F32)
    zs, sps = [], []
    for p in range(n):
        z = z_all[p]
        sp = jnp.maximum(z, 0.0) + jnp.log(1.0 + jnp.exp(-jnp.abs(z)))
        if visible is not None:
            sp = jnp.where(visible, sp, 0.0)
        zs.append(z)
        sps.append(sp)
    later_all = _dot2(jnp.concatenate(sps, axis=0), u2)
    weights, vals = [], []
    for p in range(n):
        later = later_all[p * rows:(p + 1) * rows]
        w = jnp.exp(zs[p] - sps[p] - later - state[p][0])
        if visible is not None:
            w = jnp.where(visible, w, 0.0)
        w_hi, w_lo = _split2(w)
        v_hi, v_lo = _stack_heads(values[p][0]), _stack_heads(values[p][1])
        weights.append(jnp.concatenate([wide(w_hi), wide(w_lo), wide(w_hi)], axis=1))
        vals.append(jnp.concatenate([v_hi, v_hi, v_lo], axis=0))
    gained = lax.dot_general(jnp.stack(weights), jnp.stack(vals), _BATCHED,
                             preferred_element_type=F32)
    return tuple((state[p][0] + jnp.sum(sps[p], axis=-1, keepdims=True), state[p][1] + gained[p])
                 for p in range(n))


def _stack_heads(x, axis=0):
    mask_a = _pair_masks(x.shape[0])
    return jnp.concatenate([jnp.where(mask_a, x, 0), jnp.where(mask_a, 0, x)], axis=axis)


def _diag_visible(tq):
    row = lax.broadcasted_iota(jnp.int32, (2 * tq, LANES), 0)
    col = lax.broadcasted_iota(jnp.int32, (2 * tq, LANES), 1)
    qpos = jnp.where(row >= tq, row - tq, row)
    return col < qpos


def _pair_lanes(p):
    return slice(p * LANES, (p + 1) * LANES)


def _sweep_earlier_tiles(n_tiles, sweep, state):
    def cond(c):
        return jnp.logical_and(c[0] < n_tiles, jnp.logical_not(c[1]))

    def body(c):
        st = sweep(n_tiles - 1 - c[0], c[2])
        low = st[0][0]
        for p in range(1, len(st)):
            low = jnp.minimum(low, st[p][0])
        return c[0] + 1, jnp.min(low) > ZERO_WEIGHT_CARRY, st

    return lax.while_loop(cond, body, (jnp.int32(0), False, state))[2]


def _attn_prompt_kernel(qh_ref, ql_ref, kh_ref, kl_ref, vh_ref, vl_ref, o_ref):
    tq = qh_ref.shape[1]
    qi = pl.program_id(1)
    u2 = _suffix_matrix()
    q3 = [_query_pieces(qh_ref[0, :, _pair_lanes(p)], ql_ref[0, :, _pair_lanes(p)])
          for p in range(PAIRS)]

    def sweep(start, state, visible):
        rows = pl.ds(pl.multiple_of(start, tq), tq)
        tile = lambda ref, p: ref[0, rows, _pair_lanes(p)]
        return _sb_sweep(q3, [(tile(kh_ref, p), tile(kl_ref, p)) for p in range(PAIRS)],
                         [(tile(vh_ref, p), tile(vl_ref, p)) for p in range(PAIRS)],
                         state, u2, visible)

    zero = (jnp.zeros((2 * tq, LANES), F32), jnp.zeros((tq, LANES), F32))
    state = sweep(qi * tq, (zero,) * PAIRS, _diag_visible(tq))
    state = _sweep_earlier_tiles(qi, lambda j, st: sweep(j * tq, st, None), state)
    for p in range(PAIRS):
        o_ref[0, :, _pair_lanes(p)] = state[p][1]


def _attn_prompt(q, k, v):
    batch, length, _ = q[0].shape
    tq = LANES
    qspec = pl.BlockSpec((1, tq, D_MODEL), lambda b, i: (b, i, 0))
    kspec = pl.BlockSpec((1, length, D_MODEL), lambda b, i: (b, 0, 0), pipeline_mode=pl.Buffered(1))
    return pl.pallas_call(
        _attn_prompt_kernel,
        out_shape=jax.ShapeDtypeStruct((batch, length, D_MODEL), F32),
        grid=(batch, length // tq),
        in_specs=[qspec, qspec, kspec, kspec, kspec, kspec],
        out_specs=qspec,
        compiler_params=_cparams("arbitrary", "arbitrary"),
        name="sb_attn_prompt",
    )(*q, *k, *v)


def _attn_sample_kernel(qh_ref, ql_ref, kh_ref, kl_ref, vh_ref, vl_ref, kc_ref, vc_ref, o_ref):
    tq = qh_ref.shape[1]
    past = kc_ref.shape[1]
    u2 = _suffix_matrix()
    pad = jnp.zeros((LANES - tq, LANES), BF16)
    padded = lambda ref, p: jnp.concatenate([ref[0, :, _pair_lanes(p)], pad], axis=0)
    q3 = [_query_pieces(qh_ref[0, :, _pair_lanes(p)], ql_ref[0, :, _pair_lanes(p)])
          for p in range(PAIRS)]
    zero = (jnp.zeros((2 * tq, LANES), F32), jnp.zeros((tq, LANES), F32))
    state = _sb_sweep(q3, [(padded(kh_ref, p), padded(kl_ref, p)) for p in range(PAIRS)],
                      [(padded(vh_ref, p), padded(vl_ref, p)) for p in range(PAIRS)],
                      (zero,) * PAIRS, u2, _diag_visible(tq))

    def sweep(j, st):
        rows = pl.ds(pl.multiple_of(j * LANES, LANES), LANES)
        return _sb_sweep(q3, [_split2(kc_ref[0, rows, _pair_lanes(p)]) for p in range(PAIRS)],
                         [_split2(vc_ref[0, rows, _pair_lanes(p)]) for p in range(PAIRS)],
                         st, u2, None)

    state = _sweep_earlier_tiles(past // LANES, sweep, state)
    for p in range(PAIRS):
        o_ref[0, :, _pair_lanes(p)] = state[p][1]


def _attn_sample(q, k, v, k_cache, v_cache):
    batch, length, _ = q[0].shape
    past = k_cache.shape[1]
    assert length <= LANES and past % LANES == 0
    qspec = pl.BlockSpec((1, length, D_MODEL), lambda b: (b, 0, 0))
    cspec = pl.BlockSpec((1, past, D_MODEL), lambda b: (b, 0, 0))
    return pl.pallas_call(
        _attn_sample_kernel,
        out_shape=jax.ShapeDtypeStruct((batch, length, D_MODEL), F32),
        grid=(batch,),
        in_specs=[qspec] * 6 + [cspec, cspec],
        out_specs=qspec,
        compiler_params=_cparams("arbitrary"),
        name="sb_attn_sample",
    )(*q, *k, *v, k_cache, v_cache)


def _row_max(rows):
    m = rows[0]
    for r in rows[1:]:
        m = jnp.maximum(m, r)
    return m


def _first_hit(rows, m):
    hits = []
    taken = None
    for r in rows:
        eq = r == m
        hit = eq if taken is None else jnp.logical_and(eq, jnp.logical_not(taken))
        taken = eq if taken is None else jnp.logical_or(taken, eq)
        hits.append(hit)
    return hits


def _route(s_rows, sel_rows):
    neg = jnp.float32(-jnp.inf)
    top1, top2, score = [], [], []
    for g in range(N_GROUPS):
        rows = sel_rows[g * EXPERTS_PER_GROUP:(g + 1) * EXPERTS_PER_GROUP]
        m1 = _row_max(rows)
        h1 = _first_hit(rows, m1)
        rest = [jnp.where(h, neg, r) for h, r in zip(h1, rows)]
        m2 = _row_max(rest)
        h2 = _first_hit(rest, m2)
        top1.append(h1)
        top2.append(h2)
        score.append(m1 + m2)
    best = _row_max(score)
    ghit = _first_hit(score, best)
    picked = []
    for g in range(N_GROUPS):
        for e in range(EXPERTS_PER_GROUP):
            picked.append(jnp.logical_and(ghit[g], jnp.logical_or(top1[g][e], top2[g][e])))
    total = None
    for p, s in zip(picked, s_rows):
        term = jnp.where(p, s, 0.0)
        total = term if total is None else total + term
    gates = [jnp.where(p, s / total, 0.0) for p, s in zip(picked, s_rows)]
    in_group = []
    for e in range(EXPERTS_PER_GROUP):
        row = gates[e]
        for g in range(1, N_GROUPS):
            row = row + gates[g * EXPERTS_PER_GROUP + e]
        in_group.append(row)
    group = jnp.zeros_like(best)
    for g in range(1, N_GROUPS):
        group = jnp.where(ghit[g], float(g), group)
    return in_group, group


def _oproj_kernel(m_ref, x_ref, gate_ref, wh_ref, wl_ref, g_ref, sc_ref, sh_ref, rw_ref, rb_ref,
                  x1_ref, h_ref, route_ref, *, precise):
    bs, tl, _ = x_ref.shape
    m = m_ref[...].reshape(bs * tl, D_MODEL)
    if precise:
        mix = _dot3(m, wh_ref[...], wl_ref[...])
    else:
        mix = _dot(m.astype(BF16), wh_ref[...])
    mix = mix.reshape(bs, tl, D_MODEL)
    x1 = x_ref[...] + gate_ref[...] * mix
    x1_ref[...] = x1
    h = _norm_mod(x1, g_ref[...], sc_ref[...], sh_ref[...])
    h_ref[...] = h
    logits = lax.dot_general(rw_ref[...], h.reshape(bs * tl, D_MODEL), _NT,
                             precision=lax.Precision.HIGHEST,
                             preferred_element_type=F32)
    s = _sigmoid(logits)
    sel = s + rb_ref[...]
    s_rows = [s[e:e + 1, :] for e in range(N_EXPERTS)]
    sel_rows = [sel[e:e + 1, :] for e in range(N_EXPERTS)]
    in_group, group = _route(s_rows, sel_rows)
    pad = [jnp.zeros_like(group)] * (SUBLANES - EXPERTS_PER_GROUP - 1)
    route_ref[0] = jnp.concatenate(in_group + [group] + pad, axis=0)


def _oproj_router(m, x, gate, w_pieces, g, sc, sh, router_wt, router_b, precise):
    seqs, length, _ = x.shape
    bs, tl = _seq_blocks(seqs, length)
    n_tiles = (seqs // bs) * (length // tl)
    per_seq = length // tl
    tok = pl.BlockSpec((bs, tl, D_MODEL), lambda s, i: (s, i, 0))
    mod = pl.BlockSpec((bs, 1, D_MODEL), lambda s, i: (s, 0, 0))
    one = pl.BlockSpec((1, 1, D_MODEL), lambda s, i: (0, 0, 0))
    wspec = pl.BlockSpec((D_MODEL, D_MODEL), lambda s, i: (0, 0))
    f32o = jax.ShapeDtypeStruct((seqs, length, D_MODEL), F32)
    x1, h, route = pl.pallas_call(
        functools.partial(_oproj_kernel, precise=precise),
        out_shape=(f32o, f32o, jax.ShapeDtypeStruct((n_tiles, SUBLANES, bs * tl), F32)),
        grid=(seqs // bs, per_seq),
        in_specs=[tok, tok, mod, wspec, wspec, one, mod, mod,
                  pl.BlockSpec((N_EXPERTS, D_MODEL), lambda s, i: (0, 0)),
                  pl.BlockSpec((N_EXPERTS, 1), lambda s, i: (0, 0))],
        out_specs=(tok, tok,
                   pl.BlockSpec((1, SUBLANES, bs * tl), lambda s, i: (s * per_seq + i, 0, 0))),
        compiler_params=_cparams("arbitrary", "arbitrary"),
        name="oproj_router",
    )(m, x, gate, *w_pieces, g.reshape(1, 1, D_MODEL), sc, sh, router_wt,
      router_b.reshape(N_EXPERTS, 1))
    route = jnp.swapaxes(route, 1, 2).reshape(seqs * length, SUBLANES)
    return x1, h, route[:, :EXPERTS_PER_GROUP], route[:, EXPERTS_PER_GROUP].astype(jnp.int32)


def _dispatch_plan(group, gates, tile):
    n = group.shape[0]
    n_slots = n + N_GROUPS * tile
    onehot = (group[:, None] == jnp.arange(N_GROUPS)[None, :]).astype(jnp.int32)
    counts = jnp.sum(onehot, axis=0)
    padded = (counts + tile - 1) // tile * tile
    ends = jnp.cumsum(padded)
    starts = ends - padded
    rank = jnp.take_along_axis(jnp.cumsum(onehot, axis=0) - onehot, group[:, None], axis=1)[:, 0]
    slot = starts[group] + rank
    src = jnp.zeros((n_slots,), jnp.int32).at[slot].set(jnp.arange(n, dtype=jnp.int32))
    slot_gates = jnp.zeros((n_slots, EXPERTS_PER_GROUP), F32).at[slot].set(gates)
    tile_start = jnp.arange(n_slots // tile, dtype=jnp.int32) * tile
    tile_group = jnp.minimum(jnp.sum(tile_start[:, None] >= ends[None, :], axis=1), N_GROUPS - 1)
    return src, slot.astype(jnp.int32), slot_gates, tile_group.astype(jnp.int32)


def _start_row_gather(idx_ref, base, src_hbm, dst_ref, sem):
    def body(i, _):
        for j in range(SUBLANES):
            r = i * SUBLANES + j
            row = idx_ref[base + r]
            pltpu.make_async_copy(src_hbm.at[pl.ds(row, 1), :], dst_ref.at[pl.ds(r, 1), :],
                                  sem).start(priority=j % 2)
        return 0

    lax.fori_loop(0, dst_ref.shape[0] // SUBLANES, body, 0)


def _wait_row_gather(src_hbm, dst_ref, sem):
    pltpu.make_async_copy(src_hbm.at[pl.ds(0, dst_ref.shape[0]), :], dst_ref, sem).wait()


def _moe_kernel(src_ref, tg_ref, h_hbm, gates_ref, wg_ref, wu_ref, wd_ref, o_ref, hbuf, sem):
    t = pl.program_id(0)
    tile = o_ref.shape[0]
    slot = t % 2

    @pl.when(t == 0)
    def _():
        _start_row_gather(src_ref, 0, h_hbm, hbuf.at[0], sem.at[0])

    @pl.when(t + 1 < pl.num_programs(0))
    def _():
        _start_row_gather(src_ref, (t + 1) * tile, h_hbm, hbuf.at[1 - slot], sem.at[1 - slot])

    _wait_row_gather(h_hbm, hbuf.at[slot], sem.at[slot])
    h = hbuf[slot].astype(BF16)
    gates = gates_ref[...]
    acc = jnp.zeros((tile, D_MODEL), F32)
    for e in range(EXPERTS_PER_GROUP):
        a = _dot(h, wg_ref[e])
        u = _dot(h, wu_ref[e])
        he = (a * _sigmoid(a) * u).astype(BF16)
        acc = acc + gates[:, e:e + 1] * _dot(he, wd_ref[e])
    o_ref[...] = acc


def _moe_experts(h, src, slot_gates, tile_group, wg_bf, wu_bf, wd_bf, tile):
    n_slots = src.shape[0]
    group_w = lambda shape: pl.BlockSpec((EXPERTS_PER_GROUP,) + shape, lambda t, src, tg: (tg[t], 0, 0))
    return pl.pallas_call(
        _moe_kernel,
        out_shape=jax.ShapeDtypeStruct((n_slots, D_MODEL), F32),
        grid_spec=pltpu.PrefetchScalarGridSpec(
            num_scalar_prefetch=2,
            grid=(n_slots // tile,),
            in_specs=[pl.BlockSpec(memory_space=pl.ANY),
                      pl.BlockSpec((tile, EXPERTS_PER_GROUP), lambda t, src, tg: (t, 0)),
                      group_w((D_MODEL, D_EXPERT)), group_w((D_MODEL, D_EXPERT)),
                      group_w((D_EXPERT, D_MODEL))],
            out_specs=pl.BlockSpec((tile, D_MODEL), lambda t, src, tg: (t, 0)),
            scratch_shapes=[pltpu.VMEM((2, tile, D_MODEL), F32), pltpu.SemaphoreType.DMA((2,))]),
        compiler_params=_cparams("arbitrary"),
        name="moe_experts",
    )(src, tile_group, h, slot_gates, wg_bf, wu_bf, wd_bf)


def _moe_combine_kernel(slot_ref, f_hbm, x_ref, gate_ref, fg_ref, o_ref, fbuf, sem, *, final_norm):
    bs, tl, _ = x_ref.shape
    tile = bs * tl
    t = pl.program_id(0) * pl.num_programs(1) + pl.program_id(1)
    n_tiles = pl.num_programs(0) * pl.num_programs(1)
    slot = t % 2

    @pl.when(t == 0)
    def _():
        _start_row_gather(slot_ref, 0, f_hbm, fbuf.at[0], sem.at[0])

    @pl.when(t + 1 < n_tiles)
    def _():
        _start_row_gather(slot_ref, (t + 1) * tile, f_hbm, fbuf.at[1 - slot], sem.at[1 - slot])

    _wait_row_gather(f_hbm, fbuf.at[slot], sem.at[slot])
    x2 = x_ref[...] + gate_ref[...] * fbuf[slot].reshape(bs, tl, D_MODEL)
    if final_norm:
        ms = jnp.mean(x2 * x2, axis=-1, keepdims=True)
        x2 = x2 * lax.rsqrt(ms + NORM_EPS) * fg_ref[...]
    o_ref[...] = x2


def _moe_combine(ffn_sorted, slot, x, gate, final_g, final_norm):
    seqs, length, _ = x.shape
    bs, tl = _seq_blocks(seqs, length)
    tok = pl.BlockSpec((bs, tl, D_MODEL), lambda s, i, slot: (s, i, 0))
    return pl.pallas_call(
        functools.partial(_moe_combine_kernel, final_norm=final_norm),
        out_shape=jax.ShapeDtypeStruct((seqs, length, D_MODEL), F32),
        grid_spec=pltpu.PrefetchScalarGridSpec(
            num_scalar_prefetch=1,
            grid=(seqs // bs, length // tl),
            in_specs=[pl.BlockSpec(memory_space=pl.ANY), tok,
                      pl.BlockSpec((bs, 1, D_MODEL), lambda s, i, slot: (s, 0, 0)),
                      pl.BlockSpec((1, 1, D_MODEL), lambda s, i, slot: (0, 0, 0))],
            out_specs=tok,
            scratch_shapes=[pltpu.VMEM((2, bs * tl, D_MODEL), F32), pltpu.SemaphoreType.DMA((2,))]),
        compiler_params=_cparams("arbitrary", "arbitrary"),
        name="moe_combine",
    )(slot, ffn_sorted, x, gate, final_g.reshape(1, 1, D_MODEL))


def _moe(h, group, gates, x, gate, final_g, wg_bf, wu_bf, wd_bf, final_norm):
    seqs, length, _ = x.shape
    tile = min(MOE_TILE, max(LANES, seqs * length // (2 * N_GROUPS)))
    src, slot, slot_gates, tile_group = _dispatch_plan(group, gates, tile)
    ffn_sorted = _moe_experts(h.reshape(seqs * length, D_MODEL), src, slot_gates, tile_group,
                              wg_bf, wu_bf, wd_bf, tile)
    return _moe_combine(ffn_sorted, slot, x, gate, final_g, final_norm)


def _head_selectors():
    r = lax.broadcasted_iota(jnp.int32, (D_MODEL, LANES), 0)
    c = lax.broadcasted_iota(jnp.int32, (D_MODEL, LANES), 1)
    sel = jnp.where(r // HEAD_DIM == c, 1.0, 0.0).astype(BF16)
    rt = lax.broadcasted_iota(jnp.int32, (LANES, D_MODEL), 0)
    ct = lax.broadcasted_iota(jnp.int32, (LANES, D_MODEL), 1)
    selt = jnp.where(ct // HEAD_DIM == rt, 1.0, 0.0).astype(BF16)
    return jnp.concatenate([sel, sel], axis=0), jnp.concatenate([selt, selt], axis=0)


def _head_sum(x, sel2, selt2):
    return _dot2(_dot2(x, sel2), selt2)


def _rwkv_pre_kernel(x_ref, prev_ref, shift_ref, g_ref, sc_ref, sh_ref, mu_ref,
                     wr_ref, wk_ref, wv_ref, w0_ref, w1_ref, w2_ref, a0_ref, a1_ref, a2_ref,
                     g1_ref, g2_ref, kk_ref, ka_ref,
                     r_ref, w_ref, k_ref, v_ref, na_ref, kb_ref, go_ref, so_ref):
    bs, tl, _ = x_ref.shape
    t = bs * tl
    i = pl.program_id(1)
    g, sc, sh = g_ref[...], sc_ref[...], sh_ref[...]
    h = _norm_mod(x_ref[...], g, sc, sh)
    so_ref[...] = h[:, tl - 1:tl, :]
    h_before = _norm_mod(prev_ref[:, 7:8, :], g, sc, sh)
    first = jnp.where(i == 0, shift_ref[...], h_before)
    row = lax.broadcasted_iota(jnp.int32, (bs, tl, D_MODEL), 1)
    prev = jnp.where(row == 0, first, pltpu.roll(h, 1, 1))
    h = h.reshape(t, D_MODEL)
    xx = prev.reshape(t, D_MODEL) - h
    mixed = [(h + xx * mu_ref[j:j + 1, :]).astype(BF16) for j in range(6)]
    xr, xw, xk, xv, xa, xg = mixed
    r = _dot(xr, wr_ref[...])
    k = _dot(xk, wk_ref[...])
    v = _dot(xv, wv_ref[...])
    wl = w0_ref[...] + _dot(jnp.tanh(_dot(xw, w1_ref[...])).astype(BF16), w2_ref[...])
    neg = -wl
    softplus = jnp.maximum(neg, 0.0) + jnp.log(1.0 + jnp.exp(-jnp.abs(neg)))
    w = -softplus - 0.5
    log_decay = -jnp.exp(w)
    a = _sigmoid(a0_ref[...] + _dot(_dot(xa, a1_ref[...]).astype(BF16), a2_ref[...]))
    gate = _dot(_sigmoid(_dot(xg, g1_ref[...])).astype(BF16), g2_ref[...])
    sel2, selt2 = _head_selectors()
    kk = k * kk_ref[...]
    norm = jnp.maximum(jnp.sqrt(_head_sum(kk * kk, sel2, selt2)), 1e-12)
    kk = kk / norm
    k = k * (1.0 + (a - 1.0) * ka_ref[...])
    shape = (bs, tl, D_MODEL)
    r_ref[...] = r.reshape(shape)
    w_ref[...] = log_decay.reshape(shape)
    k_ref[...] = k.reshape(shape)
    v_ref[...] = v.reshape(shape)
    na_ref[...] = (-kk).reshape(shape)
    kb_ref[...] = (kk * a).reshape(shape)
    go_ref[...] = gate.reshape(shape)


def _rwkv_pre(x, shift0, g, sc, sh, p):
    seqs, length, _ = x.shape
    bs, tl = _seq_blocks(seqs, length, RWKV_PRE_TILE)
    tok = pl.BlockSpec((bs, tl, D_MODEL), lambda s, i: (s, i, 0))
    prev = pl.BlockSpec((bs, 8, D_MODEL), lambda s, i: (s, jnp.maximum(i * (tl // 8) - 1, 0), 0))
    mod = pl.BlockSpec((bs, 1, D_MODEL), lambda s, i: (s, 0, 0))
    one = pl.BlockSpec((1, 1, D_MODEL), lambda s, i: (0, 0, 0))

    def full(a):
        return pl.BlockSpec(a.shape, lambda s, i: (0,) * a.ndim)

    vec = lambda a: a.reshape(1, D_MODEL)
    weights = [p['rw_mu'], p['rw_w_r'], p['rw_w_k'], p['rw_w_v'], vec(p['rw_w0']), p['rw_w1'],
               p['rw_w2'], vec(p['rw_a0']), p['rw_a1'], p['rw_a2'], p['rw_g1'], p['rw_g2'],
               vec(p['rw_k_k']), vec(p['rw_k_a'])]
    f32o = jax.ShapeDtypeStruct((seqs, length, D_MODEL), F32)
    outs = pl.pallas_call(
        _rwkv_pre_kernel,
        out_shape=(f32o,) * 7 + (jax.ShapeDtypeStruct((seqs, 1, D_MODEL), F32),),
        grid=(seqs // bs, length // tl),
        in_specs=[tok, prev, mod, one, mod, mod] + [full(a) for a in weights],
        out_specs=(tok,) * 7 + (mod,),
        compiler_params=_cparams("arbitrary", "arbitrary"),
        name="rwkv_pre",
    )(x, x, shift0, g.reshape(1, 1, D_MODEL), sc, sh, *weights)
    return outs


def _bdot(a, b, dims=None):
    a, b = a.astype(BF16), b.astype(BF16)
    if dims is None:
        return jnp.dot(a, b, preferred_element_type=F32)
    return lax.dot_general(a, b, dims, preferred_element_type=F32)


def _scan_kernel(r_ref, lw_ref, k_ref, v_ref, a_ref, b_ref, s0_ref, y_ref, st_ref, z_ref, *, chunk):
    tc = r_ref.shape[1]
    t2 = 2 * chunk
    c = pl.program_id(1)
    lane_a = _pair_masks(chunk)
    fold = lambda x: jnp.where(lane_a, x[:chunk], x[chunk:])
    stack = lambda x: jnp.concatenate([x, x], axis=0)
    split_heads = lambda x: _stack_heads(x, axis=0)

    def block_diag(mats):
        n = mats[0].shape[0]
        zero = jnp.zeros((n, n), F32)
        return jnp.concatenate(
            [jnp.concatenate([m if j == i else zero for j in range(len(mats))], axis=1)
             for i, m in enumerate(mats)], axis=0)

    @pl.when(c == 0)
    def _():
        for q in range(PAIRS // SCAN_GROUP):
            z_ref[q] = block_diag([s0_ref[0, 2 * SCAN_GROUP * q + h] for h in range(2 * SCAN_GROUP)])

    row = lax.broadcasted_iota(jnp.int32, (t2, t2), 0)
    col = lax.broadcasted_iota(jnp.int32, (t2, t2), 1)
    same_head = (row // chunk) == (col // chunk)
    strict = jnp.logical_and(same_head, (row % chunk) > (col % chunk))
    incl = jnp.logical_and(same_head, (row % chunk) >= (col % chunk))
    tr = lax.broadcasted_iota(jnp.int32, (chunk, chunk), 0)
    tcol = lax.broadcasted_iota(jnp.int32, (chunk, chunk), 1)
    tri = jnp.where(tr >= tcol, 1.0, 0.0).astype(BF16)
    tri2 = jnp.concatenate([tri, tri], axis=1)
    gw = SCAN_GROUP * LANES
    zr = lax.broadcasted_iota(jnp.int32, (gw, gw), 0)
    zc = lax.broadcasted_iota(jnp.int32, (gw, gw), 1)
    z_diag = (zr // HEAD_DIM) == (zc // HEAD_DIM)
    n_doublings = chunk.bit_length() - 1

    def one_chunk(ci, _):
        rows = pl.ds(pl.multiple_of(ci * chunk, chunk), chunk)
        lw, r, k, v, a, b = [ref[0, rows, :] for ref in (lw_ref, r_ref, k_ref, v_ref, a_ref, b_ref)]
        hi, lo = _split2(lw)
        cum = _dot(tri2, jnp.concatenate([hi, lo], axis=0))
        grow = jnp.exp(-cum)
        decay = jnp.exp(cum)
        total = decay[chunk - 1:chunk, :]
        a_t = a * jnp.exp(cum - lw)
        r_t = r * decay
        b_t = b * grow
        k_t = k * grow
        b_end = b_t * total
        k_end = k_t * total
        n_groups = PAIRS // SCAN_GROUP
        pair = lambda x, p: x[:, _pair_lanes(p)]
        grp = lambda x, q: x[:, q * gw:(q + 1) * gw]
        members = lambda q: range(SCAN_GROUP * q, SCAN_GROUP * (q + 1))
        per_group = lambda f: jnp.stack([f(q) for q in range(n_groups)])
        g = _bdot(jnp.stack([jnp.concatenate([split_heads(pair(a_t, p)), split_heads(pair(r_t, p))], axis=0)
                             for p in range(PAIRS)]),
                  jnp.stack([jnp.concatenate([stack(pair(b_t, p)), stack(pair(k_t, p))], axis=0)
                             for p in range(PAIRS)]), _BATCHED_NT)
        l_ab = jnp.where(strict, g[:, :t2, :t2], 0.0)
        l_ak = jnp.where(strict, g[:, :t2, t2:], 0.0)
        l_rb = jnp.where(incl, g[:, t2:, :t2], 0.0)
        l_rk = jnp.where(incl, g[:, t2:, t2:], 0.0)
        diag = lambda m: per_group(lambda q: block_diag([m[p] for p in members(q)]))
        z = z_ref[...]
        applied = _bdot(per_group(lambda q: jnp.concatenate([grp(a_t, q), grp(r_t, q)], axis=0)),
                        z, _BATCHED_NT)
        vv = per_group(lambda q: jnp.concatenate([stack(pair(v, p)) for p in members(q)], axis=0))
        x = per_group(lambda q: jnp.concatenate(
            [stack(applied[q, :chunk, j * LANES:(j + 1) * LANES]) for j in range(SCAN_GROUP)],
            axis=0)) + _bdot(diag(l_ak), vv, _BATCHED)
        power = diag(l_ab)
        for i in range(n_doublings):
            x = x + _bdot(power, x, _BATCHED)
            if i + 1 < n_doublings:
                power = _bdot(power, power, _BATCHED)
        unstack = lambda m, q: jnp.concatenate(
            [fold(m[q, j * t2:(j + 1) * t2]) for j in range(SCAN_GROUP)], axis=1)
        u = [unstack(x, q) for q in range(n_groups)]
        uu = per_group(lambda q: jnp.concatenate(
            [stack(u[q][:, j * LANES:(j + 1) * LANES]) for j in range(SCAN_GROUP)], axis=0))
        ys = _bdot(jnp.concatenate([diag(l_rb), diag(l_rk)], axis=2),
                   jnp.concatenate([uu, vv], axis=1), _BATCHED)
        y_ref[0, rows, :] = jnp.concatenate(
            [applied[q, chunk:] + unstack(ys, q) for q in range(n_groups)], axis=1)
        grown = _bdot(per_group(lambda q: jnp.concatenate([u[q], grp(v, q)], axis=0)),
                      per_group(lambda q: jnp.concatenate([grp(b_end, q), grp(k_end, q)], axis=0)),
                      _BATCHED_TN)
        z_ref[...] = (z * per_group(lambda q: grp(total, q)) + jnp.where(z_diag, grown, 0.0))
        return 0

    lax.fori_loop(0, tc // chunk, one_chunk, 0)

    @pl.when(c == pl.num_programs(1) - 1)
    def _():
        for q in range(PAIRS // SCAN_GROUP):
            z = z_ref[q]
            for h in range(2 * SCAN_GROUP):
                block = slice(h * HEAD_DIM, (h + 1) * HEAD_DIM)
                st_ref[0, 2 * SCAN_GROUP * q + h] = z[block, block]


def _scan(r, lw, k, v, na, kb, s0):
    seqs, length, _ = r.shape
    chunk = min(length, SCAN_CHUNK)
    tc = min(length, SCAN_BLOCK)
    assert length % tc == 0 and tc % chunk == 0 and chunk & (chunk - 1) == 0
    tok = pl.BlockSpec((1, tc, D_MODEL), lambda s, c: (s, c, 0))
    st = pl.BlockSpec((1, HEADS, HEAD_DIM, HEAD_DIM), lambda s, c: (s, 0, 0, 0))
    return pl.pallas_call(
        functools.partial(_scan_kernel, chunk=chunk),
        out_shape=(jax.ShapeDtypeStruct((seqs, length, D_MODEL), F32),
                   jax.ShapeDtypeStruct((seqs, HEADS, HEAD_DIM, HEAD_DIM), F32)),
        grid=(seqs, length // tc),
        in_specs=[tok] * 6 + [st],
        out_specs=(tok, st),
        scratch_shapes=[pltpu.VMEM((PAIRS // SCAN_GROUP, SCAN_GROUP * LANES, SCAN_GROUP * LANES), F32)],
        compiler_params=_cparams("arbitrary", "arbitrary"),
        name="rwkv_scan",
    )(r, lw, k, v, na, kb, s0)


def _rwkv_post_kernel(y_ref, r_ref, k_ref, v_ref, g_ref, lg_ref, lb_ref, rk_ref, o_ref):
    bs, tl, _ = y_ref.shape
    t = bs * tl
    sel2, selt2 = _head_selectors()
    y = y_ref[...].reshape(t, D_MODEL)
    mu = _head_sum(y, sel2, selt2) * (1.0 / HEAD_DIM)
    d = y - mu
    var = _head_sum(d * d, sel2, selt2) * (1.0 / HEAD_DIM)
    yn = d * lax.rsqrt(var + LNX_EPS) * lg_ref[...] + lb_ref[...]
    rk = r_ref[...].reshape(t, D_MODEL) * k_ref[...].reshape(t, D_MODEL) * rk_ref[...]
    bonus = _head_sum(rk, sel2, selt2) * v_ref[...].reshape(t, D_MODEL)
    o_ref[...] = ((yn + bonus) * g_ref[...].reshape(t, D_MODEL)).reshape(bs, tl, D_MODEL)


def _rwkv_post(y, r, k, v, gate, lnx_g, lnx_b, r_k):
    seqs, length, _ = y.shape
    bs, tl = _seq_blocks(seqs, length)
    tok = pl.BlockSpec((bs, tl, D_MODEL), lambda s, i: (s, i, 0))
    vec = pl.BlockSpec((1, D_MODEL), lambda s, i: (0, 0))
    return pl.pallas_call(
        _rwkv_post_kernel,
        out_shape=jax.ShapeDtypeStruct((seqs, length, D_MODEL), F32),
        grid=(seqs // bs, length // tl),
        in_specs=[tok] * 5 + [vec] * 3,
        out_specs=tok,
        compiler_params=_cparams("arbitrary", "arbitrary"),
        name="rwkv_post",
    )(y, r, k, v, gate, lnx_g.reshape(1, D_MODEL), lnx_b.reshape(1, D_MODEL),
      r_k.reshape(1, D_MODEL))


def _trunk(x, mod, p, wb, k_cache, v_cache, shift0, s0):
    seqs, length, _ = x.shape
    part = lambda layer, j: mod[layer, :, j * D_MODEL:(j + 1) * D_MODEL].reshape(seqs, 1, D_MODEL)
    router_wt = p['router_w'].T
    sh1, sc1, g1, sh2, sc2, g2 = [part(0, j) for j in range(6)]
    k, v, qh, ql, kh, kl, vh, vl = _qkv(x, p['norm_mix_g'][0], sc1, sh1, *wb['sb_w_qkv'])
    if k_cache is None:
        attn = _attn_prompt((qh, ql), (kh, kl), (vh, vl))
    else:
        past = k_cache.shape[1]
        attn = _attn_sample((qh, ql), (kh, kl), (vh, vl), k_cache.reshape(seqs, past, D_MODEL),
                            v_cache.reshape(seqs, past, D_MODEL))
    x, h, gates, group = _oproj_router(attn, x, g1, wb['sb_w_o'], p['norm_ffn_g'][0], sc2, sh2,
                                       router_wt, p['router_bias'], precise=True)
    x = _moe(h, group, gates, x, g2, p['final_norm_g'], wb['exp_w_gate'][0], wb['exp_w_up'][0],
             wb['exp_w_down'][0], final_norm=False)
    sb_k = k.reshape(seqs, length, HEADS, HEAD_DIM)
    sb_v = v.reshape(seqs, length, HEADS, HEAD_DIM)
    sh1, sc1, g1, sh2, sc2, g2 = [part(1, j) for j in range(6)]
    r, w, kq, vv, na, kb, gate, shift = _rwkv_pre(x, shift0, p['norm_mix_g'][1], sc1, sh1, wb)
    y, state = _scan(r, w, kq, vv, na, kb, s0)
    m = _rwkv_post(y, r, kq, vv, gate, p['rw_lnx_g'], p['rw_lnx_b'], p['rw_r_k'])
    x, h, gates, group = _oproj_router(m, x, g1, wb['rw_w_o'], p['norm_ffn_g'][1], sc2, sh2,
                                       router_wt, p['router_bias'], precise=False)
    y_out = _moe(h, group, gates, x, g2, p['final_norm_g'], wb['exp_w_gate'][1], wb['exp_w_up'][1],
                 wb['exp_w_down'][1], final_norm=True)
    return y_out, sb_k, sb_v, state, shift


def kernel(x_prompt, x_sample, cache_sb_k, cache_sb_v, state_rw_wkv, state_rw_shift,
           c_prompt, c_sample, ada_w, ada_b, norm_mix_g, norm_ffn_g, final_norm_g,
           sb_w_qkv, sb_w_o, rw_mu, rw_w_r, rw_w_k, rw_w_v, rw_w0, rw_w1, rw_w2,
           rw_a0, rw_a1, rw_a2, rw_g1, rw_g2, rw_k_k, rw_k_a, rw_r_k, rw_lnx_g, rw_lnx_b,
           rw_w_o, router_w, router_bias, exp_w_gate, exp_w_up, exp_w_down):
    p = {
        'norm_mix_g': norm_mix_g, 'norm_ffn_g': norm_ffn_g, 'final_norm_g': final_norm_g,
        'rw_r_k': rw_r_k, 'rw_lnx_g': rw_lnx_g, 'rw_lnx_b': rw_lnx_b,
        'router_w': router_w, 'router_bias': router_bias,
    }
    bf = lambda a: a.astype(BF16)
    wb = {
        'sb_w_qkv': _split_weight(sb_w_qkv), 'sb_w_o': _split_weight(sb_w_o),
        'rw_w_o': _split_weight(rw_w_o),
        'rw_mu': rw_mu, 'rw_w_r': bf(rw_w_r), 'rw_w_k': bf(rw_w_k), 'rw_w_v': bf(rw_w_v),
        'rw_w0': rw_w0, 'rw_w1': bf(rw_w1), 'rw_w2': bf(rw_w2), 'rw_a0': rw_a0,
        'rw_a1': bf(rw_a1), 'rw_a2': bf(rw_a2), 'rw_g1': bf(rw_g1), 'rw_g2': bf(rw_g2),
        'rw_k_k': rw_k_k, 'rw_k_a': rw_k_a,
        'exp_w_gate': bf(exp_w_gate), 'exp_w_up': bf(exp_w_up), 'exp_w_down': bf(exp_w_down),
    }
    n_prompt = x_prompt.shape[0]
    mod = _ada(jnp.concatenate([c_prompt, c_sample], axis=0), ada_w, ada_b)
    shift0 = jnp.zeros((n_prompt, 1, D_MODEL), x_prompt.dtype)
    s0 = jnp.zeros((n_prompt, HEADS, HEAD_DIM, HEAD_DIM), F32)
    y_p, k_p, v_p, st_p, sh_p = _trunk(x_prompt, mod[:, :n_prompt], p, wb, None, None, shift0, s0)
    y_s, k_s, v_s, st_s, sh_s = _trunk(x_sample, mod[:, n_prompt:], p, wb, cache_sb_k, cache_sb_v,
                                       state_rw_shift, state_rw_wkv)
    return (y_p, y_s, k_p, v_p, k_s, v_s, st_p, sh_p, st_s, sh_s)
```
